```python
import math
import jax, jax.numpy as jnp
from jax import lax
import numpy as np

D_MODEL = 2048
BATCH = 4
SEQ = 4096
DEPTH = 2

HEAD_DIM = 128
MLA_HEADS = 6
MLA_Q_LORA = 512
MLA_KV_LORA = 512
MLA_NOPE = 128
MLA_ROPE = 64
MLA_V = 128
MOBA_HEADS = 5
MOBA_BLOCK = 256
MOBA_TOPK = 3
MOBA_QCHUNK = 32
SB_HEADS = 5
N_BRANCH = 3
ATTN_QBLK = 128
ROPE_THETA = 10000.0
MEM_LEN = 256
XATTN_HEADS = 4
XATTN_HEAD_DIM = D_MODEL // XATTN_HEADS
D_FF = 5632
N_EXPERTS = 8
MOE_TOPK = 2
D_FF_EXPERT = 7168
MOE_BLOCK = 256
RMS_EPS = 1e-6
MAX_POS_OFFSET = 1024
IN_WIDTHS = (MLA_Q_LORA, MLA_KV_LORA, MLA_ROPE, 3 * MOBA_HEADS * HEAD_DIM, 3 * SB_HEADS * HEAD_DIM, N_BRANCH * D_MODEL)
IN_COLS = sum(IN_WIDTHS)

kernel_name = 'hybrid_mla_moba_stickbreaking_moe_block'


def rms_norm(x, g):
    xf = x.astype(jnp.float32)
    y = xf * lax.rsqrt(jnp.mean(xf * xf, axis=-1, keepdims=True) + RMS_EPS)
    return (y * g.astype(jnp.float32)).astype(x.dtype)


def rope(x, positions):
    half = x.shape[-1] // 2
    inv_freq = ROPE_THETA ** (-jnp.arange(half, dtype=jnp.float32) / half)
    ang = positions.astype(jnp.float32)[..., None] * inv_freq
    cos = jnp.cos(ang)[:, :, None, :].astype(x.dtype)
    sin = jnp.sin(ang)[:, :, None, :].astype(x.dtype)
    x1, x2 = x[..., :half], x[..., half:]
    return jnp.concatenate([x1 * cos - x2 * sin, x2 * cos + x1 * sin], axis=-1)


def to_query_blocks(t, blk):
    B, S = t.shape[:2]
    return jnp.moveaxis(t.reshape((B, S // blk, blk) + t.shape[2:]), 1, 0)


def from_query_blocks(t):
    n, B, blk = t.shape[:3]
    return jnp.moveaxis(t, 0, 1).reshape((B, n * blk) + t.shape[3:])


def causal_softmax_attention(q, k, v, scale):
    S = q.shape[1]
    kpos = jnp.arange(S)

    def block(args):
        qb, i = args
        s = jnp.einsum('bqhd,bkhd->bhqk', qb, k, preferred_element_type=jnp.float32) * scale
        qpos = i * ATTN_QBLK + jnp.arange(ATTN_QBLK)
        s = jnp.where(kpos[None, :] <= qpos[:, None], s, -jnp.inf)
        p = jax.nn.softmax(s, axis=-1).astype(v.dtype)
        return jnp.einsum('bhqk,bkhd->bqhd', p, v)

    out = lax.map(block, (to_query_blocks(q, ATTN_QBLK), jnp.arange(S // ATTN_QBLK)))
    return from_query_blocks(out)


def stick_breaking_attention(q, k, v):
    S, d = q.shape[1], q.shape[-1]
    scale = d ** -0.5
    kpos = jnp.arange(S)

    def block(args):
        qb, i = args
        z = jnp.einsum('bqhd,bkhd->bhqk', qb, k, preferred_element_type=jnp.float32) * scale
        qpos = i * ATTN_QBLK + jnp.arange(ATTN_QBLK)
        strict = kpos[None, :] < qpos[:, None]
        log_one_minus_beta = jnp.where(strict, jax.nn.log_sigmoid(-z), 0.0)
        later = lax.cumsum(log_one_minus_beta, axis=3, reverse=True) - log_one_minus_beta
        a = jnp.where(strict, jnp.exp(jax.nn.log_sigmoid(z) + later), 0.0).astype(v.dtype)
        return jnp.einsum('bhqk,bkhd->bqhd', a, v)

    out = lax.map(block, (to_query_blocks(q, ATTN_QBLK), jnp.arange(S // ATTN_QBLK)))
    return from_query_blocks(out)


def moba_attention(q, k, v):
    B, S, H, d = q.shape
    scale = d ** -0.5
    nb = -(-S // MOBA_BLOCK)
    pad = nb * MOBA_BLOCK - S
    padw = ((0, 0), (0, pad), (0, 0), (0, 0))
    kt = jnp.pad(k, padw).transpose(0, 2, 1, 3).reshape(B, H, nb, MOBA_BLOCK, d)
    vt = jnp.pad(v, padw).transpose(0, 2, 1, 3).reshape(B, H, nb, MOBA_BLOCK, d)
    qt = q.transpose(0, 2, 1, 3)
    k_mean = jnp.mean(kt, axis=3)
    gate = jnp.einsum('bhsd,bhnd->bhsn', qt, k_mean, preferred_element_type=jnp.float32)
    q_block = jnp.arange(S) // MOBA_BLOCK
    fully_past = jnp.arange(nb)[None, :] < q_block[:, None]
    gate = jnp.where(fully_past, gate, -jnp.inf)
    n_sel = min(MOBA_TOPK, nb)
    _, sel = lax.top_k(gate, n_sel)
    n_chunks = S // MOBA_QCHUNK
    q_chunks = jnp.moveaxis(qt.reshape(B, H, n_chunks, MOBA_QCHUNK, d), 2, 0)
    sel_chunks = jnp.moveaxis(sel.reshape(B, H, n_chunks, MOBA_QCHUNK, n_sel), 2, 0)
    b_idx = jnp.arange(B)[:, None, None, None]
    h_idx = jnp.arange(H)[None, :, None, None]
    kpos_blk = jnp.arange(MOBA_BLOCK)
    n_sel_keys = n_sel * MOBA_BLOCK

    def chunk(args):
        qc, sc, c = args
        q0 = c * MOBA_QCHUNK
        own = q0 // MOBA_BLOCK
        qpos = q0 + jnp.arange(MOBA_QCHUNK)
        k_sel = kt[b_idx, h_idx, sc]
        v_sel = vt[b_idx, h_idx, sc]
        s_sel = jnp.einsum('bhqd,bhqnkd->bhqnk', qc, k_sel, preferred_element_type=jnp.float32) * scale
        s_sel = jnp.where((sc < own)[..., None], s_sel, -jnp.inf)
        k_own = lax.dynamic_index_in_dim(kt, own, axis=2, keepdims=False)
        v_own = lax.dynamic_index_in_dim(vt, own, axis=2, keepdims=False)
        s_own = jnp.einsum('bhqd,bhkd->bhqk', qc, k_own, preferred_element_type=jnp.float32) * scale
        s_own = jnp.where(own * MOBA_BLOCK + kpos_blk[None, :] <= qpos[:, None], s_own, -jnp.inf)
        s_all = jnp.concatenate([s_sel.reshape(B, H, MOBA_QCHUNK, n_sel_keys), s_own], axis=-1)
        p = jax.nn.softmax(s_all, axis=-1).astype(v.dtype)
        p_sel = p[..., :n_sel_keys].reshape(B, H, MOBA_QCHUNK, n_sel, MOBA_BLOCK)
        return (jnp.einsum('bhqnk,bhqnkd->bhqd', p_sel, v_sel)
                + jnp.einsum('bhqk,bhkd->bhqd', p[..., n_sel_keys:], v_own))

    out = lax.map(chunk, (q_chunks, sel_chunks, jnp.arange(n_chunks)))
    return jnp.moveaxis(out, 0, 2).reshape(B, H, S, d).transpose(0, 2, 1, 3)


def hybrid_mixer(h, positions, w_in, g_cq, g_ckv, w_uq, w_ukv, w_up_a, w_up_b, w_up_c, w_o):
    B, S, D = h.shape
    splits = [int(s) for s in np.cumsum(IN_WIDTHS)[:-1]]
    z_cq, z_ckv, z_kr, z_b, z_c, z_g = jnp.split(h @ w_in, splits, axis=-1)
    q_a = (rms_norm(z_cq, g_cq) @ w_uq).reshape(B, S, MLA_HEADS, MLA_NOPE + MLA_ROPE)
    kv_a = (rms_norm(z_ckv, g_ckv) @ w_ukv).reshape(B, S, MLA_HEADS, MLA_NOPE + MLA_V)
    k_pe = rope(z_kr[:, :, None, :], positions)
    q_a = jnp.concatenate([q_a[..., :MLA_NOPE], rope(q_a[..., MLA_NOPE:], positions)], axis=-1)
    k_a = jnp.concatenate([kv_a[..., :MLA_NOPE], jnp.broadcast_to(k_pe, (B, S, MLA_HEADS, MLA_ROPE))], axis=-1)
    o_a = causal_softmax_attention(q_a, k_a, kv_a[..., MLA_NOPE:], (MLA_NOPE + MLA_ROPE) ** -0.5)
    zb = z_b.reshape(B, S, 3, MOBA_HEADS, HEAD_DIM)
    o_b = moba_attention(rope(zb[:, :, 0], positions), rope(zb[:, :, 1], positions), zb[:, :, 2])
    zc = z_c.reshape(B, S, 3, SB_HEADS, HEAD_DIM)
    o_c = stick_breaking_attention(zc[:, :, 0], zc[:, :, 1], zc[:, :, 2])
    g = jax.nn.sigmoid(z_g).reshape(B, S, N_BRANCH, D)
    merged = (g[:, :, 0] * (o_a.reshape(B, S, -1) @ w_up_a)
              + g[:, :, 1] * (o_b.reshape(B, S, -1) @ w_up_b)
              + g[:, :, 2] * (o_c.reshape(B, S, -1) @ w_up_c))
    return merged @ w_o


def memory_cross_attention(h, m, w_q, w_k, w_v, w_o):
    B, S, D = h.shape
    M = m.shape[1]
    q = (h @ w_q).reshape(B, S, XATTN_HEADS, XATTN_HEAD_DIM)
    k = (m @ w_k).reshape(B, M, XATTN_HEADS, XATTN_HEAD_DIM)
    v = (m @ w_v).reshape(B, M, XATTN_HEADS, XATTN_HEAD_DIM)
    s = jnp.einsum('bshd,bmhd->bhsm', q, k, preferred_element_type=jnp.float32) * XATTN_HEAD_DIM ** -0.5
    p = jax.nn.softmax(s, axis=-1).astype(v.dtype)
    o = jnp.einsum('bhsm,bmhd->bshd', p, v).reshape(B, S, D)
    return o @ w_o


def swiglu(h, w1, w3, w2):
    return (jax.nn.silu(h @ w1) * (h @ w3)) @ w2


def moe_swiglu(h, w_router, w_e1, w_e3, w_e2):
    B, S, D = h.shape
    T = B * S
    TK = T * MOE_TOPK
    ht = h.reshape(T, D)
    logits = jnp.matmul(ht, w_router, preferred_element_type=jnp.float32)
    top_val, top_idx = lax.top_k(logits, MOE_TOPK)
    gate = jax.nn.softmax(top_val, axis=-1)
    flat_e = top_idx.reshape(-1)
    flat_t = jnp.arange(TK, dtype=jnp.int32) // MOE_TOPK
    flat_g = gate.reshape(-1)
    order = jnp.argsort(flat_e)
    se, st, sg = flat_e[order], flat_t[order], flat_g[order]
    counts = jnp.bincount(flat_e, length=N_EXPERTS)
    padded = (counts + MOE_BLOCK - 1) // MOE_BLOCK * MOE_BLOCK
    start = jnp.cumsum(counts) - counts
    pad_end = jnp.cumsum(padded)
    pad_start = pad_end - padded
    dest = pad_start[se] + jnp.arange(TK) - start[se]
    n_blocks = -(-TK // MOE_BLOCK) + N_EXPERTS
    n_slots = n_blocks * MOE_BLOCK
    slot_tok = jnp.full((n_slots,), T, jnp.int32).at[dest].set(st)
    slot_g = jnp.zeros((n_slots,), jnp.float32).at[dest].set(sg)
    block_e = jnp.minimum(jnp.searchsorted(pad_end, jnp.arange(n_blocks) * MOE_BLOCK, side='right'), N_EXPERTS - 1)
    h_pad = jnp.concatenate([ht, jnp.zeros((1, D), ht.dtype)], axis=0)
    xs = h_pad[slot_tok].reshape(n_blocks, MOE_BLOCK, D)

    def expert_block(args):
        xb, e = args
        return (jax.nn.silu(xb @ w_e1[e]) * (xb @ w_e3[e])) @ w_e2[e]

    ys = lax.map(expert_block, (xs, block_e)).reshape(n_slots, D)
    ys = ys * slot_g[:, None].astype(ys.dtype)
    out = jnp.zeros((T + 1, D), ys.dtype).at[slot_tok].add(ys)[:T]
    return out.reshape(B, S, D)


def setup_inputs(seed: int = 0) -> dict:
    key = jax.random.key(seed)
    ks = jax.random.split(key, 40)
    f32 = jnp.float32
    D = D_MODEL
    n_dense = (DEPTH + 1) // 2
    n_moe = DEPTH // 2

    def w(k, shape, fan_in):
        return jax.random.normal(k, shape, f32) * fan_in ** -0.5

    def gain(k, shape):
        return 1.0 + 0.02 * jax.random.normal(k, shape, f32)

    x = jax.random.normal(ks[0], (BATCH, SEQ, D), f32)
    mem = jax.random.normal(ks[1], (BATCH, MEM_LEN, D), f32)
    positions = (jnp.arange(SEQ, dtype=jnp.int32)[None, :]
                 + jax.random.randint(ks[2], (BATCH, 1), 0, MAX_POS_OFFSET, dtype=jnp.int32))
    return {
        'x': x,
        'mem': mem,
        'positions': positions,
        'g_mix': gain(ks[3], (DEPTH, D)),
        'w_in': w(ks[4], (DEPTH, D, IN_COLS), D),
        'g_cq': gain(ks[5], (DEPTH, MLA_Q_LORA)),
        'g_ckv': gain(ks[6], (DEPTH, MLA_KV_LORA)),
        'w_uq': w(ks[7], (DEPTH, MLA_Q_LORA, MLA_HEADS * (MLA_NOPE + MLA_ROPE)), MLA_Q_LORA),
        'w_ukv': w(ks[8], (DEPTH, MLA_KV_LORA, MLA_HEADS * (MLA_NOPE + MLA_V)), MLA_KV_LORA),
        'w_up_a': w(ks[9], (DEPTH, MLA_HEADS * MLA_V, D), MLA_HEADS * MLA_V),
        'w_up_b': w(ks[10], (DEPTH, MOBA_HEADS * HEAD_DIM, D), MOBA_HEADS * HEAD_DIM),
        'w_up_c': w(ks[11], (DEPTH, SB_HEADS * HEAD_DIM, D), SB_HEADS * HEAD_DIM),
        'w_o': w(ks[12], (DEPTH, D, D), D),
        'g_x': gain(ks[13], (DEPTH, D)),
        'g_mem': gain(ks[14], (DEPTH, D)),
        'w_xq': w(ks[15], (DEPTH, D, D), D),
        'w_xk': w(ks[16], (DEPTH, D, D), D),
        'w_xv': w(ks[17], (DEPTH, D, D), D),
        'w_xo': w(ks[18], (DEPTH, D, D), D),
        'g_ffn': gain(ks[19], (DEPTH, D)),
        'w_ff1': w(ks[20], (n_dense, D, D_FF), D),
        'w_ff3': w(ks[21], (n_dense, D, D_FF), D),
        'w_ff2': w(ks[22], (n_dense, D_FF, D), D_FF),
        'w_router': w(ks[23], (n_moe, D, N_EXPERTS), D),
        'w_e1': w(ks[24], (n_moe, N_EXPERTS, D, D_FF_EXPERT), D),
        'w_e3': w(ks[25], (n_moe, N_EXPERTS, D, D_FF_EXPERT), D),
        'w_e2': w(ks[26], (n_moe, N_EXPERTS, D_FF_EXPERT, D), D_FF_EXPERT),
        'g_final': gain(ks[27], (D,)),
    }


def reference(x, mem, positions, g_mix, w_in, g_cq, g_ckv, w_uq, w_ukv, w_up_a, w_up_b, w_up_c, w_o,
              g_x, g_mem, w_xq, w_xk, w_xv, w_xo, g_ffn, w_ff1, w_ff3, w_ff2,
              w_router, w_e1, w_e3, w_e2, g_final):
    h = x
    for l in range(DEPTH):
        h = h + hybrid_mixer(rms_norm(h, g_mix[l]), positions, w_in[l], g_cq[l], g_ckv[l], w_uq[l], w_ukv[l],
                             w_up_a[l], w_up_b[l], w_up_c[l], w_o[l])
        h = h + memory_cross_attention(rms_norm(h, g_x[l]), rms_norm(mem, g_mem[l]),
                                       w_xq[l], w_xk[l], w_xv[l], w_xo[l])
        hn = rms_norm(h, g_ffn[l])
        if l % 2 == 0:
            h = h + swiglu(hn, w_ff1[l // 2], w_ff3[l // 2], w_ff2[l // 2])
        else:
            h = h + moe_swiglu(hn, w_router[l // 2], w_e1[l // 2], w_e3[l // 2], w_e2[l // 2])
    return rms_norm(h, g_final)
```

```python
import functools

import numpy as np
import jax
import jax.numpy as jnp
from jax import lax
from jax.experimental import pallas as pl
from jax.experimental.pallas import tpu as pltpu

F32 = jnp.float32
BF16 = jnp.bfloat16

HEAD_DIM = 128
MLA_HEADS = 6
MLA_LORA = 512
MLA_NOPE = 128
MLA_ROPE = 64
MLA_V = 128
MOBA_HEADS = 5
MOBA_BLOCK = 256
MOBA_TOPK = 3
SB_HEADS = 5
N_BRANCH = 3
ROPE_THETA = 10000.0
XATTN_HEADS = 4
N_EXPERTS = 8
MOE_TOPK = 2
RMS_EPS = 1e-6

LANES = 128
V7X_VMEM_BYTES = 64 * 1024 * 1024
VMEM_CEILING = V7X_VMEM_BYTES - 8 * 1024 * 1024

MASK_VALUE = -1e30

MOE_TILE = 512
GATHER_ROWS = 256


def _params(sem, est_bytes):
    limit = int(min(VMEM_CEILING, max(32 * 1024 * 1024, est_bytes * 5 // 4)))
    return pltpu.CompilerParams(dimension_semantics=sem, vmem_limit_bytes=limit)


def _nbytes(shape, dtype):
    return int(np.prod(shape)) * jnp.dtype(dtype).itemsize


def _rms(x, g):
    return x * lax.rsqrt(jnp.mean(x * x, axis=-1, keepdims=True) + RMS_EPS) * g


def _rmsnorm_kernel(x_ref, g_ref, o_ref):
    o_ref[...] = _rms(x_ref[...].astype(F32), g_ref[...]).astype(o_ref.dtype)


def rmsnorm(x, g, out_dtype, tm=512):
    m, d = x.shape
    tm = min(tm, m)
    return pl.pallas_call(
        _rmsnorm_kernel,
        grid=(m // tm,),
        in_specs=[pl.BlockSpec((tm, d), lambda i: (i, 0)),
                  pl.BlockSpec((1, d), lambda i: (0, 0))],
        out_specs=pl.BlockSpec((tm, d), lambda i: (i, 0)),
        out_shape=jax.ShapeDtypeStruct((m, d), out_dtype),
        compiler_params=_params(("parallel",), 4 * _nbytes((tm, d), F32)),
    )(x, g.reshape(1, d).astype(F32))


def _mm_kernel(a_ref, w_ref, o_ref):
    o_ref[...] = jnp.dot(a_ref[...], w_ref[...], preferred_element_type=F32).astype(o_ref.dtype)


def matmul(a, w, out_dtype, tm=1024, tn=1024):
    m, k = a.shape
    n = w.shape[1]
    tm, tn = min(tm, m), min(tn, n)
    assert m % tm == 0 and n % tn == 0, (m, n, tm, tn)
    est = 2 * (_nbytes((tm, k), a.dtype) + _nbytes((k, tn), w.dtype) + _nbytes((tm, tn), out_dtype)) \
        + _nbytes((tm, tn), F32)
    return pl.pallas_call(
        _mm_kernel,
        grid=(m // tm, n // tn),
        in_specs=[pl.BlockSpec((tm, k), lambda i, j: (i, 0)),
                  pl.BlockSpec((k, tn), lambda i, j: (0, j))],
        out_specs=pl.BlockSpec((tm, tn), lambda i, j: (i, j)),
        out_shape=jax.ShapeDtypeStruct((m, n), out_dtype),
        compiler_params=_params(("parallel", "parallel"), est),
    )(a, w)


def _mm_res_norm_kernel(a_ref, w_ref, r_ref, g_ref, h_ref, hn_ref):
    h = r_ref[...] + jnp.dot(a_ref[...], w_ref[...], preferred_element_type=F32)
    h_ref[...] = h
    hn_ref[...] = _rms(h, g_ref[...]).astype(hn_ref.dtype)


def matmul_res_norm(a, w, res, g, tm=512):
    m, k = a.shape
    n = w.shape[1]
    est = 2 * (_nbytes((tm, k), a.dtype) + _nbytes((k, n), w.dtype) + 2 * _nbytes((tm, n), F32)
               + _nbytes((tm, n), BF16)) + 2 * _nbytes((tm, n), F32)
    return pl.pallas_call(
        _mm_res_norm_kernel,
        grid=(m // tm,),
        in_specs=[pl.BlockSpec((tm, k), lambda i: (i, 0)),
                  pl.BlockSpec((k, n), lambda i: (0, 0)),
                  pl.BlockSpec((tm, n), lambda i: (i, 0)),
                  pl.BlockSpec((1, n), lambda i: (0, 0))],
        out_specs=[pl.BlockSpec((tm, n), lambda i: (i, 0)),
                   pl.BlockSpec((tm, n), lambda i: (i, 0))],
        out_shape=[jax.ShapeDtypeStruct((m, n), F32), jax.ShapeDtypeStruct((m, n), BF16)],
        compiler_params=_params(("parallel",), est),
    )(a, w, res, g.reshape(1, n).astype(F32))


def _half_swap(y, c, s):
    return y * c + pltpu.roll(y, LANES // 2, 1) * s


def _mla_prep_kernel(z_ref, gq_ref, gkv_ref, wq_ref, wkv_ref, c_ref, s_ref, q_ref, k_ref, v_ref, *, scale):
    z = z_ref[...].astype(F32)
    c = c_ref[...]
    s = s_ref[...]
    nq = _rms(z[:, :MLA_LORA], gq_ref[...]).astype(BF16)
    nkv = _rms(z[:, MLA_LORA:2 * MLA_LORA], gkv_ref[...]).astype(BF16)
    q = jnp.dot(nq, wq_ref[...], preferred_element_type=F32)
    kv = jnp.dot(nkv, wkv_ref[...], preferred_element_type=F32)
    k_pe = _half_swap(z[:, 2 * MLA_LORA:], c, s).astype(BF16)
    for h in range(MLA_HEADS):
        lo = 2 * LANES * h
        q_ref[:, lo:lo + LANES] = (q[:, lo:lo + LANES] * scale).astype(BF16)
        q_ref[:, lo + LANES:lo + 2 * LANES] = (_half_swap(q[:, lo + LANES:lo + 2 * LANES], c, s) * scale).astype(BF16)
        k_ref[:, lo:lo + LANES] = kv[:, LANES * h:LANES * (h + 1)].astype(BF16)
        k_ref[:, lo + LANES:lo + 2 * LANES] = k_pe
    v_ref[...] = kv[:, MLA_HEADS * MLA_NOPE:].astype(BF16)


def mla_prep(z1, g_cq, g_ckv, wq, wkv, cos_t, sin_t, tm=512):
    m, zc = z1.shape
    nq, nkv = wq.shape[1], wkv.shape[1]
    scale = float((MLA_NOPE + MLA_ROPE) ** -0.5)
    row = lambda c: pl.BlockSpec((tm, c), lambda i: (i, 0))
    full = lambda a: pl.BlockSpec(a.shape, lambda i: (0, 0))
    g_cq = g_cq.reshape(1, -1).astype(F32)
    g_ckv = g_ckv.reshape(1, -1).astype(F32)
    est = 4 * _nbytes((tm, nq), F32) + 4 * (_nbytes(wq.shape, BF16) + _nbytes(wkv.shape, BF16))
    return pl.pallas_call(
        functools.partial(_mla_prep_kernel, scale=scale),
        grid=(m // tm,),
        in_specs=[row(zc), full(g_cq), full(g_ckv), full(wq), full(wkv), row(LANES), row(LANES)],
        out_specs=[row(nq), row(nq), row(MLA_HEADS * MLA_V)],
        out_shape=[jax.ShapeDtypeStruct((m, nq), BF16), jax.ShapeDtypeStruct((m, nq), BF16),
                   jax.ShapeDtypeStruct((m, MLA_HEADS * MLA_V), BF16)],
        compiler_params=_params(("parallel",), est),
    )(z1, g_cq, g_ckv, wq, wkv, cos_t, sin_t)


def _nt_dot(a, b):
    return lax.dot_general(a, b, (((1,), (1,)), ((), ())), preferred_element_type=F32)


def _softmax_step(s, v, m, l, acc):
    m_new = jnp.maximum(m, jnp.max(s, axis=1, keepdims=True))
    alpha = jnp.exp(m - m_new)
    p = jnp.exp(s - m_new)
    l = alpha * l + jnp.sum(p, axis=1, keepdims=True)
    acc = alpha * acc + jnp.dot(p.astype(BF16), v, preferred_element_type=F32)
    return m_new, l, acc


def _causal_attn_kernel(q_ref, k_ref, v_ref, o_ref, *, tq, tk):
    i = pl.program_id(2)
    q = q_ref[...]
    dv = v_ref.shape[1]
    n_full = (i * tq) // tk

    def body(j, carry):
        off = pl.multiple_of(j * tk, tk)
        s = _nt_dot(q, k_ref[pl.ds(off, tk), :])
        return _softmax_step(s, v_ref[pl.ds(off, tk), :], *carry)

    init = (jnp.full((tq, 1), MASK_VALUE, F32), jnp.zeros((tq, 1), F32), jnp.zeros((tq, dv), F32))
    m, l, acc = lax.fori_loop(0, n_full, body, init)
    off = pl.multiple_of(n_full * tk, tk)
    s = _nt_dot(q, k_ref[pl.ds(off, tk), :])
    qpos = i * tq + lax.broadcasted_iota(jnp.int32, (tq, tk), 0)
    kpos = n_full * tk + lax.broadcasted_iota(jnp.int32, (tq, tk), 1)
    s = jnp.where(kpos <= qpos, s, MASK_VALUE)
    m, l, acc = _softmax_step(s, v_ref[pl.ds(off, tk), :], m, l, acc)
    o_ref[...] = (acc / l).astype(o_ref.dtype)


def causal_attention(q, k, v, batch, heads, dk, dv, tq=256, tk=512):
    t = q.shape[0]
    s = t // batch
    nq = s // tq
    assert tk % tq == 0 and s % tk == 0
    est = 4 * (_nbytes((s, dk), BF16) + _nbytes((s, dv), BF16)) + 8 * _nbytes((tq, tk), F32)
    return pl.pallas_call(
        functools.partial(_causal_attn_kernel, tq=tq, tk=tk),
        grid=(batch, heads, nq),
        in_specs=[pl.BlockSpec((tq, dk), lambda b, h, i: (b * nq + i, h)),
                  pl.BlockSpec((s, dk), lambda b, h, i: (b, h)),
                  pl.BlockSpec((s, dv), lambda b, h, i: (b, h))],
        out_specs=pl.BlockSpec((tq, dv), lambda b, h, i: (b * nq + i, h)),
        out_shape=jax.ShapeDtypeStruct((t, heads * dv), BF16),
        compiler_params=_params(("parallel", "parallel", "arbitrary"), est),
    )(q, k, v)


def _moba_kernel(q_ref, k_ref, v_ref, cq_ref, sq_ref, ck_ref, sk_ref, o_ref, kr_scr, km_scr, *, nb, scale):
    i = pl.program_id(2)
    blk = MOBA_BLOCK

    @pl.when(i == 0)
    def _():
        for j in range(nb):
            rows = slice(j * blk, (j + 1) * blk)
            kj = _half_swap(k_ref[rows, :].astype(F32), ck_ref[rows, :], sk_ref[rows, :])
            km_scr[j:j + 1, :] = jnp.mean(kj, axis=0, keepdims=True)
            kr_scr[rows, :] = kj.astype(BF16)

    q = _half_swap(q_ref[...].astype(F32), cq_ref[...], sq_ref[...])
    gate = lax.dot_general(q, km_scr[...], (((1,), (1,)), ((), ())),
                           precision=lax.Precision.HIGHEST, preferred_element_type=F32)
    col = lax.broadcasted_iota(jnp.int32, (blk, nb), 1)
    past = col < i
    gate = jnp.where(past, gate, MASK_VALUE)
    rank = jnp.zeros((blk, nb), jnp.int32)
    for jj in range(nb):
        gj = gate[:, jj:jj + 1]
        ahead = jnp.logical_or(gj > gate, jnp.logical_and(gj == gate, jj < col))
        rank = rank + jnp.where(jnp.logical_and(ahead, jj < i), 1, 0)
    sel = jnp.where(jnp.logical_and(past, rank < MOBA_TOPK), 1.0, 0.0)

    qb = (q * scale).astype(BF16)
    own = pl.multiple_of(i * blk, blk)
    s = _nt_dot(qb, kr_scr[pl.ds(own, blk), :])
    r_id = lax.broadcasted_iota(jnp.int32, (blk, blk), 0)
    c_id = lax.broadcasted_iota(jnp.int32, (blk, blk), 1)
    s = jnp.where(c_id <= r_id, s, MASK_VALUE)
    init = (jnp.full((blk, 1), MASK_VALUE, F32), jnp.zeros((blk, 1), F32), jnp.zeros((blk, HEAD_DIM), F32))
    carry = _softmax_step(s, v_ref[pl.ds(own, blk), :], *init)

    def body(j, carry):
        off = pl.multiple_of(j * blk, blk)
        chosen = jnp.sum(jnp.where(col == j, sel, 0.0), axis=1, keepdims=True) > 0.5
        s = jnp.where(chosen, _nt_dot(qb, kr_scr[pl.ds(off, blk), :]), MASK_VALUE)
        return _softmax_step(s, v_ref[pl.ds(off, blk), :], *carry)

    m, l, acc = lax.fori_loop(0, i, body, carry)
    o_ref[...] = (acc / l).astype(o_ref.dtype)


def moba_attention(zbc, cos_t, sin_t, batch, col0):
    t = zbc.shape[0]
    s = t // batch
    blk = MOBA_BLOCK
    nb = s // blk
    hh = MOBA_HEADS
    est = 6 * _nbytes((s, HEAD_DIM), BF16) + 4 * _nbytes((s, HEAD_DIM), F32) + 16 * _nbytes((blk, blk), F32)
    return pl.pallas_call(
        functools.partial(_moba_kernel, nb=nb, scale=float(HEAD_DIM ** -0.5)),
        grid=(batch, hh, nb),
        in_specs=[pl.BlockSpec((blk, HEAD_DIM), lambda b, h, i: (b * nb + i, col0 + h)),
                  pl.BlockSpec((s, HEAD_DIM), lambda b, h, i: (b, col0 + hh + h)),
                  pl.BlockSpec((s, HEAD_DIM), lambda b, h, i: (b, col0 + 2 * hh + h)),
                  pl.BlockSpec((blk, HEAD_DIM), lambda b, h, i: (b * nb + i, 0)),
                  pl.BlockSpec((blk, HEAD_DIM), lambda b, h, i: (b * nb + i, 0)),
                  pl.BlockSpec((s, HEAD_DIM), lambda b, h, i: (b, 0)),
                  pl.BlockSpec((s, HEAD_DIM), lambda b, h, i: (b, 0))],
        out_specs=pl.BlockSpec((blk, HEAD_DIM), lambda b, h, i: (b * nb + i, h)),
        out_shape=jax.ShapeDtypeStruct((t, hh * HEAD_DIM), BF16),
        scratch_shapes=[pltpu.VMEM((s, HEAD_DIM), BF16), pltpu.VMEM((nb, HEAD_DIM), F32)],
        compiler_params=_params(("parallel", "parallel", "arbitrary"), est),
    )(zbc, zbc, zbc, cos_t, sin_t, cos_t, sin_t)


def _sb_tile(z, v, u, r, acc, strict):
    lsm = -(jnp.maximum(z, 0.0) + jnp.log(1.0 + jnp.exp(-jnp.abs(z))))
    if strict is not None:
        lsm = jnp.where(strict, lsm, 0.0)
    hi = lsm.astype(BF16)
    lo = (lsm - hi.astype(F32)).astype(BF16)
    incl = jnp.dot(hi, u, preferred_element_type=F32) + jnp.dot(lo, u, preferred_element_type=F32)
    a = jnp.exp(z + incl + r)
    if strict is not None:
        a = jnp.where(strict, a, 0.0)
    acc = acc + jnp.dot(a.astype(BF16), v, preferred_element_type=F32)
    return r + incl[:, 0:1], acc


def _sb_kernel(q_ref, k_ref, v_ref, u_ref, o_ref, *, tq, scale):
    i = pl.program_id(2)
    qb = (q_ref[...].astype(F32) * scale).astype(BF16)
    u = u_ref[...]
    own = pl.multiple_of(i * tq, tq)
    r_id = lax.broadcasted_iota(jnp.int32, (tq, tq), 0)
    c_id = lax.broadcasted_iota(jnp.int32, (tq, tq), 1)
    z = _nt_dot(qb, k_ref[pl.ds(own, tq), :])
    carry = _sb_tile(z, v_ref[pl.ds(own, tq), :], u, jnp.zeros((tq, 1), F32),
                     jnp.zeros((tq, HEAD_DIM), F32), c_id < r_id)

    def body(n, carry):
        off = pl.multiple_of((i - 1 - n) * tq, tq)
        z = _nt_dot(qb, k_ref[pl.ds(off, tq), :])
        return _sb_tile(z, v_ref[pl.ds(off, tq), :], u, *carry, None)

    _, acc = lax.fori_loop(0, i, body, carry)
    o_ref[...] = acc.astype(o_ref.dtype)


def stick_breaking_attention(zbc, batch, col0, tq=256):
    t = zbc.shape[0]
    s = t // batch
    nq = s // tq
    hh = SB_HEADS
    u = (jnp.arange(tq)[:, None] >= jnp.arange(tq)[None, :]).astype(BF16)
    est = 8 * _nbytes((s, HEAD_DIM), BF16) + 16 * _nbytes((tq, tq), F32)
    return pl.pallas_call(
        functools.partial(_sb_kernel, tq=tq, scale=float(HEAD_DIM ** -0.5)),
        grid=(batch, hh, nq),
        in_specs=[pl.BlockSpec((tq, HEAD_DIM), lambda b, h, i: (b * nq + i, col0 + h)),
                  pl.BlockSpec((s, HEAD_DIM), lambda b, h, i: (b, col0 + hh + h)),
                  pl.BlockSpec((s, HEAD_DIM), lambda b, h, i: (b, col0 + 2 * hh + h)),
                  pl.BlockSpec((tq, tq), lambda b, h, i: (0, 0))],
        out_specs=pl.BlockSpec((tq, HEAD_DIM), lambda b, h, i: (b * nq + i, h)),
        out_shape=jax.ShapeDtypeStruct((t, hh * HEAD_DIM), BF16),
        compiler_params=_params(("parallel", "parallel", "arbitrary"), est),
    )(zbc, zbc, zbc, u)


def _merge_kernel(oa_ref, ob_ref, oc_ref, ga_ref, gb_ref, gc_ref, wa_ref, wb_ref, wc_ref, o_ref):
    def branch(o, g, w):
        return jax.nn.sigmoid(g[...].astype(F32)) * jnp.dot(o[...], w[...], preferred_element_type=F32)
    o_ref[...] = (branch(oa_ref, ga_ref, wa_ref) + branch(ob_ref, gb_ref, wb_ref)
                  + branch(oc_ref, gc_ref, wc_ref)).astype(o_ref.dtype)


def gated_merge(o_a, o_b, o_c, zg, wa, wb, wc, tm=512):
    m = o_a.shape[0]
    d = wa.shape[1]
    row = lambda a: pl.BlockSpec((tm, a.shape[1]), lambda i: (i, 0))
    full = lambda a: pl.BlockSpec(a.shape, lambda i: (0, 0))
    gate = lambda n: pl.BlockSpec((tm, d), lambda i: (i, n))
    est = 4 * _nbytes((d, d), BF16) + 12 * _nbytes((tm, d), F32)
    return pl.pallas_call(
        _merge_kernel,
        grid=(m // tm,),
        in_specs=[row(o_a), row(o_b), row(o_c), gate(0), gate(1), gate(2), full(wa), full(wb), full(wc)],
        out_specs=pl.BlockSpec((tm, d), lambda i: (i, 0)),
        out_shape=jax.ShapeDtypeStruct((m, d), BF16),
        compiler_params=_params(("parallel",), est),
    )(o_a, o_b, o_c, zg, zg, zg, wa, wb, wc)


def _xattn_kernel(q_ref, k_ref, v_ref, o_ref, *, heads, scale):
    hd = q_ref.shape[1] // heads
    for h in range(heads):
        cols = slice(h * hd, (h + 1) * hd)
        s = _nt_dot(q_ref[:, cols], k_ref[:, cols]) * scale
        p = jnp.exp(s - jnp.max(s, axis=1, keepdims=True))
        o = jnp.dot(p.astype(BF16), v_ref[:, cols], preferred_element_type=F32)
        o_ref[:, cols] = (o / jnp.sum(p, axis=1, keepdims=True)).astype(o_ref.dtype)


def cross_attention(q, k, v, batch, heads, tq=512):
    t, d = q.shape
    s = t // batch
    mlen = k.shape[0] // batch
    nq = s // tq
    est = 8 * _nbytes((tq, d), BF16) + 8 * _nbytes((mlen, d), BF16) + 8 * _nbytes((tq, mlen), F32)
    return pl.pallas_call(
        functools.partial(_xattn_kernel, heads=heads, scale=float((d // heads) ** -0.5)),
        grid=(batch, nq),
        in_specs=[pl.BlockSpec((tq, d), lambda b, i: (b * nq + i, 0)),
                  pl.BlockSpec((mlen, d), lambda b, i: (b, 0)),
                  pl.BlockSpec((mlen, d), lambda b, i: (b, 0))],
        out_specs=pl.BlockSpec((tq, d), lambda b, i: (b * nq + i, 0)),
        out_shape=jax.ShapeDtypeStruct((t, d), BF16),
        compiler_params=_params(("parallel", "parallel"), est),
    )(q, k, v)


def _swiglu_partial(x, w1, w3, w2):
    a = jnp.dot(x, w1, preferred_element_type=F32)
    b = jnp.dot(x, w3, preferred_element_type=F32)
    act = (a * jax.nn.sigmoid(a) * b).astype(BF16)
    return jnp.dot(act, w2, preferred_element_type=F32)


def _ffn_kernel(x_ref, w1_ref, w3_ref, w2_ref, r_ref, g_ref, h_ref, hn_ref, acc_ref):
    j = pl.program_id(1)

    @pl.when(j == 0)
    def _():
        acc_ref[...] = r_ref[...]

    acc_ref[...] += _swiglu_partial(x_ref[...], w1_ref[...], w3_ref[...], w2_ref[...])

    @pl.when(j == pl.num_programs(1) - 1)
    def _():
        h = acc_ref[...]
        h_ref[...] = h
        hn_ref[...] = _rms(h, g_ref[...]).astype(hn_ref.dtype)


def dense_ffn(x, w1, w3, w2, res, g_next, tm=512, tf=512):
    m, d = x.shape
    f = w1.shape[1]
    assert f % tf == 0
    est = 2 * (_nbytes((tm, d), BF16) + 3 * _nbytes((d, tf), BF16) + 2 * _nbytes((tm, d), F32)
               + _nbytes((tm, d), BF16)) + _nbytes((tm, d), F32) + 4 * _nbytes((tm, tf), F32)
    return pl.pallas_call(
        _ffn_kernel,
        grid=(m // tm, f // tf),
        in_specs=[pl.BlockSpec((tm, d), lambda i, j: (i, 0)),
                  pl.BlockSpec((d, tf), lambda i, j: (0, j)),
                  pl.BlockSpec((d, tf), lambda i, j: (0, j)),
                  pl.BlockSpec((tf, d), lambda i, j: (j, 0)),
                  pl.BlockSpec((tm, d), lambda i, j: (i, 0)),
                  pl.BlockSpec((1, d), lambda i, j: (0, 0))],
        out_specs=[pl.BlockSpec((tm, d), lambda i, j: (i, 0)),
                   pl.BlockSpec((tm, d), lambda i, j: (i, 0))],
        out_shape=[jax.ShapeDtypeStruct((m, d), F32), jax.ShapeDtypeStruct((m, d), BF16)],
        scratch_shapes=[pltpu.VMEM((tm, d), F32)],
        compiler_params=_params(("parallel", "arbitrary"), est),
    )(x, w1, w3, w2, res, g_next.reshape(1, d).astype(F32))


def _router_kernel(h_ref, g_ref, wr_ref, idx_ref, gate_ref):
    hn = _rms(h_ref[...], g_ref[...])
    logits = lax.dot_general(wr_ref[...], hn, (((1,), (1,)), ((), ())),
                             precision=lax.Precision.HIGHEST, preferred_element_type=F32)
    e_id = lax.broadcasted_iota(jnp.int32, logits.shape, 0)
    n_e = logits.shape[0]
    v1 = jnp.max(logits, axis=0, keepdims=True)
    i1 = jnp.min(jnp.where(logits == v1, e_id, n_e), axis=0, keepdims=True)
    rest = jnp.where(e_id == i1, -jnp.inf, logits)
    v2 = jnp.max(rest, axis=0, keepdims=True)
    i2 = jnp.min(jnp.where(rest == v2, e_id, n_e), axis=0, keepdims=True)
    e2 = jnp.exp(v2 - v1)
    idx_ref[0:1, :] = i1
    idx_ref[1:2, :] = i2
    gate_ref[0:1, :] = 1.0 / (1.0 + e2)
    gate_ref[1:2, :] = e2 / (1.0 + e2)


def moe_router(h, g, w_router, tm=512):
    m, d = h.shape
    n_e = w_router.shape[1]
    wr_t = w_router.T.astype(F32)
    est = 6 * _nbytes((tm, d), F32)
    return pl.pallas_call(
        _router_kernel,
        grid=(m // tm,),
        in_specs=[pl.BlockSpec((tm, d), lambda i: (i, 0)),
                  pl.BlockSpec((1, d), lambda i: (0, 0)),
                  pl.BlockSpec((n_e, d), lambda i: (0, 0))],
        out_specs=[pl.BlockSpec((MOE_TOPK, tm), lambda i: (0, i)),
                   pl.BlockSpec((MOE_TOPK, tm), lambda i: (0, i))],
        out_shape=[jax.ShapeDtypeStruct((MOE_TOPK, m), jnp.int32), jax.ShapeDtypeStruct((MOE_TOPK, m), F32)],
        compiler_params=_params(("parallel",), est),
    )(h, g.reshape(1, d).astype(F32), wr_t)


def _row_copy(src_hbm, src_row, dst_ref, dst_row, sem):
    return pltpu.make_async_copy(src_hbm.at[pl.ds(src_row, 1), :], dst_ref.at[pl.ds(dst_row, 1), :], sem)


def _gather_kernel(rows_ref, src_hbm, o_ref, sem):
    base = pl.program_id(0) * GATHER_ROWS

    def start(r, _):
        _row_copy(src_hbm, rows_ref[base + r], o_ref, r, sem).start()
        return 0

    def wait(r, _):
        _row_copy(src_hbm, 0, o_ref, r, sem).wait()
        return 0

    lax.fori_loop(0, GATHER_ROWS, start, 0)
    lax.fori_loop(0, GATHER_ROWS, wait, 0)


def gather_rows(src, rows):
    n = rows.shape[0]
    d = src.shape[1]
    return pl.pallas_call(
        _gather_kernel,
        grid_spec=pltpu.PrefetchScalarGridSpec(
            num_scalar_prefetch=1,
            grid=(n // GATHER_ROWS,),
            in_specs=[pl.BlockSpec(memory_space=pl.ANY)],
            out_specs=pl.BlockSpec((GATHER_ROWS, d), lambda i, rows: (i, 0)),
            scratch_shapes=[pltpu.SemaphoreType.DMA(())],
        ),
        out_shape=jax.ShapeDtypeStruct((n, d), src.dtype),
        compiler_params=_params(("arbitrary",), 4 * _nbytes((GATHER_ROWS, d), src.dtype)),
    )(rows, src)


def _expert_ffn_kernel(te_ref, tv_ref, x_ref, g_ref, w1_ref, w3_ref, w2_ref, o_ref, xn_ref):
    t = pl.program_id(0)
    j = pl.program_id(1)

    @pl.when(j == 0)
    def _():
        o_ref[...] = jnp.zeros_like(o_ref)
        xn_ref[...] = _rms(x_ref[...], g_ref[...]).astype(BF16)

    @pl.when(tv_ref[t] > 0)
    def _():
        o_ref[...] += _swiglu_partial(xn_ref[...], w1_ref[...], w3_ref[...], w2_ref[...])


def expert_ffn(xs, g, w1, w3, w2, tile_e, tile_valid, tf=512):
    n, d = xs.shape
    f = w1.shape[2]
    nf = f // tf
    tm = MOE_TILE
    last = nf - 1

    def jj(t, j, tv):
        return j * tv[t] + last * (1 - tv[t])

    est = 2 * (2 * _nbytes((tm, d), F32) + 3 * _nbytes((d, tf), BF16)) + _nbytes((tm, d), BF16) \
        + 4 * _nbytes((tm, tf), F32) + _nbytes((tm, d), F32)
    return pl.pallas_call(
        _expert_ffn_kernel,
        grid_spec=pltpu.PrefetchScalarGridSpec(
            num_scalar_prefetch=2,
            grid=(n // tm, nf),
            in_specs=[pl.BlockSpec((tm, d), lambda t, j, te, tv: (t, 0)),
                      pl.BlockSpec((1, d), lambda t, j, te, tv: (0, 0)),
                      pl.BlockSpec((None, d, tf), lambda t, j, te, tv: (te[t], 0, jj(t, j, tv))),
                      pl.BlockSpec((None, d, tf), lambda t, j, te, tv: (te[t], 0, jj(t, j, tv))),
                      pl.BlockSpec((None, tf, d), lambda t, j, te, tv: (te[t], jj(t, j, tv), 0))],
            out_specs=pl.BlockSpec((tm, d), lambda t, j, te, tv: (t, 0)),
            scratch_shapes=[pltpu.VMEM((tm, d), BF16)],
        ),
        out_shape=jax.ShapeDtypeStruct((n, d), F32),
        compiler_params=_params(("arbitrary", "arbitrary"), est),
    )(tile_e, tile_valid, xs, g.reshape(1, d).astype(F32), w1, w3, w2)


def _combine_kernel(slots_ref, ys_hbm, h_ref, gate_ref, g_ref, o_ref, buf0, buf1, sem):
    base = pl.program_id(0) * GATHER_ROWS

    def start(r, _):
        flat = (base + r) * MOE_TOPK
        _row_copy(ys_hbm, slots_ref[flat], buf0, r, sem).start()
        _row_copy(ys_hbm, slots_ref[flat + 1], buf1, r, sem).start()
        return 0

    def wait(r, _):
        _row_copy(ys_hbm, 0, buf0, r, sem).wait()
        _row_copy(ys_hbm, 0, buf1, r, sem).wait()
        return 0

    lax.fori_loop(0, GATHER_ROWS, start, 0)
    lax.fori_loop(0, GATHER_ROWS, wait, 0)
    gate = gate_ref[...]
    h = h_ref[...] + gate[:, 0:1] * buf0[...] + gate[:, 1:2] * buf1[...]
    o_ref[...] = _rms(h, g_ref[...])


def moe_combine_norm(ys, slot_of_flat, h, gates, g_out):
    m, d = h.shape
    tm = GATHER_ROWS
    return pl.pallas_call(
        _combine_kernel,
        grid_spec=pltpu.PrefetchScalarGridSpec(
            num_scalar_prefetch=1,
            grid=(m // tm,),
            in_specs=[pl.BlockSpec(memory_space=pl.ANY),
                      pl.BlockSpec((tm, d), lambda i, s: (i, 0)),
                      pl.BlockSpec((tm, MOE_TOPK), lambda i, s: (i, 0)),
                      pl.BlockSpec((1, d), lambda i, s: (0, 0))],
            out_specs=pl.BlockSpec((tm, d), lambda i, s: (i, 0)),
            scratch_shapes=[pltpu.VMEM((tm, d), F32), pltpu.VMEM((tm, d), F32), pltpu.SemaphoreType.DMA(())],
        ),
        out_shape=jax.ShapeDtypeStruct((m, d), F32),
        compiler_params=_params(("arbitrary",), 8 * _nbytes((tm, d), F32)),
    )(slot_of_flat, ys, h, gates, g_out.reshape(1, d).astype(F32))


def moe_routing_metadata(top_idx, n_tokens):
    tk = n_tokens * MOE_TOPK
    flat_e = top_idx.T.reshape(-1)
    onehot = (flat_e[:, None] == jnp.arange(N_EXPERTS)[None, :]).astype(jnp.int32)
    within = jnp.take_along_axis(jnp.cumsum(onehot, axis=0), flat_e[:, None], axis=1)[:, 0] - 1
    counts = jnp.sum(onehot, axis=0)
    padded = (counts + MOE_TILE - 1) // MOE_TILE * MOE_TILE
    pad_end = jnp.cumsum(padded)
    slot_of_flat = ((pad_end - padded)[flat_e] + within).astype(jnp.int32)
    n_tiles = tk // MOE_TILE + N_EXPERTS
    slot_tok = jnp.zeros((n_tiles * MOE_TILE,), jnp.int32).at[slot_of_flat].set(
        jnp.arange(tk, dtype=jnp.int32) // MOE_TOPK)
    tile_start = jnp.arange(n_tiles, dtype=jnp.int32) * MOE_TILE
    tile_valid = (tile_start < pad_end[-1]).astype(jnp.int32)
    tile_e = jnp.minimum(jnp.searchsorted(pad_end, tile_start, side='right'), N_EXPERTS - 1)
    last_e = jnp.max(jnp.where(tile_valid > 0, tile_e, 0))
    tile_e = jnp.where(tile_valid > 0, tile_e, last_e).astype(jnp.int32)
    return slot_tok, slot_of_flat, tile_e, tile_valid


def _rope_partner(w):
    half = w.shape[-1] // 2
    return jnp.concatenate([-w[..., half:], w[..., :half]], axis=-1)


def _rope_tables(positions):
    pos = positions.astype(F32).reshape(-1, 1)

    def cs(half):
        inv_freq = ROPE_THETA ** (-jnp.arange(half, dtype=F32) / half)
        ang = pos * inv_freq
        return jnp.cos(ang), jnp.sin(ang)

    c32, s32 = cs(MLA_ROPE // 2)
    zeros = jnp.zeros((pos.shape[0], LANES // 2), F32)
    mla_c = jnp.concatenate([c32, c32, zeros], axis=1)
    mla_s = jnp.concatenate([s32, s32, zeros], axis=1)
    c64, s64 = cs(HEAD_DIM // 2)
    moba_c = jnp.concatenate([c64, c64], axis=1)
    moba_s = jnp.concatenate([-s64, s64], axis=1)
    return mla_c, mla_s, moba_c, moba_s


def _layer_weights(w_in, w_uq, w_ukv):
    d = w_in.shape[0]
    o = 0
    w_cq = w_in[:, o:o + MLA_LORA]; o += MLA_LORA
    w_ckv = w_in[:, o:o + MLA_LORA]; o += MLA_LORA
    w_kr = w_in[:, o:o + MLA_ROPE]; o += MLA_ROPE
    nbc = 3 * (MOBA_HEADS + SB_HEADS) * HEAD_DIM
    w_bc = w_in[:, o:o + nbc]; o += nbc
    w_g = w_in[:, o:]
    w1 = jnp.concatenate([w_cq, w_ckv, w_kr, _rope_partner(w_kr)], axis=1).astype(BF16)
    uq = w_uq.reshape(MLA_LORA, MLA_HEADS, MLA_NOPE + MLA_ROPE)
    uq_rope = uq[..., MLA_NOPE:]
    wq = jnp.concatenate([uq[..., :MLA_NOPE], uq_rope, _rope_partner(uq_rope)], axis=-1)
    wq = wq.reshape(MLA_LORA, MLA_HEADS * 2 * LANES).astype(BF16)
    ukv = w_ukv.reshape(MLA_LORA, MLA_HEADS, MLA_NOPE + MLA_V)
    wkv = jnp.concatenate([ukv[..., :MLA_NOPE].reshape(MLA_LORA, -1), ukv[..., MLA_NOPE:].reshape(MLA_LORA, -1)],
                          axis=1).astype(BF16)
    return w1, w_bc.astype(BF16), w_g.astype(BF16), wq, wkv


def kernel(x, mem, positions, g_mix, w_in, g_cq, g_ckv, w_uq, w_ukv, w_up_a, w_up_b, w_up_c, w_o, g_x, g_mem, w_xq, w_xk, w_xv, w_xo, g_ffn, w_ff1, w_ff3, w_ff2, w_router, w_e1, w_e3, w_e2, g_final):
    batch, seq, d = x.shape
    t = batch * seq
    depth = g_mix.shape[0]
    assert depth == 2, "the final norm is fused into the expert layer, which must come last"
    mem2 = mem.reshape(-1, d)
    mla_c, mla_s, moba_c, moba_s = _rope_tables(positions)
    n_moba = 3 * MOBA_HEADS

    h = x.reshape(t, d)
    hn = rmsnorm(h, g_mix[0], BF16)
    out = None
    for l in range(depth):
        w1, w_bc, w_g, wq, wkv = _layer_weights(w_in[l], w_uq[l], w_ukv[l])
        z1 = matmul(hn, w1, BF16, tn=w1.shape[1])
        zbc = matmul(hn, w_bc, BF16, tn=1280)
        zg = matmul(hn, w_g, BF16, tn=1024)
        q_a, k_a, v_a = mla_prep(z1, g_cq[l], g_ckv[l], wq, wkv, mla_c, mla_s)
        o_a = causal_attention(q_a, k_a, v_a, batch, MLA_HEADS, 2 * LANES, MLA_V)
        o_b = moba_attention(zbc, moba_c, moba_s, batch, 0)
        o_c = stick_breaking_attention(zbc, batch, n_moba)
        merged = gated_merge(o_a, o_b, o_c, zg, w_up_a[l].astype(BF16), w_up_b[l].astype(BF16),
                             w_up_c[l].astype(BF16))
        h, hn = matmul_res_norm(merged, w_o[l].astype(BF16), h, g_x[l])
        mn = rmsnorm(mem2, g_mem[l], BF16)
        q_x = matmul(hn, w_xq[l].astype(BF16), BF16)
        k_x = matmul(mn, w_xk[l].astype(BF16), BF16)
        v_x = matmul(mn, w_xv[l].astype(BF16), BF16)
        o_x = cross_attention(q_x, k_x, v_x, batch, XATTN_HEADS)
        h, hn = matmul_res_norm(o_x, w_xo[l].astype(BF16), h, g_ffn[l])
        g_next = g_mix[l + 1] if l + 1 < depth else g_final
        if l % 2 == 0:
            e = l // 2
            h, hn = dense_ffn(hn, w_ff1[e].astype(BF16), w_ff3[e].astype(BF16), w_ff2[e].astype(BF16), h, g_next)
            out = hn
        else:
            e = l // 2
            top_idx, top_gate = moe_router(h, g_ffn[l], w_router[e])
            slot_tok, slot_of_flat, tile_e, tile_valid = moe_routing_metadata(top_idx, t)
            xs = gather_rows(h, slot_tok)
            ys = expert_ffn(xs, g_ffn[l], w_e1[e].astype(BF16), w_e3[e].astype(BF16), w_e2[e].astype(BF16),
                            tile_e, tile_valid)
            out = moe_combine_norm(ys, slot_of_flat, h, top_gate.T, g_next)
    return out.reshape(batch, seq, d).astype(x.dtype)
```

```python
import functools

import numpy as np
import jax
import jax.numpy as jnp
from jax import lax
from jax.experimental import pallas as pl
from jax.experimental.pallas import tpu as pltpu

F32 = jnp.float32
BF16 = jnp.bfloat16

HEAD_DIM = 128
MLA_HEADS = 6
MLA_LORA = 512
MLA_NOPE = 128
MLA_ROPE = 64
MLA_V = 128
MOBA_HEADS = 5
MOBA_BLOCK = 256
MOBA_TOPK = 3
SB_HEADS = 5
N_BRANCH = 3
ROPE_THETA = 10000.0
XATTN_HEADS = 4
N_EXPERTS = 8
MOE_TOPK = 2
RMS_EPS = 1e-6

LANES = 128
V7X_VMEM_BYTES = 64 * 1024 * 1024
VMEM_CEILING = V7X_VMEM_BYTES - 8 * 1024 * 1024

MASK_VALUE = -1e30

MOE_TILE = 512
GATHER_ROWS = 256


def _params(sem, est_bytes):
    limit = int(min(VMEM_CEILING, max(32 * 1024 * 1024, est_bytes * 5 // 4)))
    return pltpu.CompilerParams(dimension_semantics=sem, vmem_limit_bytes=limit)


def _nbytes(shape, dtype):
    return int(np.prod(shape)) * jnp.dtype(dtype).itemsize


def _rms(x, g):
    return x * lax.rsqrt(jnp.mean(x * x, axis=-1, keepdims=True) + RMS_EPS) * g


def _rmsnorm_kernel(x_ref, g_ref, o_ref):
    o_ref[...] = _rms(x_ref[...].astype(F32), g_ref[...]).astype(o_ref.dtype)


def rmsnorm(x, g, out_dtype, tm=512):
    m, d = x.shape
    tm = min(tm, m)
    return pl.pallas_call(
        _rmsnorm_kernel,
        grid=(m // tm,),
        in_specs=[pl.BlockSpec((tm, d), lambda i: (i, 0)),
                  pl.BlockSpec((1, d), lambda i: (0, 0))],
        out_specs=pl.BlockSpec((tm, d), lambda i: (i, 0)),
        out_shape=jax.ShapeDtypeStruct((m, d), out_dtype),
        compiler_params=_params(("parallel",), 4 * _nbytes((tm, d), F32)),
    )(x, g.reshape(1, d).astype(F32))


def _mm_kernel(a_ref, w_ref, o_ref):
    o_ref[...] = jnp.dot(a_ref[...], w_ref[...], preferred_element_type=F32).astype(o_ref.dtype)


def matmul(a, w, out_dtype, tm=1024, tn=1024):
    m, k = a.shape
    n = w.shape[1]
    tm, tn = min(tm, m), min(tn, n)
    assert m % tm == 0 and n % tn == 0, (m, n, tm, tn)
    est = 2 * (_nbytes((tm, k), a.dtype) + _nbytes((k, tn), w.dtype) + _nbytes((tm, tn), out_dtype)) \
        + _nbytes((tm, tn), F32)
    return pl.pallas_call(
        _mm_kernel,
        grid=(m // tm, n // tn),
        in_specs=[pl.BlockSpec((tm, k), lambda i, j: (i, 0)),
                  pl.BlockSpec((k, tn), lambda i, j: (0, j))],
        out_specs=pl.BlockSpec((tm, tn), lambda i, j: (i, j)),
        out_shape=jax.ShapeDtypeStruct((m, n), out_dtype),
        compiler_params=_params(("parallel", "parallel"), est),
    )(a, w)


def _mm_res_norm_kernel(a_ref, w_ref, r_ref, g_ref, h_ref, hn_ref):
    h = r_ref[...] + jnp.dot(a_ref[...], w_ref[...], preferred_element_type=F32)
    h_ref[...] = h
    hn_ref[...] = _rms(h, g_ref[...]).astype(hn_ref.dtype)


def matmul_res_norm(a, w, res, g, tm=512):
    m, k = a.shape
    n = w.shape[1]
    est = 2 * (_nbytes((tm, k), a.dtype) + _nbytes((k, n), w.dtype) + 2 * _nbytes((tm, n), F32)
               + _nbytes((tm, n), BF16)) + 2 * _nbytes((tm, n), F32)
    return pl.pallas_call(
        _mm_res_norm_kernel,
        grid=(m // tm,),
        in_specs=[pl.BlockSpec((tm, k), lambda i: (i, 0)),
                  pl.BlockSpec((k, n), lambda i: (0, 0)),
                  pl.BlockSpec((tm, n), lambda i: (i, 0)),
                  pl.BlockSpec((1, n), lambda i: (0, 0))],
        out_specs=[pl.BlockSpec((tm, n), lambda i: (i, 0)),
                   pl.BlockSpec((tm, n), lambda i: (i, 0))],
        out_shape=[jax.ShapeDtypeStruct((m, n), F32), jax.ShapeDtypeStruct((m, n), BF16)],
        compiler_params=_params(("parallel",), est),
    )(a, w, res, g.reshape(1, n).astype(F32))


def _half_swap(y, c, s):
    return y * c + pltpu.roll(y, LANES // 2, 1) * s


def _mla_prep_kernel(z_ref, gq_ref, gkv_ref, wq_ref, wkv_ref, c_ref, s_ref, q_ref, k_ref, v_ref, *, scale):
    z = z_ref[...].astype(F32)
    c = c_ref[...]
    s = s_ref[...]
    nq = _rms(z[:, :MLA_LORA], gq_ref[...]).astype(BF16)
    nkv = _rms(z[:, MLA_LORA:2 * MLA_LORA], gkv_ref[...]).astype(BF16)
    q = jnp.dot(nq, wq_ref[...], preferred_element_type=F32)
    kv = jnp.dot(nkv, wkv_ref[...], preferred_element_type=F32)
    k_pe = _half_swap(z[:, 2 * MLA_LORA:], c, s).astype(BF16)
    for h in range(MLA_HEADS):
        lo = 2 * LANES * h
        q_ref[:, lo:lo + LANES] = (q[:, lo:lo + LANES] * scale).astype(BF16)
        q_ref[:, lo + LANES:lo + 2 * LANES] = (_half_swap(q[:, lo + LANES:lo + 2 * LANES], c, s) * scale).astype(BF16)
        k_ref[:, lo:lo + LANES] = kv[:, LANES * h:LANES * (h + 1)].astype(BF16)
        k_ref[:, lo + LANES:lo + 2 * LANES] = k_pe
    v_ref[...] = kv[:, MLA_HEADS * MLA_NOPE:].astype(BF16)


def mla_prep(z1, g_cq, g_ckv, wq, wkv, cos_t, sin_t, tm=512):
    m, zc = z1.shape
    nq, nkv = wq.shape[1], wkv.shape[1]
    scale = float((MLA_NOPE + MLA_ROPE) ** -0.5)
    row = lambda c: pl.BlockSpec((tm, c), lambda i: (i, 0))
    full = lambda a: pl.BlockSpec(a.shape, lambda i: (0, 0))
    g_cq = g_cq.reshape(1, -1).astype(F32)
    g_ckv = g_ckv.reshape(1, -1).astype(F32)
    est = 4 * _nbytes((tm, nq), F32) + 4 * (_nbytes(wq.shape, BF16) + _nbytes(wkv.shape, BF16))
    return pl.pallas_call(
        functools.partial(_mla_prep_kernel, scale=scale),
        grid=(m // tm,),
        in_specs=[row(zc), full(g_cq), full(g_ckv), full(wq), full(wkv), row(LANES), row(LANES)],
        out_specs=[row(nq), row(nq), row(MLA_HEADS * MLA_V)],
        out_shape=[jax.ShapeDtypeStruct((m, nq), BF16), jax.ShapeDtypeStruct((m, nq), BF16),
                   jax.ShapeDtypeStruct((m, MLA_HEADS * MLA_V), BF16)],
        compiler_params=_params(("parallel",), est),
    )(z1, g_cq, g_ckv, wq, wkv, cos_t, sin_t)


def _nt_dot(a, b):
    return lax.dot_general(a, b, (((1,), (1,)), ((), ())), preferred_element_type=F32)


def _softmax_tiles(scores, load_v, carry):
    p, stats = [], []
    for s, (m, l, _) in zip(scores, carry):
        m_new = jnp.maximum(m, jnp.max(s, axis=1, keepdims=True))
        alpha = jnp.exp(m - m_new)
        ph = jnp.exp(s - m_new)
        stats.append((m_new, alpha, alpha * l + jnp.sum(ph, axis=1, keepdims=True)))
        p.append(ph.astype(BF16))
    return [(m_new, l_new, alpha * acc + jnp.dot(p[h], load_v(h), preferred_element_type=F32))
            for h, ((m_new, alpha, l_new), (_, _, acc)) in enumerate(zip(stats, carry))]


def _softmax_init(rows, dv):
    return (jnp.full((rows, 1), MASK_VALUE, F32), jnp.zeros((rows, 1), F32), jnp.zeros((rows, dv), F32))


def _causal_attn_kernel(q_ref, k_ref, v_ref, o_ref, *, tq, tk, dk, dv, hpg):
    i = pl.program_id(2)
    n_full = (i * tq) // tk
    heads = range(hpg)
    q = [q_ref[:, h * dk:(h + 1) * dk] for h in heads]

    def tile(j, carry, mask):
        off = pl.multiple_of(j * tk, tk)
        s = [_nt_dot(q[h], k_ref[pl.ds(off, tk), h * dk:(h + 1) * dk]) for h in heads]
        if mask is not None:
            s = [jnp.where(mask, sh, MASK_VALUE) for sh in s]
        return _softmax_tiles(s, lambda h: v_ref[pl.ds(off, tk), h * dv:(h + 1) * dv], carry)

    carry = lax.fori_loop(0, n_full, lambda j, c: tile(j, c, None), [_softmax_init(tq, dv) for _ in heads])
    qpos = i * tq + lax.broadcasted_iota(jnp.int32, (tq, tk), 0)
    kpos = n_full * tk + lax.broadcasted_iota(jnp.int32, (tq, tk), 1)
    carry = tile(n_full, carry, kpos <= qpos)
    for h in heads:
        _, l, acc = carry[h]
        o_ref[:, h * dv:(h + 1) * dv] = (acc / l).astype(o_ref.dtype)


def causal_attention(q, k, v, batch, heads, dk, dv, tq=256, tk=512, hpg=3):
    t = q.shape[0]
    s = t // batch
    nq = s // tq
    assert tk % tq == 0 and s % tk == 0 and heads % hpg == 0
    est = 4 * hpg * (_nbytes((s, dk), BF16) + _nbytes((s, dv), BF16)) + 8 * hpg * _nbytes((tq, tk), F32)
    return pl.pallas_call(
        functools.partial(_causal_attn_kernel, tq=tq, tk=tk, dk=dk, dv=dv, hpg=hpg),
        grid=(batch, heads // hpg, nq),
        in_specs=[pl.BlockSpec((tq, hpg * dk), lambda b, g, i: (b * nq + i, g)),
                  pl.BlockSpec((s, hpg * dk), lambda b, g, i: (b, g)),
                  pl.BlockSpec((s, hpg * dv), lambda b, g, i: (b, g))],
        out_specs=pl.BlockSpec((tq, hpg * dv), lambda b, g, i: (b * nq + i, g)),
        out_shape=jax.ShapeDtypeStruct((t, heads * dv), BF16),
        compiler_params=_params(("parallel", "parallel", "arbitrary"), est),
    )(q, k, v)


def _moba_select(gate_t, i, nb):
    nq = gate_t.shape[1]
    row = lax.broadcasted_iota(jnp.int32, (nb, nq), 0)
    rank = jnp.zeros((nb, nq), jnp.int32)
    for jj in range(nb):
        gj = gate_t[jj:jj + 1, :]
        ahead = jnp.logical_or(gj > gate_t, jnp.logical_and(gj == gate_t, jj < row))
        rank = rank + jnp.where(jnp.logical_and(ahead, jj < i), 1, 0)
    sel_t = jnp.where(jnp.logical_and(row < i, rank < MOBA_TOPK), 1.0, 0.0)
    sel_t = jnp.concatenate([sel_t, jnp.zeros((LANES - nb, nq), F32)], axis=0)
    return sel_t.T


def _moba_kernel(q_ref, k_ref, v_ref, cq_ref, sq_ref, ck_ref, sk_ref, o_ref, kr_scr, km_scr, *, nb, nh, scale):
    i = pl.program_id(1)
    blk = MOBA_BLOCK
    d = HEAD_DIM
    heads = range(nh)

    @pl.when(i == 0)
    def _():
        def prep(j, _):
            rows = pl.ds(pl.multiple_of(j * blk, blk), blk)
            c, s = ck_ref[rows, :], sk_ref[rows, :]
            for h in heads:
                kj = _half_swap(k_ref[rows, h * d:(h + 1) * d].astype(F32), c, s)
                km_scr[pl.ds(h * nb + j, 1), :] = jnp.mean(kj, axis=0, keepdims=True)
                kr_scr[rows, h * d:(h + 1) * d] = kj.astype(BF16)
            return 0
        lax.fori_loop(0, nb, prep, 0)

    cq, sq = cq_ref[...], sq_ref[...]
    lane = lax.broadcasted_iota(jnp.int32, (blk, LANES), 1)
    qb, sel = [], []
    for h in heads:
        q = _half_swap(q_ref[:, h * d:(h + 1) * d].astype(F32), cq, sq)
        gate_t = lax.dot_general(km_scr[h * nb:(h + 1) * nb, :], q, (((1,), (1,)), ((), ())),
                                 precision=lax.Precision.HIGHEST, preferred_element_type=F32)
        sel.append(_moba_select(gate_t, i, nb))
        qb.append((q * scale).astype(BF16))

    def tile(j, carry, mask_fn):
        off = pl.multiple_of(j * blk, blk)
        s = [_nt_dot(qb[h], kr_scr[pl.ds(off, blk), h * d:(h + 1) * d]) for h in heads]
        s = [jnp.where(mask_fn(h), s[h], MASK_VALUE) for h in heads]
        return _softmax_tiles(s, lambda h: v_ref[pl.ds(off, blk), h * d:(h + 1) * d], carry)

    r_id = lax.broadcasted_iota(jnp.int32, (blk, blk), 0)
    c_id = lax.broadcasted_iota(jnp.int32, (blk, blk), 1)
    causal = c_id <= r_id
    carry = tile(i, [_softmax_init(blk, d) for _ in heads], lambda h: causal)

    def body(j, carry):
        def chosen(h):
            return jnp.sum(jnp.where(lane == j, sel[h], 0.0), axis=1, keepdims=True) > 0.5
        return tile(j, carry, chosen)

    carry = lax.fori_loop(0, i, body, carry)
    for h in heads:
        _, l, acc = carry[h]
        o_ref[:, h * d:(h + 1) * d] = (acc / l).astype(o_ref.dtype)


def moba_attention(zbc, cos_t, sin_t, batch, group0):
    t = zbc.shape[0]
    s = t // batch
    blk = MOBA_BLOCK
    nb = s // blk
    nh = MOBA_HEADS
    w = nh * HEAD_DIM
    est = 6 * _nbytes((s, w), BF16) + 4 * _nbytes((s, HEAD_DIM), F32) + 12 * nh * _nbytes((blk, blk), F32)
    return pl.pallas_call(
        functools.partial(_moba_kernel, nb=nb, nh=nh, scale=float(HEAD_DIM ** -0.5)),
        grid=(batch, nb),
        in_specs=[pl.BlockSpec((blk, w), lambda b, i: (b * nb + i, group0)),
                  pl.BlockSpec((s, w), lambda b, i: (b, group0 + 1)),
                  pl.BlockSpec((s, w), lambda b, i: (b, group0 + 2)),
                  pl.BlockSpec((blk, HEAD_DIM), lambda b, i: (b * nb + i, 0)),
                  pl.BlockSpec((blk, HEAD_DIM), lambda b, i: (b * nb + i, 0)),
                  pl.BlockSpec((s, HEAD_DIM), lambda b, i: (b, 0)),
                  pl.BlockSpec((s, HEAD_DIM), lambda b, i: (b, 0))],
        out_specs=pl.BlockSpec((blk, w), lambda b, i: (b * nb + i, 0)),
        out_shape=jax.ShapeDtypeStruct((t, w), BF16),
        scratch_shapes=[pltpu.VMEM((s, w), BF16), pltpu.VMEM((nh * nb, HEAD_DIM), F32)],
        compiler_params=_params(("parallel", "arbitrary"), est),
    )(zbc, zbc, zbc, cos_t, sin_t, cos_t, sin_t)


def _sb_kernel(q_ref, k_ref, v_ref, u_ref, o_ref, *, tq, nh, scale):
    i = pl.program_id(1)
    d = HEAD_DIM
    heads = range(nh)
    u = u_ref[...]
    qb = [(q_ref[:, h * d:(h + 1) * d].astype(F32) * scale).astype(BF16) for h in heads]

    def tile(j, carry, strict):
        off = pl.multiple_of(j * tq, tq)
        z = [_nt_dot(qb[h], k_ref[pl.ds(off, tq), h * d:(h + 1) * d]) for h in heads]
        hi, lo = [], []
        for h in heads:
            lsm = -(jnp.maximum(z[h], 0.0) + jnp.log(1.0 + jnp.exp(-jnp.abs(z[h]))))
            if strict is not None:
                lsm = jnp.where(strict, lsm, 0.0)
            hi.append(lsm.astype(BF16))
            lo.append((lsm - hi[h].astype(F32)).astype(BF16))
        incl = [jnp.dot(hi[h], u, preferred_element_type=F32) + jnp.dot(lo[h], u, preferred_element_type=F32)
                for h in heads]
        a = []
        for h in heads:
            ah = jnp.exp(z[h] + incl[h] + carry[h][0])
            if strict is not None:
                ah = jnp.where(strict, ah, 0.0)
            a.append(ah.astype(BF16))
        return [(carry[h][0] + incl[h][:, 0:1],
                 carry[h][1] + jnp.dot(a[h], v_ref[pl.ds(off, tq), h * d:(h + 1) * d], preferred_element_type=F32))
                for h in heads]

    r_id = lax.broadcasted_iota(jnp.int32, (tq, tq), 0)
    c_id = lax.broadcasted_iota(jnp.int32, (tq, tq), 1)
    init = [(jnp.zeros((tq, 1), F32), jnp.zeros((tq, d), F32)) for _ in heads]
    carry = tile(i, init, c_id < r_id)
    carry = lax.fori_loop(0, i, lambda n, c: tile(i - 1 - n, c, None), carry)
    for h in heads:
        o_ref[:, h * d:(h + 1) * d] = carry[h][1].astype(o_ref.dtype)


def stick_breaking_attention(zbc, batch, group0, tq=256):
    t = zbc.shape[0]
    s = t // batch
    nq = s // tq
    nh = SB_HEADS
    w = nh * HEAD_DIM
    u = (jnp.arange(tq)[:, None] >= jnp.arange(tq)[None, :]).astype(BF16)
    est = 8 * _nbytes((s, w), BF16) + 16 * nh * _nbytes((tq, tq), F32)
    return pl.pallas_call(
        functools.partial(_sb_kernel, tq=tq, nh=nh, scale=float(HEAD_DIM ** -0.5)),
        grid=(batch, nq),
        in_specs=[pl.BlockSpec((tq, w), lambda b, i: (b * nq + i, group0)),
                  pl.BlockSpec((s, w), lambda b, i: (b, group0 + 1)),
                  pl.BlockSpec((s, w), lambda b, i: (b, group0 + 2)),
                  pl.BlockSpec((tq, tq), lambda b, i: (0, 0))],
        out_specs=pl.BlockSpec((tq, w), lambda b, i: (b * nq + i, 0)),
        out_shape=jax.ShapeDtypeStruct((t, w), BF16),
        compiler_params=_params(("parallel", "arbitrary"), est),
    )(zbc, zbc, zbc, u)


def _merge_kernel(oa_ref, ob_ref, oc_ref, ga_ref, gb_ref, gc_ref, wa_ref, wb_ref, wc_ref, o_ref):
    def branch(o, g, w):
        return jax.nn.sigmoid(g[...].astype(F32)) * jnp.dot(o[...], w[...], preferred_element_type=F32)
    o_ref[...] = (branch(oa_ref, ga_ref, wa_ref) + branch(ob_ref, gb_ref, wb_ref)
                  + branch(oc_ref, gc_ref, wc_ref)).astype(o_ref.dtype)


def gated_merge(o_a, o_b, o_c, zg, wa, wb, wc, tm=512):
    m = o_a.shape[0]
    d = wa.shape[1]
    row = lambda a: pl.BlockSpec((tm, a.shape[1]), lambda i: (i, 0))
    full = lambda a: pl.BlockSpec(a.shape, lambda i: (0, 0))
    gate = lambda n: pl.BlockSpec((tm, d), lambda i: (i, n))
    est = 4 * _nbytes((d, d), BF16) + 12 * _nbytes((tm, d), F32)
    return pl.pallas_call(
        _merge_kernel,
        grid=(m // tm,),
        in_specs=[row(o_a), row(o_b), row(o_c), gate(0), gate(1), gate(2), full(wa), full(wb), full(wc)],
        out_specs=pl.BlockSpec((tm, d), lambda i: (i, 0)),
        out_shape=jax.ShapeDtypeStruct((m, d), BF16),
        compiler_params=_params(("parallel",), est),
    )(o_a, o_b, o_c, zg, zg, zg, wa, wb, wc)


def _xattn_kernel(q_ref, k_ref, v_ref, o_ref, *, heads, scale):
    hd = q_ref.shape[1] // heads
    for h in range(heads):
        cols = slice(h * hd, (h + 1) * hd)
        s = _nt_dot(q_ref[:, cols], k_ref[:, cols]) * scale
        p = jnp.exp(s - jnp.max(s, axis=1, keepdims=True))
        o = jnp.dot(p.astype(BF16), v_ref[:, cols], preferred_element_type=F32)
        o_ref[:, cols] = (o / jnp.sum(p, axis=1, keepdims=True)).astype(o_ref.dtype)


def cross_attention(q, k, v, batch, heads, tq=512):
    t, d = q.shape
    s = t // batch
    mlen = k.shape[0] // batch
    nq = s // tq
    est = 8 * _nbytes((tq, d), BF16) + 8 * _nbytes((mlen, d), BF16) + 8 * _nbytes((tq, mlen), F32)
    return pl.pallas_call(
        functools.partial(_xattn_kernel, heads=heads, scale=float((d // heads) ** -0.5)),
        grid=(batch, nq),
        in_specs=[pl.BlockSpec((tq, d), lambda b, i: (b * nq + i, 0)),
                  pl.BlockSpec((mlen, d), lambda b, i: (b, 0)),
                  pl.BlockSpec((mlen, d), lambda b, i: (b, 0))],
        out_specs=pl.BlockSpec((tq, d), lambda b, i: (b * nq + i, 0)),
        out_shape=jax.ShapeDtypeStruct((t, d), BF16),
        compiler_params=_params(("parallel", "parallel"), est),
    )(q, k, v)


def _swiglu_partial(x, w1, w3, w2):
    a = jnp.dot(x, w1, preferred_element_type=F32)
    b = jnp.dot(x, w3, preferred_element_type=F32)
    act = (a * jax.nn.sigmoid(a) * b).astype(BF16)
    return jnp.dot(act, w2, preferred_element_type=F32)


def _ffn_kernel(x_ref, w1_ref, w3_ref, w2_ref, r_ref, g_ref, h_ref, hn_ref, acc_ref):
    j = pl.program_id(1)

    @pl.when(j == 0)
    def _():
        acc_ref[...] = r_ref[...]

    acc_ref[...] += _swiglu_partial(x_ref[...], w1_ref[...], w3_ref[...], w2_ref[...])

    @pl.when(j == pl.num_programs(1) - 1)
    def _():
        h = acc_ref[...]
        h_ref[...] = h
        hn_ref[...] = _rms(h, g_ref[...]).astype(hn_ref.dtype)


def dense_ffn(x, w1, w3, w2, res, g_next, tm=512, tf=512):
    m, d = x.shape
    f = w1.shape[1]
    assert f % tf == 0
    est = 2 * (_nbytes((tm, d), BF16) + 3 * _nbytes((d, tf), BF16) + 2 * _nbytes((tm, d), F32)
               + _nbytes((tm, d), BF16)) + _nbytes((tm, d), F32) + 4 * _nbytes((tm, tf), F32)
    return pl.pallas_call(
        _ffn_kernel,
        grid=(m // tm, f // tf),
        in_specs=[pl.BlockSpec((tm, d), lambda i, j: (i, 0)),
                  pl.BlockSpec((d, tf), lambda i, j: (0, j)),
                  pl.BlockSpec((d, tf), lambda i, j: (0, j)),
                  pl.BlockSpec((tf, d), lambda i, j: (j, 0)),
                  pl.BlockSpec((tm, d), lambda i, j: (i, 0)),
                  pl.BlockSpec((1, d), lambda i, j: (0, 0))],
        out_specs=[pl.BlockSpec((tm, d), lambda i, j: (i, 0)),
                   pl.BlockSpec((tm, d), lambda i, j: (i, 0))],
        out_shape=[jax.ShapeDtypeStruct((m, d), F32), jax.ShapeDtypeStruct((m, d), BF16)],
        scratch_shapes=[pltpu.VMEM((tm, d), F32)],
        compiler_params=_params(("parallel", "arbitrary"), est),
    )(x, w1, w3, w2, res, g_next.reshape(1, d).astype(F32))


def _router_kernel(h_ref, g_ref, wr_ref, idx_ref, gate_ref):
    hn = _rms(h_ref[...], g_ref[...])
    logits = lax.dot_general(wr_ref[...], hn, (((1,), (1,)), ((), ())),
                             precision=lax.Precision.HIGHEST, preferred_element_type=F32)
    e_id = lax.broadcasted_iota(jnp.int32, logits.shape, 0)
    n_e = logits.shape[0]
    v1 = jnp.max(logits, axis=0, keepdims=True)
    i1 = jnp.min(jnp.where(logits == v1, e_id, n_e), axis=0, keepdims=True)
    rest = jnp.where(e_id == i1, -jnp.inf, logits)
    v2 = jnp.max(rest, axis=0, keepdims=True)
    i2 = jnp.min(jnp.where(rest == v2, e_id, n_e), axis=0, keepdims=True)
    e2 = jnp.exp(v2 - v1)
    idx_ref[0:1, :] = i1
    idx_ref[1:2, :] = i2
    gate_ref[0:1, :] = 1.0 / (1.0 + e2)
    gate_ref[1:2, :] = e2 / (1.0 + e2)


def moe_router(h, g, w_router, tm=512):
    m, d = h.shape
    n_e = w_router.shape[1]
    wr_t = w_router.T.astype(F32)
    est = 6 * _nbytes((tm, d), F32)
    return pl.pallas_call(
        _router_kernel,
        grid=(m // tm,),
        in_specs=[pl.BlockSpec((tm, d), lambda i: (i, 0)),
                  pl.BlockSpec((1, d), lambda i: (0, 0)),
                  pl.BlockSpec((n_e, d), lambda i: (0, 0))],
        out_specs=[pl.BlockSpec((MOE_TOPK, tm), lambda i: (0, i)),
                   pl.BlockSpec((MOE_TOPK, tm), lambda i: (0, i))],
        out_shape=[jax.ShapeDtypeStruct((MOE_TOPK, m), jnp.int32), jax.ShapeDtypeStruct((MOE_TOPK, m), F32)],
        compiler_params=_params(("parallel",), est),
    )(h, g.reshape(1, d).astype(F32), wr_t)


def _row_copy(src_hbm, src_row, dst_ref, dst_row, sem):
    return pltpu.make_async_copy(src_hbm.at[pl.ds(src_row, 1), :], dst_ref.at[pl.ds(dst_row, 1), :], sem)


def _gather_kernel(rows_ref, src_hbm, o_ref, sem):
    base = pl.program_id(0) * GATHER_ROWS

    def start(r, _):
        _row_copy(src_hbm, rows_ref[base + r], o_ref, r, sem).start()
        return 0

    def wait(r, _):
        _row_copy(src_hbm, 0, o_ref, r, sem).wait()
        return 0

    lax.fori_loop(0, GATHER_ROWS, start, 0, unroll=8)
    lax.fori_loop(0, GATHER_ROWS, wait, 0, unroll=8)


def gather_rows(src, rows):
    n = rows.shape[0]
    d = src.shape[1]
    return pl.pallas_call(
        _gather_kernel,
        grid_spec=pltpu.PrefetchScalarGridSpec(
            num_scalar_prefetch=1,
            grid=(n // GATHER_ROWS,),
            in_specs=[pl.BlockSpec(memory_space=pl.ANY)],
            out_specs=pl.BlockSpec((GATHER_ROWS, d), lambda i, rows: (i, 0)),
            scratch_shapes=[pltpu.SemaphoreType.DMA(())],
        ),
        out_shape=jax.ShapeDtypeStruct((n, d), src.dtype),
        compiler_params=_params(("arbitrary",), 4 * _nbytes((GATHER_ROWS, d), src.dtype)),
    )(rows, src)


def _expert_ffn_kernel(te_ref, tv_ref, x_ref, g_ref, w1_ref, w3_ref, w2_ref, o_ref, xn_ref):
    t = pl.program_id(0)
    j = pl.program_id(1)

    @pl.when(j == 0)
    def _():
        o_ref[...] = jnp.zeros_like(o_ref)
        xn_ref[...] = _rms(x_ref[...], g_ref[...]).astype(BF16)

    @pl.when(tv_ref[t] > 0)
    def _():
        o_ref[...] += _swiglu_partial(xn_ref[...], w1_ref[...], w3_ref[...], w2_ref[...])


def expert_ffn(xs, g, w1, w3, w2, tile_e, tile_valid, tf=512):
    n, d = xs.shape
    f = w1.shape[2]
    nf = f // tf
    tm = MOE_TILE
    last = nf - 1

    def jj(t, j, tv):
        return j * tv[t] + last * (1 - tv[t])

    est = 2 * (2 * _nbytes((tm, d), F32) + 3 * _nbytes((d, tf), BF16)) + _nbytes((tm, d), BF16) \
        + 4 * _nbytes((tm, tf), F32) + _nbytes((tm, d), F32)
    return pl.pallas_call(
        _expert_ffn_kernel,
        grid_spec=pltpu.PrefetchScalarGridSpec(
            num_scalar_prefetch=2,
            grid=(n // tm, nf),
            in_specs=[pl.BlockSpec((tm, d), lambda t, j, te, tv: (t, 0)),
                      pl.BlockSpec((1, d), lambda t, j, te, tv: (0, 0)),
                      pl.BlockSpec((None, d, tf), lambda t, j, te, tv: (te[t], 0, jj(t, j, tv))),
                      pl.BlockSpec((None, d, tf), lambda t, j, te, tv: (te[t], 0, jj(t, j, tv))),
                      pl.BlockSpec((None, tf, d), lambda t, j, te, tv: (te[t], jj(t, j, tv), 0))],
            out_specs=pl.BlockSpec((tm, d), lambda t, j, te, tv: (t, 0)),
            scratch_shapes=[pltpu.VMEM((tm, d), BF16)],
        ),
        out_shape=jax.ShapeDtypeStruct((n, d), F32),
        compiler_params=_params(("arbitrary", "arbitrary"), est),
    )(tile_e, tile_valid, xs, g.reshape(1, d).astype(F32), w1, w3, w2)


def _combine_kernel(slots_ref, ys_hbm, h_ref, gate_ref, g_ref, o_ref, buf0, buf1, sem):
    base = pl.program_id(0) * GATHER_ROWS

    def start(r, _):
        flat = (base + r) * MOE_TOPK
        _row_copy(ys_hbm, slots_ref[flat], buf0, r, sem).start()
        _row_copy(ys_hbm, slots_ref[flat + 1], buf1, r, sem).start()
        return 0

    def wait(r, _):
        _row_copy(ys_hbm, 0, buf0, r, sem).wait()
        _row_copy(ys_hbm, 0, buf1, r, sem).wait()
        return 0

    lax.fori_loop(0, GATHER_ROWS, start, 0, unroll=8)
    lax.fori_loop(0, GATHER_ROWS, wait, 0, unroll=8)
    gate = gate_ref[...]
    h = h_ref[...] + gate[:, 0:1] * buf0[...] + gate[:, 1:2] * buf1[...]
    o_ref[...] = _rms(h, g_ref[...])


def moe_combine_norm(ys, slot_of_flat, h, gates, g_out):
    m, d = h.shape
    tm = GATHER_ROWS
    return pl.pallas_call(
        _combine_kernel,
        grid_spec=pltpu.PrefetchScalarGridSpec(
            num_scalar_prefetch=1,
            grid=(m // tm,),
            in_specs=[pl.BlockSpec(memory_space=pl.ANY),
                      pl.BlockSpec((tm, d), lambda i, s: (i, 0)),
                      pl.BlockSpec((tm, MOE_TOPK), lambda i, s: (i, 0)),
                      pl.BlockSpec((1, d), lambda i, s: (0, 0))],
            out_specs=pl.BlockSpec((tm, d), lambda i, s: (i, 0)),
            scratch_shapes=[pltpu.VMEM((tm, d), F32), pltpu.VMEM((tm, d), F32), pltpu.SemaphoreType.DMA(())],
        ),
        out_shape=jax.ShapeDtypeStruct((m, d), F32),
        compiler_params=_params(("arbitrary",), 8 * _nbytes((tm, d), F32)),
    )(slot_of_flat, ys, h, gates, g_out.reshape(1, d).astype(F32))


def moe_routing_metadata(top_idx, n_tokens):
    tk = n_tokens * MOE_TOPK
    flat_e = top_idx.T.reshape(-1)
    onehot = (flat_e[:, None] == jnp.arange(N_EXPERTS)[None, :]).astype(jnp.int32)
    within = jnp.take_along_axis(jnp.cumsum(onehot, axis=0), flat_e[:, None], axis=1)[:, 0] - 1
    counts = jnp.sum(onehot, axis=0)
    padded = (counts + MOE_TILE - 1) // MOE_TILE * MOE_TILE
    pad_end = jnp.cumsum(padded)
    slot_of_flat = ((pad_end - padded)[flat_e] + within).astype(jnp.int32)
    n_tiles = tk // MOE_TILE + N_EXPERTS
    slot_tok = jnp.zeros((n_tiles * MOE_TILE,), jnp.int32).at[slot_of_flat].set(
        jnp.arange(tk, dtype=jnp.int32) // MOE_TOPK)
    tile_start = jnp.arange(n_tiles, dtype=jnp.int32) * MOE_TILE
    tile_valid = (tile_start < pad_end[-1]).astype(jnp.int32)
    tile_e = jnp.minimum(jnp.sum(tile_start[:, None] >= pad_end[None, :], axis=1), N_EXPERTS - 1)
    last_e = jnp.max(jnp.where(tile_valid > 0, tile_e, 0))
    tile_e = jnp.where(tile_valid > 0, tile_e, last_e).astype(jnp.int32)
    return slot_tok, slot_of_flat, tile_e, tile_valid


def _rope_partner(w):
    half = w.shape[-1] // 2
    return jnp.concatenate([-w[..., half:], w[..., :half]], axis=-1)


def _rope_tables(positions):
    pos = positions.astype(F32).reshape(-1, 1)

    def cs(half):
        inv_freq = ROPE_THETA ** (-jnp.arange(half, dtype=F32) / half)
        ang = pos * inv_freq
        return jnp.cos(ang), jnp.sin(ang)

    c32, s32 = cs(MLA_ROPE // 2)
    zeros = jnp.zeros((pos.shape[0], LANES // 2), F32)
    mla_c = jnp.concatenate([c32, c32, zeros], axis=1)
    mla_s = jnp.concatenate([s32, s32, zeros], axis=1)
    c64, s64 = cs(HEAD_DIM // 2)
    moba_c = jnp.concatenate([c64, c64], axis=1)
    moba_s = jnp.concatenate([-s64, s64], axis=1)
    return mla_c, mla_s, moba_c, moba_s


def _layer_weights(w_in, w_uq, w_ukv):
    d = w_in.shape[0]
    o = 0
    w_cq = w_in[:, o:o + MLA_LORA]; o += MLA_LORA
    w_ckv = w_in[:, o:o + MLA_LORA]; o += MLA_LORA
    w_kr = w_in[:, o:o + MLA_ROPE]; o += MLA_ROPE
    nbc = 3 * (MOBA_HEADS + SB_HEADS) * HEAD_DIM
    w_bc = w_in[:, o:o + nbc]; o += nbc
    w_g = w_in[:, o:]
    w1 = jnp.concatenate([w_cq, w_ckv, w_kr, _rope_partner(w_kr)], axis=1).astype(BF16)
    uq = w_uq.reshape(MLA_LORA, MLA_HEADS, MLA_NOPE + MLA_ROPE)
    uq_rope = uq[..., MLA_NOPE:]
    wq = jnp.concatenate([uq[..., :MLA_NOPE], uq_rope, _rope_partner(uq_rope)], axis=-1)
    wq = wq.reshape(MLA_LORA, MLA_HEADS * 2 * LANES).astype(BF16)
    ukv = w_ukv.reshape(MLA_LORA, MLA_HEADS, MLA_NOPE + MLA_V)
    wkv = jnp.concatenate([ukv[..., :MLA_NOPE].reshape(MLA_LORA, -1), ukv[..., MLA_NOPE:].reshape(MLA_LORA, -1)],
                          axis=1).astype(BF16)
    return w1, w_bc.astype(BF16), w_g.astype(BF16), wq, wkv


def kernel(x, mem, positions, g_mix, w_in, g_cq, g_ckv, w_uq, w_ukv, w_up_a, w_up_b, w_up_c, w_o, g_x, g_mem, w_xq, w_xk, w_xv, w_xo, g_ffn, w_ff1, w_ff3, w_ff2, w_router, w_e1, w_e3, w_e2, g_final):
    batch, seq, d = x.shape
    t = batch * seq
    depth = g_mix.shape[0]
    assert depth == 2, "the final norm is fused into the expert layer, which must come last"
    mem2 = mem.reshape(-1, d)
    mla_c, mla_s, moba_c, moba_s = _rope_tables(positions)
    assert MOBA_HEADS == SB_HEADS

    h = x.reshape(t, d)
    hn = rmsnorm(h, g_mix[0], BF16)
    out = None
    for l in range(depth):
        w1, w_bc, w_g, wq, wkv = _layer_weights(w_in[l], w_uq[l], w_ukv[l])
        z1 = matmul(hn, w1, BF16, tn=w1.shape[1])
        zbc = matmul(hn, w_bc, BF16, tn=1280)
        zg = matmul(hn, w_g, BF16, tn=1024)
        q_a, k_a, v_a = mla_prep(z1, g_cq[l], g_ckv[l], wq, wkv, mla_c, mla_s)
        o_a = causal_attention(q_a, k_a, v_a, batch, MLA_HEADS, 2 * LANES, MLA_V)
        o_b = moba_attention(zbc, moba_c, moba_s, batch, 0)
        o_c = stick_breaking_attention(zbc, batch, 3)
        merged = gated_merge(o_a, o_b, o_c, zg, w_up_a[l].astype(BF16), w_up_b[l].astype(BF16),
                             w_up_c[l].astype(BF16))
        h, hn = matmul_res_norm(merged, w_o[l].astype(BF16), h, g_x[l])
        mn = rmsnorm(mem2, g_mem[l], BF16)
        q_x = matmul(hn, w_xq[l].astype(BF16), BF16)
        k_x = matmul(mn, w_xk[l].astype(BF16), BF16)
        v_x = matmul(mn, w_xv[l].astype(BF16), BF16)
        o_x = cross_attention(q_x, k_x, v_x, batch, XATTN_HEADS)
        h, hn = matmul_res_norm(o_x, w_xo[l].astype(BF16), h, g_ffn[l])
        g_next = g_mix[l + 1] if l + 1 < depth else g_final
        if l % 2 == 0:
            e = l // 2
            h, hn = dense_ffn(hn, w_ff1[e].astype(BF16), w_ff3[e].astype(BF16), w_ff2[e].astype(BF16), h, g_next)
            out = hn
        else:
            e = l // 2
            top_idx, top_gate = moe_router(h, g_ffn[l], w_router[e])
            slot_tok, slot_of_flat, tile_e, tile_valid = moe_routing_metadata(top_idx, t)
            xs = gather_rows(h, slot_tok)
            ys = expert_ffn(xs, g_ffn[l], w_e1[e].astype(BF16), w_e3[e].astype(BF16), w_e2[e].astype(BF16),
                            tile_e, tile_valid)
            out = moe_combine_norm(ys, slot_of_flat, h, top_gate.T, g_next)
    return out.reshape(batch, seq, d).astype(x.dtype)
```

```python
import functools

import numpy as np
import jax
import jax.numpy as jnp
from jax import lax
from jax.experimental import pallas as pl
from jax.experimental.pallas import tpu as pltpu

F32 = jnp.float32
BF16 = jnp.bfloat16

HEAD_DIM = 128
MLA_HEADS = 6
MLA_LORA = 512
MLA_NOPE = 128
MLA_ROPE = 64
MLA_V = 128
MOBA_HEADS = 5
MOBA_BLOCK = 256
MOBA_TOPK = 3
SB_HEADS = 5
N_BRANCH = 3
ROPE_THETA = 10000.0
XATTN_HEADS = 4
N_EXPERTS = 8
MOE_TOPK = 2
RMS_EPS = 1e-6

LANES = 128
V7X_VMEM_BYTES = 64 * 1024 * 1024
VMEM_CEILING = V7X_VMEM_BYTES - 8 * 1024 * 1024

MASK_VALUE = -1e30
EXP_UNDERFLOW = -104.0

MOE_TILE = 512
GATHER_ROWS = 256


def _params(sem, est_bytes):
    limit = int(min(VMEM_CEILING, max(32 * 1024 * 1024, est_bytes * 5 // 4)))
    return pltpu.CompilerParams(dimension_semantics=sem, vmem_limit_bytes=limit)


def _nbytes(shape, dtype):
    return int(np.prod(shape)) * jnp.dtype(dtype).itemsize


def _rms(x, g):
    return x * lax.rsqrt(jnp.mean(x * x, axis=-1, keepdims=True) + RMS_EPS) * g


def _rmsnorm_kernel(x_ref, g_ref, o_ref):
    o_ref[...] = _rms(x_ref[...].astype(F32), g_ref[...]).astype(o_ref.dtype)


def rmsnorm(x, g, out_dtype, tm=512):
    m, d = x.shape
    tm = min(tm, m)
    return pl.pallas_call(
        _rmsnorm_kernel,
        grid=(m // tm,),
        in_specs=[pl.BlockSpec((tm, d), lambda i: (i, 0)),
                  pl.BlockSpec((1, d), lambda i: (0, 0))],
        out_specs=pl.BlockSpec((tm, d), lambda i: (i, 0)),
        out_shape=jax.ShapeDtypeStruct((m, d), out_dtype),
        compiler_params=_params(("parallel",), 4 * _nbytes((tm, d), F32)),
    )(x, g.reshape(1, d).astype(F32))


def _mm_kernel(a_ref, w_ref, o_ref):
    o_ref[...] = jnp.dot(a_ref[...], w_ref[...], preferred_element_type=F32).astype(o_ref.dtype)


def matmul(a, w, out_dtype, tm=1024, tn=1024):
    m, k = a.shape
    n = w.shape[1]
    tm, tn = min(tm, m), min(tn, n)
    assert m % tm == 0 and n % tn == 0, (m, n, tm, tn)
    est = 2 * (_nbytes((tm, k), a.dtype) + _nbytes((k, tn), w.dtype) + _nbytes((tm, tn), out_dtype)) \
        + _nbytes((tm, tn), F32)
    return pl.pallas_call(
        _mm_kernel,
        grid=(m // tm, n // tn),
        in_specs=[pl.BlockSpec((tm, k), lambda i, j: (i, 0)),
                  pl.BlockSpec((k, tn), lambda i, j: (0, j))],
        out_specs=pl.BlockSpec((tm, tn), lambda i, j: (i, j)),
        out_shape=jax.ShapeDtypeStruct((m, n), out_dtype),
        compiler_params=_params(("parallel", "parallel"), est),
    )(a, w)


def _mm_res_norm_kernel(a_ref, w_ref, r_ref, g_ref, h_ref, hn_ref):
    h = r_ref[...] + jnp.dot(a_ref[...], w_ref[...], preferred_element_type=F32)
    h_ref[...] = h
    hn_ref[...] = _rms(h, g_ref[...]).astype(hn_ref.dtype)


def matmul_res_norm(a, w, res, g, tm=512):
    m, k = a.shape
    n = w.shape[1]
    est = 2 * (_nbytes((tm, k), a.dtype) + 2 * _nbytes((tm, n), F32) + _nbytes((tm, n), BF16)) \
        + _nbytes((k, n), w.dtype) + 2 * _nbytes((tm, n), F32)
    return pl.pallas_call(
        _mm_res_norm_kernel,
        grid=(m // tm,),
        in_specs=[pl.BlockSpec((tm, k), lambda i: (i, 0)),
                  pl.BlockSpec((k, n), lambda i: (0, 0), pipeline_mode=pl.Buffered(1)),
                  pl.BlockSpec((tm, n), lambda i: (i, 0)),
                  pl.BlockSpec((1, n), lambda i: (0, 0))],
        out_specs=[pl.BlockSpec((tm, n), lambda i: (i, 0)),
                   pl.BlockSpec((tm, n), lambda i: (i, 0))],
        out_shape=[jax.ShapeDtypeStruct((m, n), F32), jax.ShapeDtypeStruct((m, n), BF16)],
        compiler_params=_params(("parallel",), est),
    )(a, w, res, g.reshape(1, n).astype(F32))


def _half_swap(y, c, s):
    return y * c + pltpu.roll(y, LANES // 2, 1) * s


def _mla_prep_kernel(z_ref, gq_ref, gkv_ref, wq_ref, wkv_ref, c_ref, s_ref, q_ref, k_ref, v_ref, *, scale):
    z = z_ref[...].astype(F32)
    c = c_ref[...]
    s = s_ref[...]
    nq = _rms(z[:, :MLA_LORA], gq_ref[...]).astype(BF16)
    nkv = _rms(z[:, MLA_LORA:2 * MLA_LORA], gkv_ref[...]).astype(BF16)
    q = jnp.dot(nq, wq_ref[...], preferred_element_type=F32)
    kv = jnp.dot(nkv, wkv_ref[...], preferred_element_type=F32)
    k_pe = _half_swap(z[:, 2 * MLA_LORA:], c, s).astype(BF16)
    for h in range(MLA_HEADS):
        lo = 2 * LANES * h
        q_ref[:, lo:lo + LANES] = (q[:, lo:lo + LANES] * scale).astype(BF16)
        q_ref[:, lo + LANES:lo + 2 * LANES] = (_half_swap(q[:, lo + LANES:lo + 2 * LANES], c, s) * scale).astype(BF16)
        k_ref[:, lo:lo + LANES] = kv[:, LANES * h:LANES * (h + 1)].astype(BF16)
        k_ref[:, lo + LANES:lo + 2 * LANES] = k_pe
    v_ref[...] = kv[:, MLA_HEADS * MLA_NOPE:].astype(BF16)


def mla_prep(z1, g_cq, g_ckv, wq, wkv, cos_t, sin_t, tm=512):
    m, zc = z1.shape
    nq, nkv = wq.shape[1], wkv.shape[1]
    scale = float((MLA_NOPE + MLA_ROPE) ** -0.5)
    row = lambda c: pl.BlockSpec((tm, c), lambda i: (i, 0))
    full = lambda a: pl.BlockSpec(a.shape, lambda i: (0, 0))
    g_cq = g_cq.reshape(1, -1).astype(F32)
    g_ckv = g_ckv.reshape(1, -1).astype(F32)
    est = 4 * _nbytes((tm, nq), F32) + 4 * (_nbytes(wq.shape, BF16) + _nbytes(wkv.shape, BF16))
    return pl.pallas_call(
        functools.partial(_mla_prep_kernel, scale=scale),
        grid=(m // tm,),
        in_specs=[row(zc), full(g_cq), full(g_ckv), full(wq), full(wkv), row(LANES), row(LANES)],
        out_specs=[row(nq), row(nq), row(MLA_HEADS * MLA_V)],
        out_shape=[jax.ShapeDtypeStruct((m, nq), BF16), jax.ShapeDtypeStruct((m, nq), BF16),
                   jax.ShapeDtypeStruct((m, MLA_HEADS * MLA_V), BF16)],
        compiler_params=_params(("parallel",), est),
    )(z1, g_cq, g_ckv, wq, wkv, cos_t, sin_t)


def _nt_dot(a, b):
    return lax.dot_general(a, b, (((1,), (1,)), ((), ())), preferred_element_type=F32)


def _softmax_tiles(scores, load_v, carry):
    p, stats = [], []
    for s, (m, l, _) in zip(scores, carry):
        m_new = jnp.maximum(m, jnp.max(s, axis=1, keepdims=True))
        alpha = jnp.exp(m - m_new)
        ph = jnp.exp(s - m_new)
        stats.append((m_new, alpha, alpha * l + jnp.sum(ph, axis=1, keepdims=True)))
        p.append(ph.astype(BF16))
    return [(m_new, l_new, alpha * acc + jnp.dot(p[h], load_v(h), preferred_element_type=F32))
            for h, ((m_new, alpha, l_new), (_, _, acc)) in enumerate(zip(stats, carry))]


def _softmax_init(rows, dv):
    return (jnp.full((rows, 1), MASK_VALUE, F32), jnp.zeros((rows, 1), F32), jnp.zeros((rows, dv), F32))


def _causal_attn_kernel(q_ref, k_ref, v_ref, o_ref, *, tq, tk, dk, dv, hpg):
    i = pl.program_id(2)
    n_full = (i * tq) // tk
    heads = range(hpg)
    q = [q_ref[:, h * dk:(h + 1) * dk] for h in heads]

    def tile(j, carry, mask):
        off = pl.multiple_of(j * tk, tk)
        s = [_nt_dot(q[h], k_ref[pl.ds(off, tk), h * dk:(h + 1) * dk]) for h in heads]
        if mask is not None:
            s = [jnp.where(mask, sh, MASK_VALUE) for sh in s]
        return _softmax_tiles(s, lambda h: v_ref[pl.ds(off, tk), h * dv:(h + 1) * dv], carry)

    carry = lax.fori_loop(0, n_full, lambda j, c: tile(j, c, None), [_softmax_init(tq, dv) for _ in heads])
    qpos = i * tq + lax.broadcasted_iota(jnp.int32, (tq, tk), 0)
    kpos = n_full * tk + lax.broadcasted_iota(jnp.int32, (tq, tk), 1)
    carry = tile(n_full, carry, kpos <= qpos)
    for h in heads:
        _, l, acc = carry[h]
        o_ref[:, h * dv:(h + 1) * dv] = (acc / l).astype(o_ref.dtype)


def causal_attention(q, k, v, batch, heads, dk, dv, tq=256, tk=512, hpg=3):
    t = q.shape[0]
    s = t // batch
    nq = s // tq
    assert tk % tq == 0 and s % tk == 0 and heads % hpg == 0
    est = 4 * hpg * (_nbytes((s, dk), BF16) + _nbytes((s, dv), BF16)) + 8 * hpg * _nbytes((tq, tk), F32)
    return pl.pallas_call(
        functools.partial(_causal_attn_kernel, tq=tq, tk=tk, dk=dk, dv=dv, hpg=hpg),
        grid=(batch, heads // hpg, nq),
        in_specs=[pl.BlockSpec((tq, hpg * dk), lambda b, g, i: (b * nq + i, g)),
                  pl.BlockSpec((s, hpg * dk), lambda b, g, i: (b, g)),
                  pl.BlockSpec((s, hpg * dv), lambda b, g, i: (b, g))],
        out_specs=pl.BlockSpec((tq, hpg * dv), lambda b, g, i: (b * nq + i, g)),
        out_shape=jax.ShapeDtypeStruct((t, heads * dv), BF16),
        compiler_params=_params(("parallel", "parallel", "arbitrary"), est),
    )(q, k, v)


def _moba_select(gate_t, i, nb):
    nq = gate_t.shape[1]
    row = lax.broadcasted_iota(jnp.int32, (nb, nq), 0)
    rank = jnp.zeros((nb, nq), jnp.int32)
    for jj in range(nb):
        gj = gate_t[jj:jj + 1, :]
        ahead = jnp.logical_or(gj > gate_t, jnp.logical_and(gj == gate_t, jj < row))
        rank = rank + jnp.where(jnp.logical_and(ahead, jj < i), 1, 0)
    sel_t = jnp.where(jnp.logical_and(row < i, rank < MOBA_TOPK), 1.0, 0.0)
    sel_t = jnp.concatenate([sel_t, jnp.zeros((LANES - nb, nq), F32)], axis=0)
    return sel_t.T


def _moba_kernel(q_ref, k_ref, v_ref, cq_ref, sq_ref, ck_ref, sk_ref, o_ref, kr_scr, km_scr, *, nb, nh, scale):
    i = pl.program_id(1)
    blk = MOBA_BLOCK
    d = HEAD_DIM
    heads = range(nh)

    @pl.when(i == 0)
    def _():
        def prep(j, _):
            rows = pl.ds(pl.multiple_of(j * blk, blk), blk)
            c, s = ck_ref[rows, :], sk_ref[rows, :]
            for h in heads:
                kj = _half_swap(k_ref[rows, h * d:(h + 1) * d].astype(F32), c, s)
                km_scr[pl.ds(h * nb + j, 1), :] = jnp.mean(kj, axis=0, keepdims=True)
                kr_scr[rows, h * d:(h + 1) * d] = kj.astype(BF16)
            return 0
        lax.fori_loop(0, nb, prep, 0)

    cq, sq = cq_ref[...], sq_ref[...]
    lane = lax.broadcasted_iota(jnp.int32, (blk, LANES), 1)
    qb, sel = [], []
    for h in heads:
        q = _half_swap(q_ref[:, h * d:(h + 1) * d].astype(F32), cq, sq)
        gate_t = lax.dot_general(km_scr[h * nb:(h + 1) * nb, :], q, (((1,), (1,)), ((), ())),
                                 precision=lax.Precision.HIGHEST, preferred_element_type=F32)
        sel.append(_moba_select(gate_t, i, nb))
        qb.append((q * scale).astype(BF16))

    def tile(j, carry, mask_fn):
        off = pl.multiple_of(j * blk, blk)
        s = [_nt_dot(qb[h], kr_scr[pl.ds(off, blk), h * d:(h + 1) * d]) for h in heads]
        s = [jnp.where(mask_fn(h), s[h], MASK_VALUE) for h in heads]
        return _softmax_tiles(s, lambda h: v_ref[pl.ds(off, blk), h * d:(h + 1) * d], carry)

    r_id = lax.broadcasted_iota(jnp.int32, (blk, blk), 0)
    c_id = lax.broadcasted_iota(jnp.int32, (blk, blk), 1)
    causal = c_id <= r_id
    carry = tile(i, [_softmax_init(blk, d) for _ in heads], lambda h: causal)

    def body(j, carry):
        def chosen(h):
            return jnp.sum(jnp.where(lane == j, sel[h], 0.0), axis=1, keepdims=True) > 0.5
        return tile(j, carry, chosen)

    carry = lax.fori_loop(0, i, body, carry)
    for h in heads:
        _, l, acc = carry[h]
        o_ref[:, h * d:(h + 1) * d] = (acc / l).astype(o_ref.dtype)


def moba_attention(zbc, cos_t, sin_t, batch, group0):
    t = zbc.shape[0]
    s = t // batch
    blk = MOBA_BLOCK
    nb = s // blk
    nh = MOBA_HEADS
    w = nh * HEAD_DIM
    est = 6 * _nbytes((s, w), BF16) + 4 * _nbytes((s, HEAD_DIM), F32) + 12 * nh * _nbytes((blk, blk), F32)
    return pl.pallas_call(
        functools.partial(_moba_kernel, nb=nb, nh=nh, scale=float(HEAD_DIM ** -0.5)),
        grid=(batch, nb),
        in_specs=[pl.BlockSpec((blk, w), lambda b, i: (b * nb + i, group0)),
                  pl.BlockSpec((s, w), lambda b, i: (b, group0 + 1)),
                  pl.BlockSpec((s, w), lambda b, i: (b, group0 + 2)),
                  pl.BlockSpec((blk, HEAD_DIM), lambda b, i: (b * nb + i, 0)),
                  pl.BlockSpec((blk, HEAD_DIM), lambda b, i: (b * nb + i, 0)),
                  pl.BlockSpec((s, HEAD_DIM), lambda b, i: (b, 0)),
                  pl.BlockSpec((s, HEAD_DIM), lambda b, i: (b, 0))],
        out_specs=pl.BlockSpec((blk, w), lambda b, i: (b * nb + i, 0)),
        out_shape=jax.ShapeDtypeStruct((t, w), BF16),
        scratch_shapes=[pltpu.VMEM((s, w), BF16), pltpu.VMEM((nh * nb, HEAD_DIM), F32)],
        compiler_params=_params(("parallel", "arbitrary"), est),
    )(zbc, zbc, zbc, cos_t, sin_t, cos_t, sin_t)


def _sb_kernel(q_ref, k_ref, v_ref, u_ref, o_ref, *, tq, nh, scale):
    i = pl.program_id(1)
    d = HEAD_DIM
    heads = range(nh)
    u = u_ref[...]
    qb = [(q_ref[:, h * d:(h + 1) * d].astype(F32) * scale).astype(BF16) for h in heads]

    def tile(j, carry, strict):
        off = pl.multiple_of(j * tq, tq)
        z = [_nt_dot(qb[h], k_ref[pl.ds(off, tq), h * d:(h + 1) * d]) for h in heads]
        hi, lo = [], []
        for h in heads:
            lsm = -(jnp.maximum(z[h], 0.0) + jnp.log(1.0 + jnp.exp(-jnp.abs(z[h]))))
            if strict is not None:
                lsm = jnp.where(strict, lsm, 0.0)
            hi.append(lsm.astype(BF16))
            lo.append((lsm - hi[h].astype(F32)).astype(BF16))
        incl = [jnp.dot(hi[h], u, preferred_element_type=F32) + jnp.dot(lo[h], u, preferred_element_type=F32)
                for h in heads]
        a = []
        for h in heads:
            ah = jnp.exp(jnp.minimum(z[h] + incl[h], 0.0) + carry[h][0])
            if strict is not None:
                ah = jnp.where(strict, ah, 0.0)
            a.append(ah.astype(BF16))
        return [(carry[h][0] + incl[h][:, 0:1],
                 carry[h][1] + jnp.dot(a[h], v_ref[pl.ds(off, tq), h * d:(h + 1) * d], preferred_element_type=F32))
                for h in heads]

    r_id = lax.broadcasted_iota(jnp.int32, (tq, tq), 0)
    c_id = lax.broadcasted_iota(jnp.int32, (tq, tq), 1)
    init = [(jnp.zeros((tq, 1), F32), jnp.zeros((tq, d), F32)) for _ in heads]
    carry = tile(i, init, c_id < r_id)

    def live(carry):
        worst = carry[0][0]
        for h in heads[1:]:
            worst = jnp.maximum(worst, carry[h][0])
        return (jnp.max(worst) > EXP_UNDERFLOW).astype(jnp.int32)

    def body(state):
        n, _, carry = state
        carry = tile(i - 1 - n, carry, None)
        return n + 1, live(carry), carry

    _, _, carry = lax.while_loop(lambda st: jnp.logical_and(st[0] < i, st[1] > 0), body,
                                 (jnp.int32(0), live(carry), carry))
    for h in heads:
        o_ref[:, h * d:(h + 1) * d] = carry[h][1].astype(o_ref.dtype)


def stick_breaking_attention(zbc, batch, group0, tq=256):
    t = zbc.shape[0]
    s = t // batch
    nq = s // tq
    nh = SB_HEADS
    w = nh * HEAD_DIM
    u = (jnp.arange(tq)[:, None] >= jnp.arange(tq)[None, :]).astype(BF16)
    est = 8 * _nbytes((s, w), BF16) + 16 * nh * _nbytes((tq, tq), F32)
    return pl.pallas_call(
        functools.partial(_sb_kernel, tq=tq, nh=nh, scale=float(HEAD_DIM ** -0.5)),
        grid=(batch, nq),
        in_specs=[pl.BlockSpec((tq, w), lambda b, i: (b * nq + i, group0)),
                  pl.BlockSpec((s, w), lambda b, i: (b, group0 + 1)),
                  pl.BlockSpec((s, w), lambda b, i: (b, group0 + 2)),
                  pl.BlockSpec((tq, tq), lambda b, i: (0, 0))],
        out_specs=pl.BlockSpec((tq, w), lambda b, i: (b * nq + i, 0)),
        out_shape=jax.ShapeDtypeStruct((t, w), BF16),
        compiler_params=_params(("parallel", "arbitrary"), est),
    )(zbc, zbc, zbc, u)


def _merge_kernel(oa_ref, ob_ref, oc_ref, ga_ref, gb_ref, gc_ref, wa_ref, wb_ref, wc_ref, o_ref):
    def branch(o, g, w):
        return jax.nn.sigmoid(g[...].astype(F32)) * jnp.dot(o[...], w[...], preferred_element_type=F32)
    o_ref[...] = (branch(oa_ref, ga_ref, wa_ref) + branch(ob_ref, gb_ref, wb_ref)
                  + branch(oc_ref, gc_ref, wc_ref)).astype(o_ref.dtype)


def gated_merge(o_a, o_b, o_c, zg, wa, wb, wc, tm=512):
    m = o_a.shape[0]
    d = wa.shape[1]
    row = lambda a: pl.BlockSpec((tm, a.shape[1]), lambda i: (i, 0))
    full = lambda a: pl.BlockSpec(a.shape, lambda i: (0, 0))
    gate = lambda n: pl.BlockSpec((tm, d), lambda i: (i, n))
    est = 4 * _nbytes((d, d), BF16) + 12 * _nbytes((tm, d), F32)
    return pl.pallas_call(
        _merge_kernel,
        grid=(m // tm,),
        in_specs=[row(o_a), row(o_b), row(o_c), gate(0), gate(1), gate(2), full(wa), full(wb), full(wc)],
        out_specs=pl.BlockSpec((tm, d), lambda i: (i, 0)),
        out_shape=jax.ShapeDtypeStruct((m, d), BF16),
        compiler_params=_params(("parallel",), est),
    )(o_a, o_b, o_c, zg, zg, zg, wa, wb, wc)


def _xattn_kernel(q_ref, k_ref, v_ref, o_ref, *, heads, scale):
    hd = q_ref.shape[1] // heads
    for h in range(heads):
        cols = slice(h * hd, (h + 1) * hd)
        s = _nt_dot(q_ref[:, cols], k_ref[:, cols]) * scale
        p = jnp.exp(s - jnp.max(s, axis=1, keepdims=True))
        o = jnp.dot(p.astype(BF16), v_ref[:, cols], preferred_element_type=F32)
        o_ref[:, cols] = (o / jnp.sum(p, axis=1, keepdims=True)).astype(o_ref.dtype)


def cross_attention(q, k, v, batch, heads, tq=512):
    t, d = q.shape
    s = t // batch
    mlen = k.shape[0] // batch
    nq = s // tq
    est = 8 * _nbytes((tq, d), BF16) + 8 * _nbytes((mlen, d), BF16) + 8 * _nbytes((tq, mlen), F32)
    return pl.pallas_call(
        functools.partial(_xattn_kernel, heads=heads, scale=float((d // heads) ** -0.5)),
        grid=(batch, nq),
        in_specs=[pl.BlockSpec((tq, d), lambda b, i: (b * nq + i, 0)),
                  pl.BlockSpec((mlen, d), lambda b, i: (b, 0)),
                  pl.BlockSpec((mlen, d), lambda b, i: (b, 0))],
        out_specs=pl.BlockSpec((tq, d), lambda b, i: (b * nq + i, 0)),
        out_shape=jax.ShapeDtypeStruct((t, d), BF16),
        compiler_params=_params(("parallel", "parallel"), est),
    )(q, k, v)


def _swiglu_up(x, w1, w3):
    a = jnp.dot(x, w1, preferred_element_type=F32)
    b = jnp.dot(x, w3, preferred_element_type=F32)
    return (a * jax.nn.sigmoid(a) * b).astype(BF16)


def _ffn_up_kernel(x_ref, w1_ref, w3_ref, o_ref):
    o_ref[...] = _swiglu_up(x_ref[...], w1_ref[...], w3_ref[...])


def dense_ffn(x, w1, w3, w2, res, g_next, tm_up=1024, tf=512, tm_down=256):
    m, d = x.shape
    f = w1.shape[1]
    assert f % tf == 0
    est = 2 * (_nbytes((tm_up, d), BF16) + 2 * _nbytes((d, tf), BF16) + _nbytes((tm_up, tf), BF16)) \
        + 4 * _nbytes((tm_up, tf), F32)
    act = pl.pallas_call(
        _ffn_up_kernel,
        grid=(m // tm_up, f // tf),
        in_specs=[pl.BlockSpec((tm_up, d), lambda i, j: (i, 0)),
                  pl.BlockSpec((d, tf), lambda i, j: (0, j)),
                  pl.BlockSpec((d, tf), lambda i, j: (0, j))],
        out_specs=pl.BlockSpec((tm_up, tf), lambda i, j: (i, j)),
        out_shape=jax.ShapeDtypeStruct((m, f), BF16),
        compiler_params=_params(("parallel", "parallel"), est),
    )(x, w1, w3)
    return matmul_res_norm(act, w2, res, g_next, tm=tm_down)


def _router_kernel(h_ref, g_ref, wr_ref, idx_ref, gate_ref):
    hn = _rms(h_ref[...], g_ref[...])
    logits = lax.dot_general(wr_ref[...], hn, (((1,), (1,)), ((), ())),
                             precision=lax.Precision.HIGHEST, preferred_element_type=F32)
    e_id = lax.broadcasted_iota(jnp.int32, logits.shape, 0)
    n_e = logits.shape[0]
    v1 = jnp.max(logits, axis=0, keepdims=True)
    i1 = jnp.min(jnp.where(logits == v1, e_id, n_e), axis=0, keepdims=True)
    rest = jnp.where(e_id == i1, -jnp.inf, logits)
    v2 = jnp.max(rest, axis=0, keepdims=True)
    i2 = jnp.min(jnp.where(rest == v2, e_id, n_e), axis=0, keepdims=True)
    e2 = jnp.exp(v2 - v1)
    idx_ref[0:1, :] = i1
    idx_ref[1:2, :] = i2
    gate_ref[0:1, :] = 1.0 / (1.0 + e2)
    gate_ref[1:2, :] = e2 / (1.0 + e2)


def moe_router(h, g, w_router, tm=512):
    m, d = h.shape
    n_e = w_router.shape[1]
    wr_t = w_router.T.astype(F32)
    est = 6 * _nbytes((tm, d), F32)
    return pl.pallas_call(
        _router_kernel,
        grid=(m // tm,),
        in_specs=[pl.BlockSpec((tm, d), lambda i: (i, 0)),
                  pl.BlockSpec((1, d), lambda i: (0, 0)),
                  pl.BlockSpec((n_e, d), lambda i: (0, 0))],
        out_specs=[pl.BlockSpec((MOE_TOPK, tm), lambda i: (0, i)),
                   pl.BlockSpec((MOE_TOPK, tm), lambda i: (0, i))],
        out_shape=[jax.ShapeDtypeStruct((MOE_TOPK, m), jnp.int32), jax.ShapeDtypeStruct((MOE_TOPK, m), F32)],
        compiler_params=_params(("parallel",), est),
    )(h, g.reshape(1, d).astype(F32), wr_t)


def _row_copy(src_hbm, src_row, dst_ref, dst_row, sem):
    return pltpu.make_async_copy(src_hbm.at[pl.ds(src_row, 1), :], dst_ref.at[pl.ds(dst_row, 1), :], sem)


PREFETCH_ROWS = 128


def _expert_up_kernel(te_ref, tv_ref, rows_ref, h_hbm, g_ref, w1_ref, w3_ref, o_ref, xbuf, xn_ref, sem):
    t = pl.program_id(0)
    j = pl.program_id(1)
    tm = MOE_TILE
    slot = lax.rem(t, 2)

    def request(tile, first, count):
        dst, dsem = xbuf.at[lax.rem(tile, 2)], sem.at[lax.rem(tile, 2)]

        def start(r, _):
            _row_copy(h_hbm, rows_ref[tile * tm + first + r], dst, first + r, dsem).start()
            return 0
        lax.fori_loop(0, count, start, 0, unroll=8)

    @pl.when(jnp.logical_and(t == 0, j == 0))
    def _():
        request(0, 0, tm)

    nxt = jnp.minimum(t + 1, pl.num_programs(0) - 1)
    @pl.when(jnp.logical_and(jnp.logical_and(t + 1 < pl.num_programs(0), tv_ref[nxt] > 0),
                             j < tm // PREFETCH_ROWS))
    def _():
        request(t + 1, j * PREFETCH_ROWS, PREFETCH_ROWS)

    @pl.when(jnp.logical_and(j == 0, tv_ref[t] > 0))
    def _():
        def wait(r, _):
            _row_copy(h_hbm, 0, xbuf.at[slot], r, sem.at[slot]).wait()
            return 0
        lax.fori_loop(0, tm, wait, 0, unroll=8)
        xn_ref[...] = _rms(xbuf[slot], g_ref[...]).astype(BF16)

    @pl.when(tv_ref[t] > 0)
    def _():
        o_ref[...] = _swiglu_up(xn_ref[...], w1_ref[...], w3_ref[...])

    @pl.when(tv_ref[t] == 0)
    def _():
        o_ref[...] = jnp.zeros_like(o_ref)


def _expert_down_kernel(te_ref, tv_ref, a_ref, w_ref, o_ref):
    t = pl.program_id(0)

    @pl.when(pl.program_id(1) == 0)
    def _():
        o_ref[...] = jnp.zeros_like(o_ref)

    @pl.when(tv_ref[t] > 0)
    def _():
        o_ref[...] += jnp.dot(a_ref[...], w_ref[...], preferred_element_type=F32)


def expert_ffn(h, slot_tok, g, w1, w3, w2, tile_e, tile_valid, tf=1024, nk=4):
    d = h.shape[1]
    n = slot_tok.shape[0]
    f = w1.shape[2]
    nf = f // tf
    tk = f // nk
    tm = MOE_TILE
    assert tm % PREFETCH_ROWS == 0 and nf >= tm // PREFETCH_ROWS
    assert f % tf == 0 and f % nk == 0 and tk % LANES == 0

    def hold(last):
        return lambda t, j, tv: j * tv[t] + last * (1 - tv[t])

    ju = hold(nf - 1)
    est = 2 * (_nbytes((tm, d), F32) + 2 * _nbytes((d, tf), BF16) + _nbytes((tm, tf), BF16)) \
        + _nbytes((tm, d), BF16) + 4 * _nbytes((tm, tf), F32) + _nbytes((tm, d), F32)
    act = pl.pallas_call(
        _expert_up_kernel,
        grid_spec=pltpu.PrefetchScalarGridSpec(
            num_scalar_prefetch=3,
            grid=(n // tm, nf),
            in_specs=[pl.BlockSpec(memory_space=pl.ANY),
                      pl.BlockSpec((1, d), lambda t, j, te, tv, rows: (0, 0)),
                      pl.BlockSpec((None, d, tf), lambda t, j, te, tv, rows: (te[t], 0, ju(t, j, tv))),
                      pl.BlockSpec((None, d, tf), lambda t, j, te, tv, rows: (te[t], 0, ju(t, j, tv)))],
            out_specs=pl.BlockSpec((tm, tf), lambda t, j, te, tv, rows: (t, j)),
            scratch_shapes=[pltpu.VMEM((2, tm, d), F32), pltpu.VMEM((tm, d), BF16),
                            pltpu.SemaphoreType.DMA((2,))],
        ),
        out_shape=jax.ShapeDtypeStruct((n, f), BF16),
        compiler_params=_params(("arbitrary", "arbitrary"), est),
    )(tile_e, tile_valid, slot_tok, h, g.reshape(1, d).astype(F32), w1, w3)
    jd = hold(nk - 1)
    est = 2 * (_nbytes((tm, tk), BF16) + _nbytes((tk, d), BF16) + _nbytes((tm, d), F32)) + 2 * _nbytes((tm, d), F32)
    return pl.pallas_call(
        _expert_down_kernel,
        grid_spec=pltpu.PrefetchScalarGridSpec(
            num_scalar_prefetch=2,
            grid=(n // tm, nk),
            in_specs=[pl.BlockSpec((tm, tk), lambda t, k, te, tv: (t, jd(t, k, tv))),
                      pl.BlockSpec((None, tk, d), lambda t, k, te, tv: (te[t], jd(t, k, tv), 0))],
            out_specs=pl.BlockSpec((tm, d), lambda t, k, te, tv: (t, 0)),
        ),
        out_shape=jax.ShapeDtypeStruct((n, d), F32),
        compiler_params=_params(("arbitrary", "arbitrary"), est),
    )(tile_e, tile_valid, act, w2)


def _combine_kernel(slots_ref, ys_hbm, h_ref, gate_ref, g_ref, o_ref, buf0, buf1, sem):
    base = pl.program_id(0) * GATHER_ROWS

    def start(r, _):
        flat = (base + r) * MOE_TOPK
        _row_copy(ys_hbm, slots_ref[flat], buf0, r, sem).start()
        _row_copy(ys_hbm, slots_ref[flat + 1], buf1, r, sem).start()
        return 0

    def wait(r, _):
        _row_copy(ys_hbm, 0, buf0, r, sem).wait()
        _row_copy(ys_hbm, 0, buf1, r, sem).wait()
        return 0

    lax.fori_loop(0, GATHER_ROWS, start, 0, unroll=8)
    lax.fori_loop(0, GATHER_ROWS, wait, 0, unroll=8)
    gate = gate_ref[...]
    h = h_ref[...] + gate[:, 0:1] * buf0[...] + gate[:, 1:2] * buf1[...]
    o_ref[...] = _rms(h, g_ref[...])


def moe_combine_norm(ys, slot_of_flat, h, gates, g_out):
    m, d = h.shape
    tm = GATHER_ROWS
    return pl.pallas_call(
        _combine_kernel,
        grid_spec=pltpu.PrefetchScalarGridSpec(
            num_scalar_prefetch=1,
            grid=(m // tm,),
            in_specs=[pl.BlockSpec(memory_space=pl.ANY),
                      pl.BlockSpec((tm, d), lambda i, s: (i, 0)),
                      pl.BlockSpec((tm, MOE_TOPK), lambda i, s: (i, 0)),
                      pl.BlockSpec((1, d), lambda i, s: (0, 0))],
            out_specs=pl.BlockSpec((tm, d), lambda i, s: (i, 0)),
            scratch_shapes=[pltpu.VMEM((tm, d), F32), pltpu.VMEM((tm, d), F32), pltpu.SemaphoreType.DMA(())],
        ),
        out_shape=jax.ShapeDtypeStruct((m, d), F32),
        compiler_params=_params(("arbitrary",), 8 * _nbytes((tm, d), F32)),
    )(slot_of_flat, ys, h, gates, g_out.reshape(1, d).astype(F32))


def moe_routing_metadata(top_idx, n_tokens):
    tk = n_tokens * MOE_TOPK
    flat_e = top_idx.T.reshape(-1)
    onehot = (flat_e[:, None] == jnp.arange(N_EXPERTS)[None, :]).astype(jnp.int32)
    within = jnp.take_along_axis(jnp.cumsum(onehot, axis=0), flat_e[:, None], axis=1)[:, 0] - 1
    counts = jnp.sum(onehot, axis=0)
    padded = (counts + MOE_TILE - 1) // MOE_TILE * MOE_TILE
    pad_end = jnp.cumsum(padded)
    slot_of_flat = ((pad_end - padded)[flat_e] + within).astype(jnp.int32)
    n_tiles = tk // MOE_TILE + N_EXPERTS
    slot_tok = jnp.zeros((n_tiles * MOE_TILE,), jnp.int32).at[slot_of_flat].set(
        jnp.arange(tk, dtype=jnp.int32) // MOE_TOPK)
    tile_start = jnp.arange(n_tiles, dtype=jnp.int32) * MOE_TILE
    tile_valid = (tile_start < pad_end[-1]).astype(jnp.int32)
    tile_e = jnp.minimum(jnp.sum(tile_start[:, None] >= pad_end[None, :], axis=1), N_EXPERTS - 1)
    last_e = jnp.max(jnp.where(tile_valid > 0, tile_e, 0))
    tile_e = jnp.where(tile_valid > 0, tile_e, last_e).astype(jnp.int32)
    return slot_tok, slot_of_flat, tile_e, tile_valid


def _rope_partner(w):
    half = w.shape[-1] // 2
    return jnp.concatenate([-w[..., half:], w[..., :half]], axis=-1)


def _rope_tables(positions):
    pos = positions.astype(F32).reshape(-1, 1)

    def cs(half):
        inv_freq = ROPE_THETA ** (-jnp.arange(half, dtype=F32) / half)
        ang = pos * inv_freq
        return jnp.cos(ang), jnp.sin(ang)

    c32, s32 = cs(MLA_ROPE // 2)
    zeros = jnp.zeros((pos.shape[0], LANES // 2), F32)
    mla_c = jnp.concatenate([c32, c32, zeros], axis=1)
    mla_s = jnp.concatenate([s32, s32, zeros], axis=1)
    c64, s64 = cs(HEAD_DIM // 2)
    moba_c = jnp.concatenate([c64, c64], axis=1)
    moba_s = jnp.concatenate([-s64, s64], axis=1)
    return mla_c, mla_s, moba_c, moba_s


def _layer_weights(w_in, w_uq, w_ukv):
    d = w_in.shape[0]
    o = 0
    w_cq = w_in[:, o:o + MLA_LORA]; o += MLA_LORA
    w_ckv = w_in[:, o:o + MLA_LORA]; o += MLA_LORA
    w_kr = w_in[:, o:o + MLA_ROPE]; o += MLA_ROPE
    nbc = 3 * (MOBA_HEADS + SB_HEADS) * HEAD_DIM
    w_bc = w_in[:, o:o + nbc]; o += nbc
    w_g = w_in[:, o:]
    w1 = jnp.concatenate([w_cq, w_ckv, w_kr, _rope_partner(w_kr)], axis=1).astype(BF16)
    uq = w_uq.reshape(MLA_LORA, MLA_HEADS, MLA_NOPE + MLA_ROPE)
    uq_rope = uq[..., MLA_NOPE:]
    wq = jnp.concatenate([uq[..., :MLA_NOPE], uq_rope, _rope_partner(uq_rope)], axis=-1)
    wq = wq.reshape(MLA_LORA, MLA_HEADS * 2 * LANES).astype(BF16)
    ukv = w_ukv.reshape(MLA_LORA, MLA_HEADS, MLA_NOPE + MLA_V)
    wkv = jnp.concatenate([ukv[..., :MLA_NOPE].reshape(MLA_LORA, -1), ukv[..., MLA_NOPE:].reshape(MLA_LORA, -1)],
                          axis=1).astype(BF16)
    return w1, w_bc.astype(BF16), w_g.astype(BF16), wq, wkv


def kernel(x, mem, positions, g_mix, w_in, g_cq, g_ckv, w_uq, w_ukv, w_up_a, w_up_b, w_up_c, w_o, g_x, g_mem, w_xq, w_xk, w_xv, w_xo, g_ffn, w_ff1, w_ff3, w_ff2, w_router, w_e1, w_e3, w_e2, g_final):
    batch, seq, d = x.shape
    t = batch * seq
    depth = g_mix.shape[0]
    assert depth == 2, "the final norm is fused into the expert layer, which must come last"
    mem2 = mem.reshape(-1, d)
    mla_c, mla_s, moba_c, moba_s = _rope_tables(positions)
    assert MOBA_HEADS == SB_HEADS

    h = x.reshape(t, d)
    hn = rmsnorm(h, g_mix[0], BF16)
    out = None
    for l in range(depth):
        w1, w_bc, w_g, wq, wkv = _layer_weights(w_in[l], w_uq[l], w_ukv[l])
        z1 = matmul(hn, w1, BF16, tn=w1.shape[1])
        zbc = matmul(hn, w_bc, BF16, tn=1280)
        zg = matmul(hn, w_g, BF16, tn=1024)
        q_a, k_a, v_a = mla_prep(z1, g_cq[l], g_ckv[l], wq, wkv, mla_c, mla_s)
        o_a = causal_attention(q_a, k_a, v_a, batch, MLA_HEADS, 2 * LANES, MLA_V)
        o_b = moba_attention(zbc, moba_c, moba_s, batch, 0)
        o_c = stick_breaking_attention(zbc, batch, 3)
        merged = gated_merge(o_a, o_b, o_c, zg, w_up_a[l].astype(BF16), w_up_b[l].astype(BF16),
                             w_up_c[l].astype(BF16))
        h, hn = matmul_res_norm(merged, w_o[l].astype(BF16), h, g_x[l])
        mn = rmsnorm(mem2, g_mem[l], BF16)
        q_x = matmul(hn, w_xq[l].astype(BF16), BF16)
        k_x = matmul(mn, w_xk[l].astype(BF16), BF16)
        v_x = matmul(mn, w_xv[l].astype(BF16), BF16)
        o_x = cross_attention(q_x, k_x, v_x, batch, XATTN_HEADS)
        h, hn = matmul_res_norm(o_x, w_xo[l].astype(BF16), h, g_ffn[l])
        g_next = g_mix[l + 1] if l + 1 < depth else g_final
        if l % 2 == 0:
            e = l // 2
            h, hn = dense_ffn(hn, w_ff1[e].astype(BF16), w_ff3[e].astype(BF16), w_ff2[e].astype(BF16), h, g_next)
            out = hn
        else:
            e = l // 2
            top_idx, top_gate = moe_router(h, g_ffn[l], w_router[e])
            slot_tok, slot_of_flat, tile_e, tile_valid = moe_routing_metadata(top_idx, t)
            ys = expert_ffn(h, slot_tok, g_ffn[l],w_e1[e].astype(BF16), w_e3[e].astype(BF16), w_e2[e].astype(BF16),
                            tile_e, tile_valid)
            out = moe_combine_norm(ys, slot_of_flat, h, top_gate.T, g_next)
    return out.reshape(batch, seq, d).astype(x.dtype)
```

```python
import functools

import numpy as np
import jax
import jax.numpy as jnp
from jax import lax
from jax.experimental import pallas as pl
from jax.experimental.pallas import tpu as pltpu

F32 = jnp.float32
BF16 = jnp.bfloat16

HEAD_DIM = 128
MLA_HEADS = 6
MLA_LORA = 512
MLA_NOPE = 128
MLA_ROPE = 64
MLA_V = 128
MOBA_HEADS = 5
MOBA_BLOCK = 256
MOBA_TOPK = 3
SB_HEADS = 5
N_BRANCH = 3
ROPE_THETA = 10000.0
XATTN_HEADS = 4
N_EXPERTS = 8
MOE_TOPK = 2
RMS_EPS = 1e-6

LANES = 128
V7X_VMEM_BYTES = 64 * 1024 * 1024
VMEM_CEILING = V7X_VMEM_BYTES - 8 * 1024 * 1024

MASK_VALUE = -1e30
EXP_UNDERFLOW = -104.0

MOE_TILE = 512
GATHER_ROWS = 256


def _params(sem, est_bytes):
    limit = int(min(VMEM_CEILING, max(32 * 1024 * 1024, est_bytes * 5 // 4)))
    return pltpu.CompilerParams(dimension_semantics=sem, vmem_limit_bytes=limit)


def _nbytes(shape, dtype):
    return int(np.prod(shape)) * jnp.dtype(dtype).itemsize


def _rms(x, g):
    return x * lax.rsqrt(jnp.mean(x * x, axis=-1, keepdims=True) + RMS_EPS) * g


def _rmsnorm_kernel(x_ref, g_ref, o_ref):
    o_ref[...] = _rms(x_ref[...].astype(F32), g_ref[...]).astype(o_ref.dtype)


def rmsnorm(x, g, out_dtype, tm=512):
    m, d = x.shape
    tm = min(tm, m)
    return pl.pallas_call(
        _rmsnorm_kernel,
        grid=(m // tm,),
        in_specs=[pl.BlockSpec((tm, d), lambda i: (i, 0)),
                  pl.BlockSpec((1, d), lambda i: (0, 0))],
        out_specs=pl.BlockSpec((tm, d), lambda i: (i, 0)),
        out_shape=jax.ShapeDtypeStruct((m, d), out_dtype),
        compiler_params=_params(("parallel",), 4 * _nbytes((tm, d), F32)),
    )(x, g.reshape(1, d).astype(F32))


def _mm_kernel(a_ref, w_ref, o_ref):
    o_ref[...] = jnp.dot(a_ref[...], w_ref[...], preferred_element_type=F32).astype(o_ref.dtype)


def matmul(a, w, out_dtype, tm=1024, tn=1024):
    m, k = a.shape
    n = w.shape[1]
    tm, tn = min(tm, m), min(tn, n)
    assert m % tm == 0 and n % tn == 0, (m, n, tm, tn)
    est = 2 * (_nbytes((tm, k), a.dtype) + _nbytes((k, tn), w.dtype) + _nbytes((tm, tn), out_dtype)) \
        + _nbytes((tm, tn), F32)
    return pl.pallas_call(
        _mm_kernel,
        grid=(m // tm, n // tn),
        in_specs=[pl.BlockSpec((tm, k), lambda i, j: (i, 0)),
                  pl.BlockSpec((k, tn), lambda i, j: (0, j))],
        out_specs=pl.BlockSpec((tm, tn), lambda i, j: (i, j)),
        out_shape=jax.ShapeDtypeStruct((m, n), out_dtype),
        compiler_params=_params(("parallel", "parallel"), est),
    )(a, w)


def _mm_res_norm_kernel(a_ref, w_ref, r_ref, g_ref, h_ref, hn_ref):
    h = r_ref[...] + jnp.dot(a_ref[...], w_ref[...], preferred_element_type=F32)
    h_ref[...] = h
    hn_ref[...] = _rms(h, g_ref[...]).astype(hn_ref.dtype)


def matmul_res_norm(a, w, res, g, tm=512):
    m, k = a.shape
    n = w.shape[1]
    est = 2 * (_nbytes((tm, k), a.dtype) + 2 * _nbytes((tm, n), F32) + _nbytes((tm, n), BF16)) \
        + _nbytes((k, n), w.dtype) + 2 * _nbytes((tm, n), F32)
    return pl.pallas_call(
        _mm_res_norm_kernel,
        grid=(m // tm,),
        in_specs=[pl.BlockSpec((tm, k), lambda i: (i, 0)),
                  pl.BlockSpec((k, n), lambda i: (0, 0), pipeline_mode=pl.Buffered(1)),
                  pl.BlockSpec((tm, n), lambda i: (i, 0)),
                  pl.BlockSpec((1, n), lambda i: (0, 0))],
        out_specs=[pl.BlockSpec((tm, n), lambda i: (i, 0)),
                   pl.BlockSpec((tm, n), lambda i: (i, 0))],
        out_shape=[jax.ShapeDtypeStruct((m, n), F32), jax.ShapeDtypeStruct((m, n), BF16)],
        compiler_params=_params(("parallel",), est),
    )(a, w, res, g.reshape(1, n).astype(F32))


def _half_swap(y, c, s):
    return y * c + pltpu.roll(y, LANES // 2, 1) * s


def _mla_prep_kernel(z_ref, gq_ref, gkv_ref, wq_ref, wkv_ref, c_ref, s_ref, q_ref, k_ref, v_ref, *, scale):
    z = z_ref[...].astype(F32)
    c = c_ref[...]
    s = s_ref[...]
    nq = _rms(z[:, :MLA_LORA], gq_ref[...]).astype(BF16)
    nkv = _rms(z[:, MLA_LORA:2 * MLA_LORA], gkv_ref[...]).astype(BF16)
    q = jnp.dot(nq, wq_ref[...], preferred_element_type=F32)
    kv = jnp.dot(nkv, wkv_ref[...], preferred_element_type=F32)
    k_pe = _half_swap(z[:, 2 * MLA_LORA:], c, s).astype(BF16)
    for h in range(MLA_HEADS):
        lo = 2 * LANES * h
        q_ref[:, lo:lo + LANES] = (q[:, lo:lo + LANES] * scale).astype(BF16)
        q_ref[:, lo + LANES:lo + 2 * LANES] = (_half_swap(q[:, lo + LANES:lo + 2 * LANES], c, s) * scale).astype(BF16)
        k_ref[:, lo:lo + LANES] = kv[:, LANES * h:LANES * (h + 1)].astype(BF16)
        k_ref[:, lo + LANES:lo + 2 * LANES] = k_pe
    v_ref[...] = kv[:, MLA_HEADS * MLA_NOPE:].astype(BF16)


def mla_prep(z1, g_cq, g_ckv, wq, wkv, cos_t, sin_t, tm=512):
    m, zc = z1.shape
    nq, nkv = wq.shape[1], wkv.shape[1]
    scale = float((MLA_NOPE + MLA_ROPE) ** -0.5)
    row = lambda c: pl.BlockSpec((tm, c), lambda i: (i, 0))
    full = lambda a: pl.BlockSpec(a.shape, lambda i: (0, 0))
    g_cq = g_cq.reshape(1, -1).astype(F32)
    g_ckv = g_ckv.reshape(1, -1).astype(F32)
    est = 4 * _nbytes((tm, nq), F32) + 4 * (_nbytes(wq.shape, BF16) + _nbytes(wkv.shape, BF16))
    return pl.pallas_call(
        functools.partial(_mla_prep_kernel, scale=scale),
        grid=(m // tm,),
        in_specs=[row(zc), full(g_cq), full(g_ckv), full(wq), full(wkv), row(LANES), row(LANES)],
        out_specs=[row(nq), row(nq), row(MLA_HEADS * MLA_V)],
        out_shape=[jax.ShapeDtypeStruct((m, nq), BF16), jax.ShapeDtypeStruct((m, nq), BF16),
                   jax.ShapeDtypeStruct((m, MLA_HEADS * MLA_V), BF16)],
        compiler_params=_params(("parallel",), est),
    )(z1, g_cq, g_ckv, wq, wkv, cos_t, sin_t)


def _nt_dot(a, b):
    return lax.dot_general(a, b, (((1,), (1,)), ((), ())), preferred_element_type=F32)


def _softmax_tiles_keymajor(scores, load_vt, carry):
    p, stats = [], []
    for s, (m, l, _) in zip(scores, carry):
        m_new = jnp.maximum(m, jnp.max(s, axis=0, keepdims=True))
        alpha = jnp.exp(m - m_new)
        ph = jnp.exp(s - m_new)
        stats.append((m_new, alpha, alpha * l + jnp.sum(ph, axis=0, keepdims=True)))
        p.append(ph.astype(BF16))
    out = []
    for h, ((m_new, alpha, l_new), (_, _, acc)) in enumerate(zip(stats, carry)):
        acc = alpha * acc
        for n, vt in enumerate(load_vt(h)):
            acc = acc + jnp.dot(vt, p[h][n * vt.shape[1]:(n + 1) * vt.shape[1], :], preferred_element_type=F32)
        out.append((m_new, l_new, acc))
    return out


def _softmax_init_keymajor(queries, dv):
    return (jnp.full((1, queries), MASK_VALUE, F32), jnp.zeros((1, queries), F32), jnp.zeros((dv, queries), F32))


def _transposed_bf16(x):
    return x.astype(F32).T.astype(BF16)


def _causal_attn_kernel(q_ref, k_ref, v_ref, o_ref, vt_scr, *, tq, tk, dk, dv, hpg):
    i = pl.program_id(2)
    n_full = (i * tq) // tk
    heads = range(hpg)

    @pl.when(i == 0)
    def _():
        def prep(j, _):
            rows = pl.ds(pl.multiple_of(j * tk, tk), tk)
            for h in heads:
                vt_scr[j, h * dv:(h + 1) * dv, :] = _transposed_bf16(v_ref[rows, h * dv:(h + 1) * dv])
            return 0
        lax.fori_loop(0, v_ref.shape[0] // tk, prep, 0)

    q = [q_ref[:, h * dk:(h + 1) * dk] for h in heads]

    def tile(j, carry, bias):
        rows = pl.ds(pl.multiple_of(j * tk, tk), tk)
        s = [_nt_dot(k_ref[rows, h * dk:(h + 1) * dk], q[h]) for h in heads]
        if bias is not None:
            s = [sh + bias for sh in s]
        return _softmax_tiles_keymajor(s, lambda h: [vt_scr[j, h * dv:(h + 1) * dv, :]], carry)

    carry = lax.fori_loop(0, n_full, lambda j, c: tile(j, c, None),
                          [_softmax_init_keymajor(tq, dv) for _ in heads])
    kpos = n_full * tk + lax.broadcasted_iota(jnp.int32, (tk, tq), 0)
    qpos = i * tq + lax.broadcasted_iota(jnp.int32, (tk, tq), 1)
    carry = tile(n_full, carry, jnp.where(kpos <= qpos, 0.0, MASK_VALUE))
    for h in heads:
        _, l, acc = carry[h]
        o_ref[:, h * dv:(h + 1) * dv] = (acc / l).T.astype(o_ref.dtype)


def causal_attention(q, k, v, batch, heads, dk, dv, tq=256, tk=512, hpg=3):
    t = q.shape[0]
    s = t // batch
    nq = s // tq
    assert tk % tq == 0 and s % tk == 0 and heads % hpg == 0
    est = 4 * hpg * (_nbytes((s, dk), BF16) + _nbytes((s, dv), BF16)) + 8 * hpg * _nbytes((tq, tk), F32)
    return pl.pallas_call(
        functools.partial(_causal_attn_kernel, tq=tq, tk=tk, dk=dk, dv=dv, hpg=hpg),
        grid=(batch, heads // hpg, nq),
        in_specs=[pl.BlockSpec((tq, hpg * dk), lambda b, g, i: (b * nq + i, g)),
                  pl.BlockSpec((s, hpg * dk), lambda b, g, i: (b, g)),
                  pl.BlockSpec((s, hpg * dv), lambda b, g, i: (b, g))],
        out_specs=pl.BlockSpec((tq, hpg * dv), lambda b, g, i: (b * nq + i, g)),
        out_shape=jax.ShapeDtypeStruct((t, heads * dv), BF16),
        scratch_shapes=[pltpu.VMEM((s // tk, hpg * dv, tk), BF16)],
        compiler_params=_params(("parallel", "parallel", "arbitrary"), est),
    )(q, k, v)


def _moba_select(gate_t, i, nb):
    nq = gate_t.shape[1]
    row = lax.broadcasted_iota(jnp.int32, (nb, nq), 0)
    rank = jnp.zeros((nb, nq), jnp.int32)
    for jj in range(nb):
        gj = gate_t[jj:jj + 1, :]
        ahead = jnp.logical_or(gj > gate_t, jnp.logical_and(gj == gate_t, jj < row))
        rank = rank + jnp.where(jnp.logical_and(ahead, jj < i), 1, 0)
    return jnp.where(jnp.logical_and(row < i, rank < MOBA_TOPK), 1.0, 0.0)


def _moba_kernel(q_ref, k_ref, v_ref, cq_ref, sq_ref, ck_ref, sk_ref, o_ref, kr_scr, vt_scr, km_scr, sel_scr,
                 *, nb, nh, scale):
    i = pl.program_id(1)
    blk = MOBA_BLOCK
    d = HEAD_DIM
    heads = range(nh)

    @pl.when(i == 0)
    def _():
        def prep(j, _):
            rows = pl.ds(pl.multiple_of(j * blk, blk), blk)
            c, s = ck_ref[rows, :], sk_ref[rows, :]
            for h in heads:
                kj = _half_swap(k_ref[rows, h * d:(h + 1) * d].astype(F32), c, s)
                km_scr[pl.ds(h * nb + j, 1), :] = jnp.mean(kj, axis=0, keepdims=True)
                kr_scr[rows, h * d:(h + 1) * d] = kj.astype(BF16)
                vt_scr[j, h * d:(h + 1) * d, :] = _transposed_bf16(v_ref[rows, h * d:(h + 1) * d])
            return 0
        lax.fori_loop(0, nb, prep, 0)

    cq, sq = cq_ref[...], sq_ref[...]
    qb = []
    for h in heads:
        q = _half_swap(q_ref[:, h * d:(h + 1) * d].astype(F32), cq, sq)
        gate_t = lax.dot_general(km_scr[h * nb:(h + 1) * nb, :], q, (((1,), (1,)), ((), ())),
                                 precision=lax.Precision.HIGHEST, preferred_element_type=F32)
        sel_scr[h * nb:(h + 1) * nb, :] = _moba_select(gate_t, i, nb)
        qb.append((q * scale).astype(BF16))

    def tile(j, nblk, carry, bias_fn):
        rows = pl.ds(pl.multiple_of(j * blk, blk), nblk * blk)
        s = [_nt_dot(kr_scr[rows, h * d:(h + 1) * d], qb[h]) + bias_fn(h) for h in heads]
        return _softmax_tiles_keymajor(
            s, lambda h: [vt_scr[j + n, h * d:(h + 1) * d, :] for n in range(nblk)], carry)

    key_id = lax.broadcasted_iota(jnp.int32, (blk, blk), 0)
    qry_id = lax.broadcasted_iota(jnp.int32, (blk, blk), 1)
    causal = jnp.where(key_id <= qry_id, 0.0, MASK_VALUE)
    carry = tile(i, 1, [_softmax_init_keymajor(blk, d) for _ in heads], lambda h: causal)

    def body(p, carry):
        def bias(h):
            rows = [jnp.broadcast_to((1.0 - sel_scr[pl.ds(h * nb + 2 * p + n, 1), :]) * MASK_VALUE, (blk, blk))
                    for n in range(2)]
            return jnp.concatenate(rows, axis=0)
        return tile(2 * p, 2, carry, bias)

    carry = lax.fori_loop(0, (i + 1) // 2, body, carry)
    for h in heads:
        _, l, acc = carry[h]
        o_ref[:, h * d:(h + 1) * d] = (acc / l).T.astype(o_ref.dtype)


def moba_attention(zbc, cos_t, sin_t, batch, group0):
    t = zbc.shape[0]
    s = t // batch
    blk = MOBA_BLOCK
    nb = s // blk
    nh = MOBA_HEADS
    w = nh * HEAD_DIM
    est = 6 * _nbytes((s, w), BF16) + 4 * _nbytes((s, HEAD_DIM), F32) + 12 * nh * _nbytes((blk, blk), F32)
    return pl.pallas_call(
        functools.partial(_moba_kernel, nb=nb, nh=nh, scale=float(HEAD_DIM ** -0.5)),
        grid=(batch, nb),
        in_specs=[pl.BlockSpec((blk, w), lambda b, i: (b * nb + i, group0)),
                  pl.BlockSpec((s, w), lambda b, i: (b, group0 + 1)),
                  pl.BlockSpec((s, w), lambda b, i: (b, group0 + 2)),
                  pl.BlockSpec((blk, HEAD_DIM), lambda b, i: (b * nb + i, 0)),
                  pl.BlockSpec((blk, HEAD_DIM), lambda b, i: (b * nb + i, 0)),
                  pl.BlockSpec((s, HEAD_DIM), lambda b, i: (b, 0)),
                  pl.BlockSpec((s, HEAD_DIM), lambda b, i: (b, 0))],
        out_specs=pl.BlockSpec((blk, w), lambda b, i: (b * nb + i, 0)),
        out_shape=jax.ShapeDtypeStruct((t, w), BF16),
        scratch_shapes=[pltpu.VMEM((s, w), BF16), pltpu.VMEM((nb, w, blk), BF16),
                        pltpu.VMEM((nh * nb, HEAD_DIM), F32), pltpu.VMEM((nh * nb, blk), F32)],
        compiler_params=_params(("parallel", "arbitrary"), est),
    )(zbc, zbc, zbc, cos_t, sin_t, cos_t, sin_t)


def _sb_kernel(q_ref, k_ref, v_ref, u_ref, o_ref, *, tq, nh, scale):
    i = pl.program_id(1)
    d = HEAD_DIM
    heads = range(nh)
    u = u_ref[...]
    qb = [(q_ref[:, h * d:(h + 1) * d].astype(F32) * scale).astype(BF16) for h in heads]

    def tile(j, carry, strict):
        off = pl.multiple_of(j * tq, tq)
        z = [_nt_dot(qb[h], k_ref[pl.ds(off, tq), h * d:(h + 1) * d]) for h in heads]
        hi, lo = [], []
        for h in heads:
            lsm = -(jnp.maximum(z[h], 0.0) + jnp.log(1.0 + jnp.exp(-jnp.abs(z[h]))))
            if strict is not None:
                lsm = jnp.where(strict, lsm, 0.0)
            hi.append(lsm.astype(BF16))
            lo.append((lsm - hi[h].astype(F32)).astype(BF16))
        incl = [jnp.dot(hi[h], u, preferred_element_type=F32) + jnp.dot(lo[h], u, preferred_element_type=F32)
                for h in heads]
        a = []
        for h in heads:
            ah = jnp.exp(jnp.minimum(z[h] + incl[h], 0.0) + carry[h][0])
            if strict is not None:
                ah = jnp.where(strict, ah, 0.0)
            a.append(ah.astype(BF16))
        return [(carry[h][0] + incl[h][:, 0:1],
                 carry[h][1] + jnp.dot(a[h], v_ref[pl.ds(off, tq), h * d:(h + 1) * d], preferred_element_type=F32))
                for h in heads]

    r_id = lax.broadcasted_iota(jnp.int32, (tq, tq), 0)
    c_id = lax.broadcasted_iota(jnp.int32, (tq, tq), 1)
    init = [(jnp.zeros((tq, 1), F32), jnp.zeros((tq, d), F32)) for _ in heads]
    carry = tile(i, init, c_id < r_id)

    def live(carry):
        worst = carry[0][0]
        for h in heads[1:]:
            worst = jnp.maximum(worst, carry[h][0])
        return (jnp.max(worst) > EXP_UNDERFLOW).astype(jnp.int32)

    def body(state):
        n, _, carry = state
        carry = tile(i - 1 - n, carry, None)
        return n + 1, live(carry), carry

    _, _, carry = lax.while_loop(lambda st: jnp.logical_and(st[0] < i, st[1] > 0), body,
                                 (jnp.int32(0), live(carry), carry))
    for h in heads:
        o_ref[:, h * d:(h + 1) * d] = carry[h][1].astype(o_ref.dtype)


def stick_breaking_attention(zbc, batch, group0, tq=256):
    t = zbc.shape[0]
    s = t // batch
    nq = s // tq
    nh = SB_HEADS
    w = nh * HEAD_DIM
    u = (jnp.arange(tq)[:, None] >= jnp.arange(tq)[None, :]).astype(BF16)
    est = 8 * _nbytes((s, w), BF16) + 16 * nh * _nbytes((tq, tq), F32)
    return pl.pallas_call(
        functools.partial(_sb_kernel, tq=tq, nh=nh, scale=float(HEAD_DIM ** -0.5)),
        grid=(batch, nq),
        in_specs=[pl.BlockSpec((tq, w), lambda b, i: (b * nq + i, group0)),
                  pl.BlockSpec((s, w), lambda b, i: (b, group0 + 1)),
                  pl.BlockSpec((s, w), lambda b, i: (b, group0 + 2)),
                  pl.BlockSpec((tq, tq), lambda b, i: (0, 0))],
        out_specs=pl.BlockSpec((tq, w), lambda b, i: (b * nq + i, 0)),
        out_shape=jax.ShapeDtypeStruct((t, w), BF16),
        compiler_params=_params(("parallel", "arbitrary"), est),
    )(zbc, zbc, zbc, u)


def _merge_kernel(oa_ref, ob_ref, oc_ref, ga_ref, gb_ref, gc_ref, wa_ref, wb_ref, wc_ref, o_ref):
    def branch(o, g, w):
        return jax.nn.sigmoid(g[...].astype(F32)) * jnp.dot(o[...], w[...], preferred_element_type=F32)
    o_ref[...] = (branch(oa_ref, ga_ref, wa_ref) + branch(ob_ref, gb_ref, wb_ref)
                  + branch(oc_ref, gc_ref, wc_ref)).astype(o_ref.dtype)


def gated_merge(o_a, o_b, o_c, zg, wa, wb, wc, tm=512):
    m = o_a.shape[0]
    d = wa.shape[1]
    row = lambda a: pl.BlockSpec((tm, a.shape[1]), lambda i: (i, 0))
    full = lambda a: pl.BlockSpec(a.shape, lambda i: (0, 0))
    gate = lambda n: pl.BlockSpec((tm, d), lambda i: (i, n))
    est = 4 * _nbytes((d, d), BF16) + 12 * _nbytes((tm, d), F32)
    return pl.pallas_call(
        _merge_kernel,
        grid=(m // tm,),
        in_specs=[row(o_a), row(o_b), row(o_c), gate(0), gate(1), gate(2), full(wa), full(wb), full(wc)],
        out_specs=pl.BlockSpec((tm, d), lambda i: (i, 0)),
        out_shape=jax.ShapeDtypeStruct((m, d), BF16),
        compiler_params=_params(("parallel",), est),
    )(o_a, o_b, o_c, zg, zg, zg, wa, wb, wc)


def _xattn_kernel(q_ref, k_ref, v_ref, o_ref, *, heads, scale):
    hd = q_ref.shape[1] // heads
    for h in range(heads):
        cols = slice(h * hd, (h + 1) * hd)
        s = _nt_dot(q_ref[:, cols], k_ref[:, cols]) * scale
        p = jnp.exp(s - jnp.max(s, axis=1, keepdims=True))
        o = jnp.dot(p.astype(BF16), v_ref[:, cols], preferred_element_type=F32)
        o_ref[:, cols] = (o / jnp.sum(p, axis=1, keepdims=True)).astype(o_ref.dtype)


def cross_attention(q, k, v, batch, heads, tq=512):
    t, d = q.shape
    s = t // batch
    mlen = k.shape[0] // batch
    nq = s // tq
    est = 8 * _nbytes((tq, d), BF16) + 8 * _nbytes((mlen, d), BF16) + 8 * _nbytes((tq, mlen), F32)
    return pl.pallas_call(
        functools.partial(_xattn_kernel, heads=heads, scale=float((d // heads) ** -0.5)),
        grid=(batch, nq),
        in_specs=[pl.BlockSpec((tq, d), lambda b, i: (b * nq + i, 0)),
                  pl.BlockSpec((mlen, d), lambda b, i: (b, 0)),
                  pl.BlockSpec((mlen, d), lambda b, i: (b, 0))],
        out_specs=pl.BlockSpec((tq, d), lambda b, i: (b * nq + i, 0)),
        out_shape=jax.ShapeDtypeStruct((t, d), BF16),
        compiler_params=_params(("parallel", "parallel"), est),
    )(q, k, v)


SWIGLU_CHUNK = 512


def _swiglu_up(x_ref, w1_ref, w3_ref, o_ref):
    x = x_ref[...]
    for c in range(0, o_ref.shape[1], SWIGLU_CHUNK):
        cols = slice(c, c + SWIGLU_CHUNK)
        a = jnp.dot(x, w1_ref[:, cols], preferred_element_type=F32)
        b = jnp.dot(x, w3_ref[:, cols], preferred_element_type=F32)
        o_ref[:, cols] = (a * jax.nn.sigmoid(a) * b).astype(BF16)


def _ffn_up_kernel(x_ref, w1_ref, w3_ref, o_ref):
    _swiglu_up(x_ref, w1_ref, w3_ref, o_ref)


def dense_ffn(x, w1, w3, w2, res, g_next, tm_up=1024, tf=512, tm_down=256):
    m, d = x.shape
    f = w1.shape[1]
    assert f % tf == 0
    est = 2 * (_nbytes((tm_up, d), BF16) + 2 * _nbytes((d, tf), BF16) + _nbytes((tm_up, tf), BF16)) \
        + 4 * _nbytes((tm_up, tf), F32)
    act = pl.pallas_call(
        _ffn_up_kernel,
        grid=(m // tm_up, f // tf),
        in_specs=[pl.BlockSpec((tm_up, d), lambda i, j: (i, 0)),
                  pl.BlockSpec((d, tf), lambda i, j: (0, j)),
                  pl.BlockSpec((d, tf), lambda i, j: (0, j))],
        out_specs=pl.BlockSpec((tm_up, tf), lambda i, j: (i, j)),
        out_shape=jax.ShapeDtypeStruct((m, f), BF16),
        compiler_params=_params(("parallel", "parallel"), est),
    )(x, w1, w3)
    return matmul_res_norm(act, w2, res, g_next, tm=tm_down)


def _router_kernel(h_ref, g_ref, wr_ref, idx_ref, gate_ref):
    hn = _rms(h_ref[...], g_ref[...])
    logits = lax.dot_general(wr_ref[...], hn, (((1,), (1,)), ((), ())),
                             precision=lax.Precision.HIGHEST, preferred_element_type=F32)
    e_id = lax.broadcasted_iota(jnp.int32, logits.shape, 0)
    n_e = logits.shape[0]
    v1 = jnp.max(logits, axis=0, keepdims=True)
    i1 = jnp.min(jnp.where(logits == v1, e_id, n_e), axis=0, keepdims=True)
    rest = jnp.where(e_id == i1, -jnp.inf, logits)
    v2 = jnp.max(rest, axis=0, keepdims=True)
    i2 = jnp.min(jnp.where(rest == v2, e_id, n_e), axis=0, keepdims=True)
    e2 = jnp.exp(v2 - v1)
    idx_ref[0:1, :] = i1
    idx_ref[1:2, :] = i2
    gate_ref[0:1, :] = 1.0 / (1.0 + e2)
    gate_ref[1:2, :] = e2 / (1.0 + e2)


def moe_router(h, g, w_router, tm=512):
    m, d = h.shape
    n_e = w_router.shape[1]
    wr_t = w_router.T.astype(F32)
    est = 6 * _nbytes((tm, d), F32)
    return pl.pallas_call(
        _router_kernel,
        grid=(m // tm,),
        in_specs=[pl.BlockSpec((tm, d), lambda i: (i, 0)),
                  pl.BlockSpec((1, d), lambda i: (0, 0)),
                  pl.BlockSpec((n_e, d), lambda i: (0, 0))],
        out_specs=[pl.BlockSpec((MOE_TOPK, tm), lambda i: (0, i)),
                   pl.BlockSpec((MOE_TOPK, tm), lambda i: (0, i))],
        out_shape=[jax.ShapeDtypeStruct((MOE_TOPK, m), jnp.int32), jax.ShapeDtypeStruct((MOE_TOPK, m), F32)],
        compiler_params=_params(("parallel",), est),
    )(h, g.reshape(1, d).astype(F32), wr_t)


def _row_copy(src_hbm, src_row, dst_ref, dst_row, sem):
    return pltpu.make_async_copy(src_hbm.at[pl.ds(src_row, 1), :], dst_ref.at[pl.ds(dst_row, 1), :], sem)


PREFETCH_ROWS = 128


def _expert_up_kernel(te_ref, tv_ref, rows_ref, h_hbm, g_ref, w1_ref, w3_ref, o_ref, xbuf, xn_ref, sem):
    t = pl.program_id(0)
    j = pl.program_id(1)
    tm = MOE_TILE
    slot = lax.rem(t, 2)

    def request(tile, first, count):
        dst, dsem = xbuf.at[lax.rem(tile, 2)], sem.at[lax.rem(tile, 2)]

        def start(r, _):
            _row_copy(h_hbm, rows_ref[tile * tm + first + r], dst, first + r, dsem).start()
            return 0
        lax.fori_loop(0, count, start, 0, unroll=8)

    @pl.when(jnp.logical_and(t == 0, j == 0))
    def _():
        request(0, 0, tm)

    nxt = jnp.minimum(t + 1, pl.num_programs(0) - 1)
    @pl.when(jnp.logical_and(jnp.logical_and(t + 1 < pl.num_programs(0), tv_ref[nxt] > 0),
                             j < tm // PREFETCH_ROWS))
    def _():
        request(t + 1, j * PREFETCH_ROWS, PREFETCH_ROWS)

    @pl.when(jnp.logical_and(j == 0, tv_ref[t] > 0))
    def _():
        def wait(r, _):
            _row_copy(h_hbm, 0, xbuf.at[slot], r, sem.at[slot]).wait()
            return 0
        lax.fori_loop(0, tm, wait, 0, unroll=8)
        xn_ref[...] = _rms(xbuf[slot], g_ref[...]).astype(BF16)

    @pl.when(tv_ref[t] > 0)
    def _():
        _swiglu_up(xn_ref, w1_ref, w3_ref, o_ref)

    @pl.when(tv_ref[t] == 0)
    def _():
        o_ref[...] = jnp.zeros_like(o_ref)


def _expert_down_kernel(te_ref, tv_ref, a_ref, w_ref, o_ref):
    t = pl.program_id(0)

    @pl.when(pl.program_id(1) == 0)
    def _():
        o_ref[...] = jnp.zeros_like(o_ref)

    @pl.when(tv_ref[t] > 0)
    def _():
        o_ref[...] += jnp.dot(a_ref[...], w_ref[...], preferred_element_type=F32)


def expert_ffn(h, slot_tok, g, w1, w3, w2, tile_e, tile_valid, tf=1024, nk=4):
    d = h.shape[1]
    n = slot_tok.shape[0]
    f = w1.shape[2]
    nf = f // tf
    tk = f // nk
    tm = MOE_TILE
    assert tm % PREFETCH_ROWS == 0 and nf >= tm // PREFETCH_ROWS
    assert f % tf == 0 and f % nk == 0 and tk % LANES == 0

    def hold(last):
        return lambda t, j, tv: j * tv[t] + last * (1 - tv[t])

    ju = hold(nf - 1)
    est = 2 * (_nbytes((tm, d), F32) + 2 * _nbytes((d, tf), BF16) + _nbytes((tm, tf), BF16)) \
        + _nbytes((tm, d), BF16) + 4 * _nbytes((tm, tf), F32) + _nbytes((tm, d), F32)
    act = pl.pallas_call(
        _expert_up_kernel,
        grid_spec=pltpu.PrefetchScalarGridSpec(
            num_scalar_prefetch=3,
            grid=(n // tm, nf),
            in_specs=[pl.BlockSpec(memory_space=pl.ANY),
                      pl.BlockSpec((1, d), lambda t, j, te, tv, rows: (0, 0)),
                      pl.BlockSpec((None, d, tf), lambda t, j, te, tv, rows: (te[t], 0, ju(t, j, tv))),
                      pl.BlockSpec((None, d, tf), lambda t, j, te, tv, rows: (te[t], 0, ju(t, j, tv)))],
            out_specs=pl.BlockSpec((tm, tf), lambda t, j, te, tv, rows: (t, j)),
            scratch_shapes=[pltpu.VMEM((2, tm, d), F32), pltpu.VMEM((tm, d), BF16),
                            pltpu.SemaphoreType.DMA((2,))],
        ),
        out_shape=jax.ShapeDtypeStruct((n, f), BF16),
        compiler_params=_params(("arbitrary", "arbitrary"), est),
    )(tile_e, tile_valid, slot_tok, h, g.reshape(1, d).astype(F32), w1, w3)
    jd = hold(nk - 1)
    est = 2 * (_nbytes((tm, tk), BF16) + _nbytes((tk, d), BF16) + _nbytes((tm, d), F32)) + 2 * _nbytes((tm, d), F32)
    return pl.pallas_call(
        _expert_down_kernel,
        grid_spec=pltpu.PrefetchScalarGridSpec(
            num_scalar_prefetch=2,
            grid=(n // tm, nk),
            in_specs=[pl.BlockSpec((tm, tk), lambda t, k, te, tv: (t, jd(t, k, tv))),
                      pl.BlockSpec((None, tk, d), lambda t, k, te, tv: (te[t], jd(t, k, tv), 0))],
            out_specs=pl.BlockSpec((tm, d), lambda t, k, te, tv: (t, 0)),
        ),
        out_shape=jax.ShapeDtypeStruct((n, d), F32),
        compiler_params=_params(("arbitrary", "arbitrary"), est),
    )(tile_e, tile_valid, act, w2)


def _combine_kernel(slots_ref, ys_hbm, h_ref, gate_ref, g_ref, o_ref, buf0, buf1, sem):
    base = pl.program_id(0) * GATHER_ROWS

    def start(r, _):
        flat = (base + r) * MOE_TOPK
        _row_copy(ys_hbm, slots_ref[flat], buf0, r, sem).start()
        _row_copy(ys_hbm, slots_ref[flat + 1], buf1, r, sem).start()
        return 0

    def wait(r, _):
        _row_copy(ys_hbm, 0, buf0, r, sem).wait()
        _row_copy(ys_hbm, 0, buf1, r, sem).wait()
        return 0

    lax.fori_loop(0, GATHER_ROWS, start, 0, unroll=8)
    lax.fori_loop(0, GATHER_ROWS, wait, 0, unroll=8)
    gate = gate_ref[...]
    h = h_ref[...] + gate[:, 0:1] * buf0[...] + gate[:, 1:2] * buf1[...]
    o_ref[...] = _rms(h, g_ref[...])


def moe_combine_norm(ys, slot_of_flat, h, gates, g_out):
    m, d = h.shape
    tm = GATHER_ROWS
    return pl.pallas_call(
        _combine_kernel,
        grid_spec=pltpu.PrefetchScalarGridSpec(
            num_scalar_prefetch=1,
            grid=(m // tm,),
            in_specs=[pl.BlockSpec(memory_space=pl.ANY),
                      pl.BlockSpec((tm, d), lambda i, s: (i, 0)),
                      pl.BlockSpec((tm, MOE_TOPK), lambda i, s: (i, 0)),
                      pl.BlockSpec((1, d), lambda i, s: (0, 0))],
            out_specs=pl.BlockSpec((tm, d), lambda i, s: (i, 0)),
            scratch_shapes=[pltpu.VMEM((tm, d), F32), pltpu.VMEM((tm, d), F32), pltpu.SemaphoreType.DMA(())],
        ),
        out_shape=jax.ShapeDtypeStruct((m, d), F32),
        compiler_params=_params(("arbitrary",), 8 * _nbytes((tm, d), F32)),
    )(slot_of_flat, ys, h, gates, g_out.reshape(1, d).astype(F32))


def moe_routing_metadata(top_idx, n_tokens):
    tk = n_tokens * MOE_TOPK
    flat_e = top_idx.T.reshape(-1)
    onehot = (flat_e[:, None] == jnp.arange(N_EXPERTS)[None, :]).astype(jnp.int32)
    within = jnp.take_along_axis(jnp.cumsum(onehot, axis=0), flat_e[:, None], axis=1)[:, 0] - 1
    counts = jnp.sum(onehot, axis=0)
    padded = (counts + MOE_TILE - 1) // MOE_TILE * MOE_TILE
    pad_end = jnp.cumsum(padded)
    slot_of_flat = ((pad_end - padded)[flat_e] + within).astype(jnp.int32)
    n_tiles = tk // MOE_TILE + N_EXPERTS
    slot_tok = jnp.zeros((n_tiles * MOE_TILE,), jnp.int32).at[slot_of_flat].set(
        jnp.arange(tk, dtype=jnp.int32) // MOE_TOPK)
    tile_start = jnp.arange(n_tiles, dtype=jnp.int32) * MOE_TILE
    tile_valid = (tile_start < pad_end[-1]).astype(jnp.int32)
    tile_e = jnp.minimum(jnp.sum(tile_start[:, None] >= pad_end[None, :], axis=1), N_EXPERTS - 1)
    last_e = jnp.max(jnp.where(tile_valid > 0, tile_e, 0))
    tile_e = jnp.where(tile_valid > 0, tile_e, last_e).astype(jnp.int32)
    return slot_tok, slot_of_flat, tile_e, tile_valid


def _rope_partner(w):
    half = w.shape[-1] // 2
    return jnp.concatenate([-w[..., half:], w[..., :half]], axis=-1)


def _rope_tables(positions):
    pos = positions.astype(F32).reshape(-1, 1)

    def cs(half):
        inv_freq = ROPE_THETA ** (-jnp.arange(half, dtype=F32) / half)
        ang = pos * inv_freq
        return jnp.cos(ang), jnp.sin(ang)

    c32, s32 = cs(MLA_ROPE // 2)
    zeros = jnp.zeros((pos.shape[0], LANES // 2), F32)
    mla_c = jnp.concatenate([c32, c32, zeros], axis=1)
    mla_s = jnp.concatenate([s32, s32, zeros], axis=1)
    c64, s64 = cs(HEAD_DIM // 2)
    moba_c = jnp.concatenate([c64, c64], axis=1)
    moba_s = jnp.concatenate([-s64, s64], axis=1)
    return mla_c, mla_s, moba_c, moba_s


def _layer_weights(w_in, w_uq, w_ukv):
    d = w_in.shape[0]
    o = 0
    w_cq = w_in[:, o:o + MLA_LORA]; o += MLA_LORA
    w_ckv = w_in[:, o:o + MLA_LORA]; o += MLA_LORA
    w_kr = w_in[:, o:o + MLA_ROPE]; o += MLA_ROPE
    nbc = 3 * (MOBA_HEADS + SB_HEADS) * HEAD_DIM
    w_bc = w_in[:, o:o + nbc]; o += nbc
    w_g = w_in[:, o:]
    w1 = jnp.concatenate([w_cq, w_ckv, w_kr, _rope_partner(w_kr)], axis=1).astype(BF16)
    uq = w_uq.reshape(MLA_LORA, MLA_HEADS, MLA_NOPE + MLA_ROPE)
    uq_rope = uq[..., MLA_NOPE:]
    wq = jnp.concatenate([uq[..., :MLA_NOPE], uq_rope, _rope_partner(uq_rope)], axis=-1)
    wq = wq.reshape(MLA_LORA, MLA_HEADS * 2 * LANES).astype(BF16)
    ukv = w_ukv.reshape(MLA_LORA, MLA_HEADS, MLA_NOPE + MLA_V)
    wkv = jnp.concatenate([ukv[..., :MLA_NOPE].reshape(MLA_LORA, -1), ukv[..., MLA_NOPE:].reshape(MLA_LORA, -1)],
                          axis=1).astype(BF16)
    return w1, w_bc.astype(BF16), w_g.astype(BF16), wq, wkv


def kernel(x, mem, positions, g_mix, w_in, g_cq, g_ckv, w_uq, w_ukv, w_up_a, w_up_b, w_up_c, w_o, g_x, g_mem, w_xq, w_xk, w_xv, w_xo, g_ffn, w_ff1, w_ff3, w_ff2, w_router, w_e1, w_e3, w_e2, g_final):
    batch, seq, d = x.shape
    t = batch * seq
    depth = g_mix.shape[0]
    assert depth == 2, "the final norm is fused into the expert layer, which must come last"
    mem2 = mem.reshape(-1, d)
    mla_c, mla_s, moba_c, moba_s = _rope_tables(positions)
    assert MOBA_HEADS == SB_HEADS

    h = x.reshape(t, d)
    hn = rmsnorm(h, g_mix[0], BF16)
    out = None
    for l in range(depth):
        w1, w_bc, w_g, wq, wkv = _layer_weights(w_in[l], w_uq[l], w_ukv[l])
        z1 = matmul(hn, w1, BF16, tn=w1.shape[1])
        zbc = matmul(hn, w_bc, BF16, tn=1280)
        zg = matmul(hn, w_g, BF16, tn=1024)
        q_a, k_a, v_a = mla_prep(z1, g_cq[l], g_ckv[l], wq, wkv, mla_c, mla_s)
        o_a = causal_attention(q_a, k_a, v_a, batch, MLA_HEADS, 2 * LANES, MLA_V)
        o_b = moba_attention(zbc, moba_c, moba_s, batch, 0)
        o_c = stick_breaking_attention(zbc, batch, 3)
        merged = gated_merge(o_a, o_b, o_c, zg, w_up_a[l].astype(BF16), w_up_b[l].astype(BF16),
                             w_up_c[l].astype(BF16))
        h, hn = matmul_res_norm(merged, w_o[l].astype(BF16), h, g_x[l])
        mn = rmsnorm(mem2, g_mem[l], BF16)
        q_x = matmul(hn, w_xq[l].astype(BF16), BF16)
        k_x = matmul(mn, w_xk[l].astype(BF16), BF16)
        v_x = matmul(mn, w_xv[l].astype(BF16), BF16)
        o_x = cross_attention(q_x, k_x, v_x, batch, XATTN_HEADS)
        h, hn = matmul_res_norm(o_x, w_xo[l].astype(BF16), h, g_ffn[l])
        g_next = g_mix[l + 1] if l + 1 < depth else g_final
        if l % 2 == 0:
            e = l // 2
            h, hn = dense_ffn(hn, w_ff1[e].astype(BF16), w_ff3[e].astype(BF16), w_ff2[e].astype(BF16), h, g_next)
            out = hn
        else:
            e = l // 2
            top_idx, top_gate = moe_router(h, g_ffn[l], w_router[e])
            slot_tok, slot_of_flat, tile_e, tile_valid = moe_routing_metadata(top_idx, t)
            ys = expert_ffn(h, slot_tok, g_ffn[l],w_e1[e].astype(BF16), w_e3[e].astype(BF16), w_e2[e].astype(BF16),
                            tile_e, tile_valid)
            out = moe_combine_norm(ys, slot_of_flat, h, top_gate.T, g_next)
    return out.reshape(batch, seq, d).astype(x.dtype)
```

```python
import functools

import numpy as np
import jax
import jax.numpy as jnp
from jax import lax
from jax.experimental import pallas as pl
from jax.experimental.pallas import tpu as pltpu

F32 = jnp.float32
BF16 = jnp.bfloat16

HEAD_DIM = 128
MLA_HEADS = 6
MLA_LORA = 512
MLA_NOPE = 128
MLA_ROPE = 64
MLA_V = 128
MOBA_HEADS = 5
MOBA_BLOCK = 256
MOBA_TOPK = 3
SB_HEADS = 5
N_BRANCH = 3
ROPE_THETA = 10000.0
XATTN_HEADS = 4
N_EXPERTS = 8
MOE_TOPK = 2
RMS_EPS = 1e-6

LANES = 128
V7X_VMEM_BYTES = 64 * 1024 * 1024
VMEM_CEILING = V7X_VMEM_BYTES - 8 * 1024 * 1024

MASK_VALUE = -1e30
EXP_UNDERFLOW = -104.0

MOE_TILE = 512
GATHER_ROWS = 256


def _params(sem, est_bytes):
    limit = int(min(VMEM_CEILING, max(32 * 1024 * 1024, est_bytes * 5 // 4)))
    return pltpu.CompilerParams(dimension_semantics=sem, vmem_limit_bytes=limit)


def _nbytes(shape, dtype):
    return int(np.prod(shape)) * jnp.dtype(dtype).itemsize


def _rms(x, g):
    return x * lax.rsqrt(jnp.mean(x * x, axis=-1, keepdims=True) + RMS_EPS) * g


def _rmsnorm_kernel(x_ref, g_ref, o_ref):
    o_ref[...] = _rms(x_ref[...].astype(F32), g_ref[...]).astype(o_ref.dtype)


def rmsnorm(x, g, out_dtype, tm=512):
    m, d = x.shape
    tm = min(tm, m)
    return pl.pallas_call(
        _rmsnorm_kernel,
        grid=(m // tm,),
        in_specs=[pl.BlockSpec((tm, d), lambda i: (i, 0)),
                  pl.BlockSpec((1, d), lambda i: (0, 0))],
        out_specs=pl.BlockSpec((tm, d), lambda i: (i, 0)),
        out_shape=jax.ShapeDtypeStruct((m, d), out_dtype),
        compiler_params=_params(("parallel",), 4 * _nbytes((tm, d), F32)),
    )(x, g.reshape(1, d).astype(F32))


def _mm_kernel(a_ref, w_ref, o_ref):
    o_ref[...] = jnp.dot(a_ref[...], w_ref[...], preferred_element_type=F32).astype(o_ref.dtype)


def matmul(a, w, out_dtype, tm=1024, tn=1024):
    m, k = a.shape
    n = w.shape[1]
    tm, tn = min(tm, m), min(tn, n)
    assert m % tm == 0 and n % tn == 0, (m, n, tm, tn)
    est = 2 * (_nbytes((tm, k), a.dtype) + _nbytes((k, tn), w.dtype) + _nbytes((tm, tn), out_dtype)) \
        + _nbytes((tm, tn), F32)
    return pl.pallas_call(
        _mm_kernel,
        grid=(m // tm, n // tn),
        in_specs=[pl.BlockSpec((tm, k), lambda i, j: (i, 0)),
                  pl.BlockSpec((k, tn), lambda i, j: (0, j))],
        out_specs=pl.BlockSpec((tm, tn), lambda i, j: (i, j)),
        out_shape=jax.ShapeDtypeStruct((m, n), out_dtype),
        compiler_params=_params(("parallel", "parallel"), est),
    )(a, w)


def _mm_res_norm_kernel(a_ref, w_ref, r_ref, g_ref, h_ref, hn_ref):
    h = r_ref[...] + jnp.dot(a_ref[...], w_ref[...], preferred_element_type=F32)
    h_ref[...] = h
    hn_ref[...] = _rms(h, g_ref[...]).astype(hn_ref.dtype)


def matmul_res_norm(a, w, res, g, tm=512):
    m, k = a.shape
    n = w.shape[1]
    est = 2 * (_nbytes((tm, k), a.dtype) + 2 * _nbytes((tm, n), F32) + _nbytes((tm, n), BF16)) \
        + _nbytes((k, n), w.dtype) + 2 * _nbytes((tm, n), F32)
    return pl.pallas_call(
        _mm_res_norm_kernel,
        grid=(m // tm,),
        in_specs=[pl.BlockSpec((tm, k), lambda i: (i, 0)),
                  pl.BlockSpec((k, n), lambda i: (0, 0), pipeline_mode=pl.Buffered(1)),
                  pl.BlockSpec((tm, n), lambda i: (i, 0)),
                  pl.BlockSpec((1, n), lambda i: (0, 0))],
        out_specs=[pl.BlockSpec((tm, n), lambda i: (i, 0)),
                   pl.BlockSpec((tm, n), lambda i: (i, 0))],
        out_shape=[jax.ShapeDtypeStruct((m, n), F32), jax.ShapeDtypeStruct((m, n), BF16)],
        compiler_params=_params(("parallel",), est),
    )(a, w, res, g.reshape(1, n).astype(F32))


def _half_swap(y, c, s):
    return y * c + pltpu.roll(y, LANES // 2, 1) * s


def _mla_prep_kernel(z_ref, gq_ref, gkv_ref, wq_ref, wkv_ref, c_ref, s_ref, q_ref, k_ref, v_ref, *, scale):
    z = z_ref[...].astype(F32)
    c = c_ref[...]
    s = s_ref[...]
    nq = _rms(z[:, :MLA_LORA], gq_ref[...]).astype(BF16)
    nkv = _rms(z[:, MLA_LORA:2 * MLA_LORA], gkv_ref[...]).astype(BF16)
    q = jnp.dot(nq, wq_ref[...], preferred_element_type=F32)
    kv = jnp.dot(nkv, wkv_ref[...], preferred_element_type=F32)
    k_pe = _half_swap(z[:, 2 * MLA_LORA:], c, s).astype(BF16)
    for h in range(MLA_HEADS):
        lo = 2 * LANES * h
        q_ref[:, lo:lo + LANES] = (q[:, lo:lo + LANES] * scale).astype(BF16)
        q_ref[:, lo + LANES:lo + 2 * LANES] = (_half_swap(q[:, lo + LANES:lo + 2 * LANES], c, s) * scale).astype(BF16)
        k_ref[:, lo:lo + LANES] = kv[:, LANES * h:LANES * (h + 1)].astype(BF16)
        k_ref[:, lo + LANES:lo + 2 * LANES] = k_pe
    v_ref[...] = kv[:, MLA_HEADS * MLA_NOPE:].astype(BF16)


def mla_prep(z1, g_cq, g_ckv, wq, wkv, cos_t, sin_t, tm=512):
    m, zc = z1.shape
    nq, nkv = wq.shape[1], wkv.shape[1]
    scale = float((MLA_NOPE + MLA_ROPE) ** -0.5)
    row = lambda c: pl.BlockSpec((tm, c), lambda i: (i, 0))
    full = lambda a: pl.BlockSpec(a.shape, lambda i: (0, 0))
    g_cq = g_cq.reshape(1, -1).astype(F32)
    g_ckv = g_ckv.reshape(1, -1).astype(F32)
    est = 4 * _nbytes((tm, nq), F32) + 4 * (_nbytes(wq.shape, BF16) + _nbytes(wkv.shape, BF16))
    return pl.pallas_call(
        functools.partial(_mla_prep_kernel, scale=scale),
        grid=(m // tm,),
        in_specs=[row(zc), full(g_cq), full(g_ckv), full(wq), full(wkv), row(LANES), row(LANES)],
        out_specs=[row(nq), row(nq), row(MLA_HEADS * MLA_V)],
        out_shape=[jax.ShapeDtypeStruct((m, nq), BF16), jax.ShapeDtypeStruct((m, nq), BF16),
                   jax.ShapeDtypeStruct((m, MLA_HEADS * MLA_V), BF16)],
        compiler_params=_params(("parallel",), est),
    )(z1, g_cq, g_ckv, wq, wkv, cos_t, sin_t)


def _nt_dot(a, b):
    return lax.dot_general(a, b, (((1,), (1,)), ((), ())), preferred_element_type=F32)


def _softmax_tiles_keymajor(scores, load_vt, carry):
    p, stats = [], []
    for s, (m, l, _) in zip(scores, carry):
        m_new = jnp.maximum(m, jnp.max(s, axis=0, keepdims=True))
        alpha = jnp.exp(m - m_new)
        ph = jnp.exp(s - m_new)
        stats.append((m_new, alpha, alpha * l + jnp.sum(ph, axis=0, keepdims=True)))
        p.append(ph.astype(BF16))
    out = []
    for h, ((m_new, alpha, l_new), (_, _, acc)) in enumerate(zip(stats, carry)):
        acc = alpha * acc
        for n, vt in enumerate(load_vt(h)):
            acc = acc + jnp.dot(vt, p[h][n * vt.shape[1]:(n + 1) * vt.shape[1], :], preferred_element_type=F32)
        out.append((m_new, l_new, acc))
    return out


def _softmax_init_keymajor(queries, dv):
    return (jnp.full((1, queries), MASK_VALUE, F32), jnp.zeros((1, queries), F32), jnp.zeros((dv, queries), F32))


def _transposed_bf16(x):
    return x.astype(F32).T.astype(BF16)


def _softmax_tiles(scores, load_v, carry):
    p, stats = [], []
    for s, (m, l, _) in zip(scores, carry):
        m_new = jnp.maximum(m, jnp.max(s, axis=1, keepdims=True))
        alpha = jnp.exp(m - m_new)
        ph = jnp.exp(s - m_new)
        stats.append((m_new, alpha, alpha * l + jnp.sum(ph, axis=1, keepdims=True)))
        p.append(ph.astype(BF16))
    return [(m_new, l_new, alpha * acc + jnp.dot(p[h], load_v(h), preferred_element_type=F32))
            for h, ((m_new, alpha, l_new), (_, _, acc)) in enumerate(zip(stats, carry))]


def _softmax_init(rows, dv):
    return (jnp.full((rows, 1), MASK_VALUE, F32), jnp.zeros((rows, 1), F32), jnp.zeros((rows, dv), F32))


def _causal_attn_kernel(q_ref, k_ref, v_ref, o_ref, *, tq, tk, dk, dv, hpg):
    i = pl.program_id(2)
    n_full = (i * tq) // tk
    heads = range(hpg)
    q = [q_ref[:, h * dk:(h + 1) * dk] for h in heads]

    def tile(j, carry, mask):
        off = pl.multiple_of(j * tk, tk)
        s = [_nt_dot(q[h], k_ref[pl.ds(off, tk), h * dk:(h + 1) * dk]) for h in heads]
        if mask is not None:
            s = [jnp.where(mask, sh, MASK_VALUE) for sh in s]
        return _softmax_tiles(s, lambda h: v_ref[pl.ds(off, tk), h * dv:(h + 1) * dv], carry)

    carry = lax.fori_loop(0, n_full, lambda j, c: tile(j, c, None), [_softmax_init(tq, dv) for _ in heads])
    qpos = i * tq + lax.broadcasted_iota(jnp.int32, (tq, tk), 0)
    kpos = n_full * tk + lax.broadcasted_iota(jnp.int32, (tq, tk), 1)
    carry = tile(n_full, carry, kpos <= qpos)
    for h in heads:
        _, l, acc = carry[h]
        o_ref[:, h * dv:(h + 1) * dv] = (acc / l).astype(o_ref.dtype)


def causal_attention(q, k, v, batch, heads, dk, dv, tq=256, tk=1024, hpg=3):
    t = q.shape[0]
    s = t // batch
    nq = s // tq
    assert tk % tq == 0 and s % tk == 0 and heads % hpg == 0
    est = 4 * hpg * (_nbytes((s, dk), BF16) + _nbytes((s, dv), BF16)) + 8 * hpg * _nbytes((tq, tk), F32)
    return pl.pallas_call(
        functools.partial(_causal_attn_kernel, tq=tq, tk=tk, dk=dk, dv=dv, hpg=hpg),
        grid=(batch, heads // hpg, nq),
        in_specs=[pl.BlockSpec((tq, hpg * dk), lambda b, g, i: (b * nq + i, g)),
                  pl.BlockSpec((s, hpg * dk), lambda b, g, i: (b, g)),
                  pl.BlockSpec((s, hpg * dv), lambda b, g, i: (b, g))],
        out_specs=pl.BlockSpec((tq, hpg * dv), lambda b, g, i: (b * nq + i, g)),
        out_shape=jax.ShapeDtypeStruct((t, heads * dv), BF16),
        compiler_params=_params(("parallel", "parallel", "arbitrary"), est),
    )(q, k, v)


def _moba_select(gate_t, i, nb):
    nq = gate_t.shape[1]
    row = lax.broadcasted_iota(jnp.int32, (nb, nq), 0)
    rank = jnp.zeros((nb, nq), jnp.int32)
    for jj in range(nb):
        gj = gate_t[jj:jj + 1, :]
        ahead = jnp.logical_or(gj > gate_t, jnp.logical_and(gj == gate_t, jj < row))
        rank = rank + jnp.where(jnp.logical_and(ahead, jj < i), 1, 0)
    return jnp.where(jnp.logical_and(row < i, rank < MOBA_TOPK), 1.0, 0.0)


def _moba_kernel(q_ref, k_ref, v_ref, cq_ref, sq_ref, ck_ref, sk_ref, o_ref, kr_scr, vt_scr, km_scr, sel_scr,
                 *, nb, nh, scale):
    i = pl.program_id(1)
    blk = MOBA_BLOCK
    d = HEAD_DIM
    heads = range(nh)

    @pl.when(i == 0)
    def _():
        def prep(j, _):
            rows = pl.ds(pl.multiple_of(j * blk, blk), blk)
            c, s = ck_ref[rows, :], sk_ref[rows, :]
            for h in heads:
                kj = _half_swap(k_ref[rows, h * d:(h + 1) * d].astype(F32), c, s)
                km_scr[pl.ds(h * nb + j, 1), :] = jnp.mean(kj, axis=0, keepdims=True)
                kr_scr[rows, h * d:(h + 1) * d] = kj.astype(BF16)
                vt_scr[j, h * d:(h + 1) * d, :] = _transposed_bf16(v_ref[rows, h * d:(h + 1) * d])
            return 0
        lax.fori_loop(0, nb, prep, 0)

    cq, sq = cq_ref[...], sq_ref[...]
    qb = []
    for h in heads:
        q = _half_swap(q_ref[:, h * d:(h + 1) * d].astype(F32), cq, sq)
        gate_t = lax.dot_general(km_scr[h * nb:(h + 1) * nb, :], q, (((1,), (1,)), ((), ())),
                                 precision=lax.Precision.HIGHEST, preferred_element_type=F32)
        sel_scr[h * nb:(h + 1) * nb, :] = _moba_select(gate_t, i, nb)
        qb.append((q * scale).astype(BF16))

    def tile(j, nblk, carry, bias_fn):
        rows = pl.ds(pl.multiple_of(j * blk, blk), nblk * blk)
        s = [_nt_dot(kr_scr[rows, h * d:(h + 1) * d], qb[h]) + bias_fn(h) for h in heads]
        return _softmax_tiles_keymajor(
            s, lambda h: [vt_scr[j + n, h * d:(h + 1) * d, :] for n in range(nblk)], carry)

    key_id = lax.broadcasted_iota(jnp.int32, (blk, blk), 0)
    qry_id = lax.broadcasted_iota(jnp.int32, (blk, blk), 1)
    causal = jnp.where(key_id <= qry_id, 0.0, MASK_VALUE)
    carry = tile(i, 1, [_softmax_init_keymajor(blk, d) for _ in heads], lambda h: causal)

    def body(p, carry):
        def bias(h):
            rows = [jnp.broadcast_to((1.0 - sel_scr[pl.ds(h * nb + 2 * p + n, 1), :]) * MASK_VALUE, (blk, blk))
                    for n in range(2)]
            return jnp.concatenate(rows, axis=0)
        return tile(2 * p, 2, carry, bias)

    carry = lax.fori_loop(0, (i + 1) // 2, body, carry)
    for h in heads:
        _, l, acc = carry[h]
        o_ref[:, h * d:(h + 1) * d] = (acc / l).T.astype(o_ref.dtype)


def moba_attention(zbc, cos_t, sin_t, batch, group0):
    t = zbc.shape[0]
    s = t // batch
    blk = MOBA_BLOCK
    nb = s // blk
    nh = MOBA_HEADS
    w = nh * HEAD_DIM
    est = 6 * _nbytes((s, w), BF16) + 4 * _nbytes((s, HEAD_DIM), F32) + 12 * nh * _nbytes((blk, blk), F32)
    return pl.pallas_call(
        functools.partial(_moba_kernel, nb=nb, nh=nh, scale=float(HEAD_DIM ** -0.5)),
        grid=(batch, nb),
        in_specs=[pl.BlockSpec((blk, w), lambda b, i: (b * nb + i, group0)),
                  pl.BlockSpec((s, w), lambda b, i: (b, group0 + 1)),
                  pl.BlockSpec((s, w), lambda b, i: (b, group0 + 2)),
                  pl.BlockSpec((blk, HEAD_DIM), lambda b, i: (b * nb + i, 0)),
                  pl.BlockSpec((blk, HEAD_DIM), lambda b, i: (b * nb + i, 0)),
                  pl.BlockSpec((s, HEAD_DIM), lambda b, i: (b, 0)),
                  pl.BlockSpec((s, HEAD_DIM), lambda b, i: (b, 0))],
        out_specs=pl.BlockSpec((blk, w), lambda b, i: (b * nb + i, 0)),
        out_shape=jax.ShapeDtypeStruct((t, w), BF16),
        scratch_shapes=[pltpu.VMEM((s, w), BF16), pltpu.VMEM((nb, w, blk), BF16),
                        pltpu.VMEM((nh * nb, HEAD_DIM), F32), pltpu.VMEM((nh * nb, blk), F32)],
        compiler_params=_params(("parallel", "arbitrary"), est),
    )(zbc, zbc, zbc, cos_t, sin_t, cos_t, sin_t)


def _sb_kernel(q_ref, k_ref, v_ref, u_ref, o_ref, *, tq, nh, scale):
    i = pl.program_id(1)
    d = HEAD_DIM
    heads = range(nh)
    u = u_ref[...]
    qb = [(q_ref[:, h * d:(h + 1) * d].astype(F32) * scale).astype(BF16) for h in heads]

    def tile(j, carry, strict):
        off = pl.multiple_of(j * tq, tq)
        z = [_nt_dot(qb[h], k_ref[pl.ds(off, tq), h * d:(h + 1) * d]) for h in heads]
        hi, lo = [], []
        for h in heads:
            lsm = -(jnp.maximum(z[h], 0.0) + jnp.log(1.0 + jnp.exp(-jnp.abs(z[h]))))
            if strict is not None:
                lsm = jnp.where(strict, lsm, 0.0)
            hi.append(lsm.astype(BF16))
            lo.append((lsm - hi[h].astype(F32)).astype(BF16))
        incl = [jnp.dot(hi[h], u, preferred_element_type=F32) + jnp.dot(lo[h], u, preferred_element_type=F32)
                for h in heads]
        a = []
        for h in heads:
            ah = jnp.exp(jnp.minimum(z[h] + incl[h], 0.0) + carry[h][0])
            if strict is not None:
                ah = jnp.where(strict, ah, 0.0)
            a.append(ah.astype(BF16))
        return [(carry[h][0] + incl[h][:, 0:1],
                 carry[h][1] + jnp.dot(a[h], v_ref[pl.ds(off, tq), h * d:(h + 1) * d], preferred_element_type=F32))
                for h in heads]

    r_id = lax.broadcasted_iota(jnp.int32, (tq, tq), 0)
    c_id = lax.broadcasted_iota(jnp.int32, (tq, tq), 1)
    init = [(jnp.zeros((tq, 1), F32), jnp.zeros((tq, d), F32)) for _ in heads]
    carry = tile(i, init, c_id < r_id)

    def live(carry):
        worst = carry[0][0]
        for h in heads[1:]:
            worst = jnp.maximum(worst, carry[h][0])
        return (jnp.max(worst) > EXP_UNDERFLOW).astype(jnp.int32)

    def body(state):
        n, _, carry = state
        carry = tile(i - 1 - n, carry, None)
        return n + 1, live(carry), carry

    _, _, carry = lax.while_loop(lambda st: jnp.logical_and(st[0] < i, st[1] > 0), body,
                                 (jnp.int32(0), live(carry), carry))
    for h in heads:
        o_ref[:, h * d:(h + 1) * d] = carry[h][1].astype(o_ref.dtype)


def stick_breaking_attention(zbc, batch, group0, tq=256):
    t = zbc.shape[0]
    s = t // batch
    nq = s // tq
    nh = SB_HEADS
    w = nh * HEAD_DIM
    u = (jnp.arange(tq)[:, None] >= jnp.arange(tq)[None, :]).astype(BF16)
    est = 8 * _nbytes((s, w), BF16) + 16 * nh * _nbytes((tq, tq), F32)
    return pl.pallas_call(
        functools.partial(_sb_kernel, tq=tq, nh=nh, scale=float(HEAD_DIM ** -0.5)),
        grid=(batch, nq),
        in_specs=[pl.BlockSpec((tq, w), lambda b, i: (b * nq + i, group0)),
                  pl.BlockSpec((s, w), lambda b, i: (b, group0 + 1)),
                  pl.BlockSpec((s, w), lambda b, i: (b, group0 + 2)),
                  pl.BlockSpec((tq, tq), lambda b, i: (0, 0))],
        out_specs=pl.BlockSpec((tq, w), lambda b, i: (b * nq + i, 0)),
        out_shape=jax.ShapeDtypeStruct((t, w), BF16),
        compiler_params=_params(("parallel", "arbitrary"), est),
    )(zbc, zbc, zbc, u)


def _merge_kernel(oa_ref, ob_ref, oc_ref, ga_ref, gb_ref, gc_ref, wa_ref, wb_ref, wc_ref, o_ref):
    def branch(o, g, w):
        return jax.nn.sigmoid(g[...].astype(F32)) * jnp.dot(o[...], w[...], preferred_element_type=F32)
    o_ref[...] = (branch(oa_ref, ga_ref, wa_ref) + branch(ob_ref, gb_ref, wb_ref)
                  + branch(oc_ref, gc_ref, wc_ref)).astype(o_ref.dtype)


def gated_merge(o_a, o_b, o_c, zg, wa, wb, wc, tm=512):
    m = o_a.shape[0]
    d = wa.shape[1]
    row = lambda a: pl.BlockSpec((tm, a.shape[1]), lambda i: (i, 0))
    full = lambda a: pl.BlockSpec(a.shape, lambda i: (0, 0))
    gate = lambda n: pl.BlockSpec((tm, d), lambda i: (i, n))
    est = 4 * _nbytes((d, d), BF16) + 12 * _nbytes((tm, d), F32)
    return pl.pallas_call(
        _merge_kernel,
        grid=(m // tm,),
        in_specs=[row(o_a), row(o_b), row(o_c), gate(0), gate(1), gate(2), full(wa), full(wb), full(wc)],
        out_specs=pl.BlockSpec((tm, d), lambda i: (i, 0)),
        out_shape=jax.ShapeDtypeStruct((m, d), BF16),
        compiler_params=_params(("parallel",), est),
    )(o_a, o_b, o_c, zg, zg, zg, wa, wb, wc)


def _xattn_kernel(q_ref, k_ref, v_ref, o_ref, *, heads, scale):
    hd = q_ref.shape[1] // heads
    for h in range(heads):
        cols = slice(h * hd, (h + 1) * hd)
        s = _nt_dot(q_ref[:, cols], k_ref[:, cols]) * scale
        p = jnp.exp(s - jnp.max(s, axis=1, keepdims=True))
        o = jnp.dot(p.astype(BF16), v_ref[:, cols], preferred_element_type=F32)
        o_ref[:, cols] = (o / jnp.sum(p, axis=1, keepdims=True)).astype(o_ref.dtype)


def cross_attention(q, k, v, batch, heads, tq=512):
    t, d = q.shape
    s = t // batch
    mlen = k.shape[0] // batch
    nq = s // tq
    est = 8 * _nbytes((tq, d), BF16) + 8 * _nbytes((mlen, d), BF16) + 8 * _nbytes((tq, mlen), F32)
    return pl.pallas_call(
        functools.partial(_xattn_kernel, heads=heads, scale=float((d // heads) ** -0.5)),
        grid=(batch, nq),
        in_specs=[pl.BlockSpec((tq, d), lambda b, i: (b * nq + i, 0)),
                  pl.BlockSpec((mlen, d), lambda b, i: (b, 0)),
                  pl.BlockSpec((mlen, d), lambda b, i: (b, 0))],
        out_specs=pl.BlockSpec((tq, d), lambda b, i: (b * nq + i, 0)),
        out_shape=jax.ShapeDtypeStruct((t, d), BF16),
        compiler_params=_params(("parallel", "parallel"), est),
    )(q, k, v)


SWIGLU_CHUNK = 512


def _swiglu_up(x_ref, w1_ref, w3_ref, o_ref):
    x = x_ref[...]
    for c in range(0, o_ref.shape[1], SWIGLU_CHUNK):
        cols = slice(c, c + SWIGLU_CHUNK)
        a = jnp.dot(x, w1_ref[:, cols], preferred_element_type=F32)
        b = jnp.dot(x, w3_ref[:, cols], preferred_element_type=F32)
        o_ref[:, cols] = (a * jax.nn.sigmoid(a) * b).astype(BF16)


def _ffn_up_kernel(x_ref, w1_ref, w3_ref, o_ref):
    _swiglu_up(x_ref, w1_ref, w3_ref, o_ref)


def dense_ffn(x, w1, w3, w2, res, g_next, tm_up=1024, tf=512, tm_down=256):
    m, d = x.shape
    f = w1.shape[1]
    assert f % tf == 0
    est = 2 * (_nbytes((tm_up, d), BF16) + 2 * _nbytes((d, tf), BF16) + _nbytes((tm_up, tf), BF16)) \
        + 4 * _nbytes((tm_up, tf), F32)
    act = pl.pallas_call(
        _ffn_up_kernel,
        grid=(m // tm_up, f // tf),
        in_specs=[pl.BlockSpec((tm_up, d), lambda i, j: (i, 0)),
                  pl.BlockSpec((d, tf), lambda i, j: (0, j)),
                  pl.BlockSpec((d, tf), lambda i, j: (0, j))],
        out_specs=pl.BlockSpec((tm_up, tf), lambda i, j: (i, j)),
        out_shape=jax.ShapeDtypeStruct((m, f), BF16),
        compiler_params=_params(("parallel", "parallel"), est),
    )(x, w1, w3)
    return matmul_res_norm(act, w2, res, g_next, tm=tm_down)


def _router_kernel(h_ref, g_ref, wr_ref, u_ref, idx_ref, gate_ref, rank_ref, count_ref, seen_scr):
    @pl.when(pl.program_id(0) == 0)
    def _():
        seen_scr[...] = jnp.zeros_like(seen_scr)

    hn = _rms(h_ref[...], g_ref[...])
    logits = lax.dot_general(wr_ref[...], hn, (((1,), (1,)), ((), ())),
                             precision=lax.Precision.HIGHEST, preferred_element_type=F32)
    e_id = lax.broadcasted_iota(jnp.int32, logits.shape, 0)
    n_e = logits.shape[0]
    v1 = jnp.max(logits, axis=0, keepdims=True)
    i1 = jnp.min(jnp.where(logits == v1, e_id, n_e), axis=0, keepdims=True)
    rest = jnp.where(e_id == i1, -jnp.inf, logits)
    v2 = jnp.max(rest, axis=0, keepdims=True)
    i2 = jnp.min(jnp.where(rest == v2, e_id, n_e), axis=0, keepdims=True)
    e2 = jnp.exp(v2 - v1)
    idx_ref[0:1, :] = i1
    idx_ref[1:2, :] = i2
    gate_ref[0:1, :] = 1.0 / (1.0 + e2)
    gate_ref[1:2, :] = e2 / (1.0 + e2)
    pick1 = jnp.where(e_id == i1, 1.0, 0.0)
    pick2 = jnp.where(e_id == i2, 1.0, 0.0)
    both = pick1 + pick2
    earlier = jnp.dot(both.astype(BF16), u_ref[...], preferred_element_type=F32) + seen_scr[...]
    rank_ref[0:1, :] = jnp.sum(pick1 * earlier, axis=0, keepdims=True).astype(jnp.int32)
    rank_ref[1:2, :] = jnp.sum(pick2 * earlier, axis=0, keepdims=True).astype(jnp.int32)
    seen_scr[...] += jnp.sum(both, axis=1, keepdims=True)
    count_ref[...] = seen_scr[...].astype(jnp.int32)


def moe_router(h, g, w_router, tm=512):
    m, d = h.shape
    n_e = w_router.shape[1]
    wr_t = w_router.T.astype(F32)
    u = (jnp.arange(tm)[:, None] < jnp.arange(tm)[None, :]).astype(BF16)
    est = 6 * _nbytes((tm, d), F32)
    return pl.pallas_call(
        _router_kernel,
        grid=(m // tm,),
        in_specs=[pl.BlockSpec((tm, d), lambda i: (i, 0)),
                  pl.BlockSpec((1, d), lambda i: (0, 0)),
                  pl.BlockSpec((n_e, d), lambda i: (0, 0)),
                  pl.BlockSpec((tm, tm), lambda i: (0, 0))],
        out_specs=[pl.BlockSpec((MOE_TOPK, tm), lambda i: (0, i)),
                   pl.BlockSpec((MOE_TOPK, tm), lambda i: (0, i)),
                   pl.BlockSpec((MOE_TOPK, tm), lambda i: (0, i)),
                   pl.BlockSpec((n_e, 1), lambda i: (0, 0))],
        out_shape=[jax.ShapeDtypeStruct((MOE_TOPK, m), jnp.int32), jax.ShapeDtypeStruct((MOE_TOPK, m), F32),
                   jax.ShapeDtypeStruct((MOE_TOPK, m), jnp.int32), jax.ShapeDtypeStruct((n_e, 1), jnp.int32)],
        scratch_shapes=[pltpu.VMEM((n_e, 1), F32)],
        compiler_params=_params(("arbitrary",), est),
    )(h, g.reshape(1, d).astype(F32), wr_t, u)


def _row_copy(src_hbm, src_row, dst_ref, dst_row, sem):
    return pltpu.make_async_copy(src_hbm.at[pl.ds(src_row, 1)], dst_ref.at[pl.ds(dst_row, 1)], sem)


PREFETCH_ROWS = 128


def _expert_up_kernel(te_ref, tv_ref, rows_ref, h_hbm, g_ref, w1_ref, w3_ref, o_ref, xbuf, xn_ref, sem):
    t = pl.program_id(0)
    j = pl.program_id(1)
    tm = MOE_TILE
    slot = lax.rem(t, 2)

    def request(tile, first, count):
        dst, dsem = xbuf.at[lax.rem(tile, 2)], sem.at[lax.rem(tile, 2)]

        def start(r, _):
            _row_copy(h_hbm, rows_ref[tile * tm + first + r], dst, first + r, dsem).start()
            return 0
        lax.fori_loop(0, count, start, 0, unroll=8)

    @pl.when(jnp.logical_and(t == 0, j == 0))
    def _():
        request(0, 0, tm)

    nxt = jnp.minimum(t + 1, pl.num_programs(0) - 1)
    @pl.when(jnp.logical_and(jnp.logical_and(t + 1 < pl.num_programs(0), tv_ref[nxt] > 0),
                             j < tm // PREFETCH_ROWS))
    def _():
        request(t + 1, j * PREFETCH_ROWS, PREFETCH_ROWS)

    @pl.when(jnp.logical_and(j == 0, tv_ref[t] > 0))
    def _():
        def wait(r, _):
            _row_copy(h_hbm, 0, xbuf.at[slot], r, sem.at[slot]).wait()
            return 0
        lax.fori_loop(0, tm, wait, 0, unroll=8)
        xn_ref[...] = _rms(xbuf[slot], g_ref[...]).astype(BF16)

    @pl.when(tv_ref[t] > 0)
    def _():
        _swiglu_up(xn_ref, w1_ref, w3_ref, o_ref)

    @pl.when(tv_ref[t] == 0)
    def _():
        o_ref[...] = jnp.zeros_like(o_ref)


def _expert_down_kernel(te_ref, tv_ref, a_ref, w_ref, o_ref):
    t = pl.program_id(0)

    @pl.when(pl.program_id(1) == 0)
    def _():
        o_ref[...] = jnp.zeros_like(o_ref)

    @pl.when(tv_ref[t] > 0)
    def _():
        o_ref[...] += jnp.dot(a_ref[...], w_ref[...], preferred_element_type=F32)


def expert_ffn(h, slot_tok, g, w1, w3, w2, tile_e, tile_valid, tf=1024, nk=4):
    d = h.shape[1]
    n = slot_tok.shape[0]
    f = w1.shape[2]
    nf = f // tf
    tk = f // nk
    tm = MOE_TILE
    assert tm % PREFETCH_ROWS == 0 and nf >= tm // PREFETCH_ROWS
    assert f % tf == 0 and f % nk == 0 and tk % LANES == 0

    def hold(last):
        return lambda t, j, tv: j * tv[t] + last * (1 - tv[t])

    ju = hold(nf - 1)
    est = 2 * (_nbytes((tm, d), F32) + 2 * _nbytes((d, tf), BF16) + _nbytes((tm, tf), BF16)) \
        + _nbytes((tm, d), BF16) + 4 * _nbytes((tm, tf), F32) + _nbytes((tm, d), F32)
    act = pl.pallas_call(
        _expert_up_kernel,
        grid_spec=pltpu.PrefetchScalarGridSpec(
            num_scalar_prefetch=3,
            grid=(n // tm, nf),
            in_specs=[pl.BlockSpec(memory_space=pl.ANY),
                      pl.BlockSpec((1, d), lambda t, j, te, tv, rows: (0, 0)),
                      pl.BlockSpec((None, d, tf), lambda t, j, te, tv, rows: (te[t], 0, ju(t, j, tv))),
                      pl.BlockSpec((None, d, tf), lambda t, j, te, tv, rows: (te[t], 0, ju(t, j, tv)))],
            out_specs=pl.BlockSpec((tm, tf), lambda t, j, te, tv, rows: (t, j)),
            scratch_shapes=[pltpu.VMEM((2, tm, d), F32), pltpu.VMEM((tm, d), BF16),
                            pltpu.SemaphoreType.DMA((2,))],
        ),
        out_shape=jax.ShapeDtypeStruct((n, f), BF16),
        compiler_params=_params(("arbitrary", "arbitrary"), est),
    )(tile_e, tile_valid, slot_tok, h, g.reshape(1, d).astype(F32), w1, w3)
    jd = hold(nk - 1)
    est = 2 * (_nbytes((tm, tk), BF16) + _nbytes((tk, d), BF16) + _nbytes((tm, d), F32)) + 2 * _nbytes((tm, d), F32)
    return pl.pallas_call(
        _expert_down_kernel,
        grid_spec=pltpu.PrefetchScalarGridSpec(
            num_scalar_prefetch=2,
            grid=(n // tm, nk),
            in_specs=[pl.BlockSpec((tm, tk), lambda t, k, te, tv: (t, jd(t, k, tv))),
                      pl.BlockSpec((None, tk, d), lambda t, k, te, tv: (te[t], jd(t, k, tv), 0))],
            out_specs=pl.BlockSpec((tm, d), lambda t, k, te, tv: (t, 0)),
        ),
        out_shape=jax.ShapeDtypeStruct((n, d), F32),
        compiler_params=_params(("arbitrary", "arbitrary"), est),
    )(tile_e, tile_valid, act, w2)


def _combine_kernel(slots_ref, ys_hbm, h_ref, gate_ref, g_ref, o_ref, buf0, buf1, sem):
    base = pl.program_id(0) * GATHER_ROWS

    def start(r, _):
        flat = (base + r) * MOE_TOPK
        _row_copy(ys_hbm, slots_ref[flat], buf0, r, sem).start(priority=0)
        _row_copy(ys_hbm, slots_ref[flat + 1], buf1, r, sem).start(priority=1)
        return 0

    def wait(r, _):
        _row_copy(ys_hbm, 0, buf0, r, sem).wait()
        _row_copy(ys_hbm, 0, buf1, r, sem).wait()
        return 0

    lax.fori_loop(0, GATHER_ROWS, start, 0, unroll=8)
    lax.fori_loop(0, GATHER_ROWS, wait, 0, unroll=8)
    gate = gate_ref[...]
    h = h_ref[...] + gate[:, 0:1] * buf0[...] + gate[:, 1:2] * buf1[...]
    o_ref[...] = _rms(h, g_ref[...])


def moe_combine_norm(ys, slot_of_flat, h, gates, g_out):
    m, d = h.shape
    tm = GATHER_ROWS
    return pl.pallas_call(
        _combine_kernel,
        grid_spec=pltpu.PrefetchScalarGridSpec(
            num_scalar_prefetch=1,
            grid=(m // tm,),
            in_specs=[pl.BlockSpec(memory_space=pl.ANY),
                      pl.BlockSpec((tm, d), lambda i, s: (i, 0)),
                      pl.BlockSpec((tm, MOE_TOPK), lambda i, s: (i, 0)),
                      pl.BlockSpec((1, d), lambda i, s: (0, 0))],
            out_specs=pl.BlockSpec((tm, d), lambda i, s: (i, 0)),
            scratch_shapes=[pltpu.VMEM((tm, d), F32), pltpu.VMEM((tm, d), F32), pltpu.SemaphoreType.DMA(())],
        ),
        out_shape=jax.ShapeDtypeStruct((m, d), F32),
        compiler_params=_params(("arbitrary",), 8 * _nbytes((tm, d), F32)),
    )(slot_of_flat, ys, h, gates, g_out.reshape(1, d).astype(F32))


def moe_routing_metadata(top_idx, top_rank, counts, n_tokens):
    tk = n_tokens * MOE_TOPK
    flat_e = top_idx.T.reshape(-1)
    within = top_rank.T.reshape(-1)
    padded = (counts + MOE_TILE - 1) // MOE_TILE * MOE_TILE
    pad_end = jnp.cumsum(padded)
    pad_start = pad_end - padded
    slot_of_flat = (jnp.sum(jnp.where(flat_e[:, None] == jnp.arange(N_EXPERTS)[None, :], pad_start[None, :], 0),
                            axis=1) + within).astype(jnp.int32)
    n_tiles = tk // MOE_TILE + N_EXPERTS
    slot_tok = jnp.zeros((n_tiles * MOE_TILE,), jnp.int32).at[slot_of_flat].set(
        jnp.arange(tk, dtype=jnp.int32) // MOE_TOPK)
    tile_start = jnp.arange(n_tiles, dtype=jnp.int32) * MOE_TILE
    tile_valid = (tile_start < pad_end[-1]).astype(jnp.int32)
    tile_e = jnp.minimum(jnp.sum(tile_start[:, None] >= pad_end[None, :], axis=1), N_EXPERTS - 1)
    last_e = jnp.max(jnp.where(tile_valid > 0, tile_e, 0))
    tile_e = jnp.where(tile_valid > 0, tile_e, last_e).astype(jnp.int32)
    return slot_tok, slot_of_flat, tile_e, tile_valid


def _rope_partner(w):
    half = w.shape[-1] // 2
    return jnp.concatenate([-w[..., half:], w[..., :half]], axis=-1)


def _rope_tables(positions):
    pos = positions.astype(F32).reshape(-1, 1)

    def cs(half):
        inv_freq = ROPE_THETA ** (-jnp.arange(half, dtype=F32) / half)
        ang = pos * inv_freq
        return jnp.cos(ang), jnp.sin(ang)

    c32, s32 = cs(MLA_ROPE // 2)
    zeros = jnp.zeros((pos.shape[0], LANES // 2), F32)
    mla_c = jnp.concatenate([c32, c32, zeros], axis=1)
    mla_s = jnp.concatenate([s32, s32, zeros], axis=1)
    c64, s64 = cs(HEAD_DIM // 2)
    moba_c = jnp.concatenate([c64, c64], axis=1)
    moba_s = jnp.concatenate([-s64, s64], axis=1)
    return mla_c, mla_s, moba_c, moba_s


def _layer_weights(w_in, w_uq, w_ukv):
    d = w_in.shape[0]
    o = 0
    w_cq = w_in[:, o:o + MLA_LORA]; o += MLA_LORA
    w_ckv = w_in[:, o:o + MLA_LORA]; o += MLA_LORA
    w_kr = w_in[:, o:o + MLA_ROPE]; o += MLA_ROPE
    nbc = 3 * (MOBA_HEADS + SB_HEADS) * HEAD_DIM
    w_bc = w_in[:, o:o + nbc]; o += nbc
    w_g = w_in[:, o:]
    w1 = jnp.concatenate([w_cq, w_ckv, w_kr, _rope_partner(w_kr)], axis=1).astype(BF16)
    uq = w_uq.reshape(MLA_LORA, MLA_HEADS, MLA_NOPE + MLA_ROPE)
    uq_rope = uq[..., MLA_NOPE:]
    wq = jnp.concatenate([uq[..., :MLA_NOPE], uq_rope, _rope_partner(uq_rope)], axis=-1)
    wq = wq.reshape(MLA_LORA, MLA_HEADS * 2 * LANES).astype(BF16)
    ukv = w_ukv.reshape(MLA_LORA, MLA_HEADS, MLA_NOPE + MLA_V)
    wkv = jnp.concatenate([ukv[..., :MLA_NOPE].reshape(MLA_LORA, -1), ukv[..., MLA_NOPE:].reshape(MLA_LORA, -1)],
                          axis=1).astype(BF16)
    return w1, w_bc.astype(BF16), w_g.astype(BF16), wq, wkv


def kernel(x, mem, positions, g_mix, w_in, g_cq, g_ckv, w_uq, w_ukv, w_up_a, w_up_b, w_up_c, w_o, g_x, g_mem, w_xq, w_xk, w_xv, w_xo, g_ffn, w_ff1, w_ff3, w_ff2, w_router, w_e1, w_e3, w_e2, g_final):
    batch, seq, d = x.shape
    t = batch * seq
    depth = g_mix.shape[0]
    assert depth == 2, "the final norm is fused into the expert layer, which must come last"
    mem2 = mem.reshape(-1, d)
    mla_c, mla_s, moba_c, moba_s = _rope_tables(positions)
    assert MOBA_HEADS == SB_HEADS

    h = x.reshape(t, d)
    hn = rmsnorm(h, g_mix[0], BF16)
    out = None
    for l in range(depth):
        w1, w_bc, w_g, wq, wkv = _layer_weights(w_in[l], w_uq[l], w_ukv[l])
        z1 = matmul(hn, w1, BF16, tn=w1.shape[1])
        zbc = matmul(hn, w_bc, BF16, tn=1280)
        zg = matmul(hn, w_g, BF16, tn=1024)
        q_a, k_a, v_a = mla_prep(z1, g_cq[l], g_ckv[l], wq, wkv, mla_c, mla_s)
        o_a = causal_attention(q_a, k_a, v_a, batch, MLA_HEADS, 2 * LANES, MLA_V)
        o_b = moba_attention(zbc, moba_c, moba_s, batch, 0)
        o_c = stick_breaking_attention(zbc, batch, 3)
        merged = gated_merge(o_a, o_b, o_c, zg, w_up_a[l].astype(BF16), w_up_b[l].astype(BF16),
                             w_up_c[l].astype(BF16))
        h, hn = matmul_res_norm(merged, w_o[l].astype(BF16), h, g_x[l])
        mn = rmsnorm(mem2, g_mem[l], BF16)
        q_x = matmul(hn, w_xq[l].astype(BF16), BF16)
        k_x = matmul(mn, w_xk[l].astype(BF16), BF16)
        v_x = matmul(mn, w_xv[l].astype(BF16), BF16)
        o_x = cross_attention(q_x, k_x, v_x, batch, XATTN_HEADS)
        h, hn = matmul_res_norm(o_x, w_xo[l].astype(BF16), h, g_ffn[l])
        g_next = g_mix[l + 1] if l + 1 < depth else g_final
        if l % 2 == 0:
            e = l // 2
            h, hn = dense_ffn(hn, w_ff1[e].astype(BF16), w_ff3[e].astype(BF16), w_ff2[e].astype(BF16), h, g_next)
            out = hn
        else:
            e = l // 2
            top_idx, top_gate, top_rank, counts = moe_router(h, g_ffn[l], w_router[e])
            slot_tok, slot_of_flat, tile_e, tile_valid = moe_routing_metadata(top_idx, top_rank, counts[:, 0], t)
            ys = expert_ffn(h, slot_tok, g_ffn[l],w_e1[e].astype(BF16), w_e3[e].astype(BF16), w_e2[e].astype(BF16),
                            tile_e, tile_valid)
            out = moe_combine_norm(ys, slot_of_flat, h, top_gate.T, g_next)
    return out.reshape(batch, seq, d).astype(x.dtype)
```

```python
import functools

import numpy as np
import jax
import jax.numpy as jnp
from jax import lax
from jax.experimental import pallas as pl
from jax.experimental.pallas import tpu as pltpu

F32 = jnp.float32
BF16 = jnp.bfloat16

HEAD_DIM = 128
MLA_HEADS = 6
MLA_LORA = 512
MLA_NOPE = 128
MLA_ROPE = 64
MLA_V = 128
MOBA_HEADS = 5
MOBA_BLOCK = 256
MOBA_TOPK = 3
SB_HEADS = 5
N_BRANCH = 3
ROPE_THETA = 10000.0
XATTN_HEADS = 4
N_EXPERTS = 8
MOE_TOPK = 2
RMS_EPS = 1e-6

LANES = 128
V7X_VMEM_BYTES = 64 * 1024 * 1024
VMEM_CEILING = V7X_VMEM_BYTES - 8 * 1024 * 1024

MASK_VALUE = -1e30
EXP_UNDERFLOW = -104.0

MOE_TILE = 512
GATHER_ROWS = 256


def _params(sem, est_bytes):
    limit = int(min(VMEM_CEILING, max(32 * 1024 * 1024, est_bytes * 5 // 4)))
    return pltpu.CompilerParams(dimension_semantics=sem, vmem_limit_bytes=limit)


def _nbytes(shape, dtype):
    return int(np.prod(shape)) * jnp.dtype(dtype).itemsize


def _rms(x, g):
    return x * lax.rsqrt(jnp.mean(x * x, axis=-1, keepdims=True) + RMS_EPS) * g


def _rmsnorm_kernel(x_ref, g_ref, o_ref):
    o_ref[...] = _rms(x_ref[...].astype(F32), g_ref[...]).astype(o_ref.dtype)


def rmsnorm(x, g, out_dtype, tm=512):
    m, d = x.shape
    tm = min(tm, m)
    return pl.pallas_call(
        _rmsnorm_kernel,
        grid=(m // tm,),
        in_specs=[pl.BlockSpec((tm, d), lambda i: (i, 0)),
                  pl.BlockSpec((1, d), lambda i: (0, 0))],
        out_specs=pl.BlockSpec((tm, d), lambda i: (i, 0)),
        out_shape=jax.ShapeDtypeStruct((m, d), out_dtype),
        compiler_params=_params(("parallel",), 4 * _nbytes((tm, d), F32)),
    )(x, g.reshape(1, d).astype(F32))


def _mm_kernel(a_ref, w_ref, o_ref):
    o_ref[...] = jnp.dot(a_ref[...], w_ref[...], preferred_element_type=F32).astype(o_ref.dtype)


def matmul(a, w, out_dtype, tm=1024, tn=1024):
    m, k = a.shape
    n = w.shape[1]
    tm, tn = min(tm, m), min(tn, n)
    assert m % tm == 0 and n % tn == 0, (m, n, tm, tn)
    est = 2 * (_nbytes((tm, k), a.dtype) + _nbytes((k, tn), w.dtype) + _nbytes((tm, tn), out_dtype)) \
        + _nbytes((tm, tn), F32)
    return pl.pallas_call(
        _mm_kernel,
        grid=(m // tm, n // tn),
        in_specs=[pl.BlockSpec((tm, k), lambda i, j: (i, 0)),
                  pl.BlockSpec((k, tn), lambda i, j: (0, j))],
        out_specs=pl.BlockSpec((tm, tn), lambda i, j: (i, j)),
        out_shape=jax.ShapeDtypeStruct((m, n), out_dtype),
        compiler_params=_params(("parallel", "parallel"), est),
    )(a, w)


def _mm_res_norm_kernel(a_ref, w_ref, r_ref, g_ref, h_ref, hn_ref):
    h = r_ref[...] + jnp.dot(a_ref[...], w_ref[...], preferred_element_type=F32)
    h_ref[...] = h
    hn_ref[...] = _rms(h, g_ref[...]).astype(hn_ref.dtype)


def matmul_res_norm(a, w, res, g, tm=512):
    m, k = a.shape
    n = w.shape[1]
    est = 2 * (_nbytes((tm, k), a.dtype) + 2 * _nbytes((tm, n), F32) + _nbytes((tm, n), BF16)) \
        + _nbytes((k, n), w.dtype) + 2 * _nbytes((tm, n), F32)
    return pl.pallas_call(
        _mm_res_norm_kernel,
        grid=(m // tm,),
        in_specs=[pl.BlockSpec((tm, k), lambda i: (i, 0)),
                  pl.BlockSpec((k, n), lambda i: (0, 0), pipeline_mode=pl.Buffered(1)),
                  pl.BlockSpec((tm, n), lambda i: (i, 0)),
                  pl.BlockSpec((1, n), lambda i: (0, 0))],
        out_specs=[pl.BlockSpec((tm, n), lambda i: (i, 0)),
                   pl.BlockSpec((tm, n), lambda i: (i, 0))],
        out_shape=[jax.ShapeDtypeStruct((m, n), F32), jax.ShapeDtypeStruct((m, n), BF16)],
        compiler_params=_params(("parallel",), est),
    )(a, w, res, g.reshape(1, n).astype(F32))


def _half_swap(y, c, s):
    return y * c + pltpu.roll(y, LANES // 2, 1) * s


def _mla_prep_kernel(z_ref, gq_ref, gkv_ref, wq_ref, wkv_ref, c_ref, s_ref, q_ref, k_ref, v_ref, *, scale):
    z = z_ref[...].astype(F32)
    c = c_ref[...]
    s = s_ref[...]
    nq = _rms(z[:, :MLA_LORA], gq_ref[...]).astype(BF16)
    nkv = _rms(z[:, MLA_LORA:2 * MLA_LORA], gkv_ref[...]).astype(BF16)
    q = jnp.dot(nq, wq_ref[...], preferred_element_type=F32)
    kv = jnp.dot(nkv, wkv_ref[...], preferred_element_type=F32)
    k_pe = _half_swap(z[:, 2 * MLA_LORA:], c, s).astype(BF16)
    for h in range(MLA_HEADS):
        lo = 2 * LANES * h
        q_ref[:, lo:lo + LANES] = (q[:, lo:lo + LANES] * scale).astype(BF16)
        q_ref[:, lo + LANES:lo + 2 * LANES] = (_half_swap(q[:, lo + LANES:lo + 2 * LANES], c, s) * scale).astype(BF16)
        k_ref[:, lo:lo + LANES] = kv[:, LANES * h:LANES * (h + 1)].astype(BF16)
        k_ref[:, lo + LANES:lo + 2 * LANES] = k_pe
    v_ref[...] = kv[:, MLA_HEADS * MLA_NOPE:].astype(BF16)


def mla_prep(z1, g_cq, g_ckv, wq, wkv, cos_t, sin_t, tm=512):
    m, zc = z1.shape
    nq, nkv = wq.shape[1], wkv.shape[1]
    scale = float((MLA_NOPE + MLA_ROPE) ** -0.5)
    row = lambda c: pl.BlockSpec((tm, c), lambda i: (i, 0))
    full = lambda a: pl.BlockSpec(a.shape, lambda i: (0, 0))
    g_cq = g_cq.reshape(1, -1).astype(F32)
    g_ckv = g_ckv.reshape(1, -1).astype(F32)
    est = 4 * _nbytes((tm, nq), F32) + 4 * (_nbytes(wq.shape, BF16) + _nbytes(wkv.shape, BF16))
    return pl.pallas_call(
        functools.partial(_mla_prep_kernel, scale=scale),
        grid=(m // tm,),
        in_specs=[row(zc), full(g_cq), full(g_ckv), full(wq), full(wkv), row(LANES), row(LANES)],
        out_specs=[row(nq), row(nq), row(MLA_HEADS * MLA_V)],
        out_shape=[jax.ShapeDtypeStruct((m, nq), BF16), jax.ShapeDtypeStruct((m, nq), BF16),
                   jax.ShapeDtypeStruct((m, MLA_HEADS * MLA_V), BF16)],
        compiler_params=_params(("parallel",), est),
    )(z1, g_cq, g_ckv, wq, wkv, cos_t, sin_t)


def _nt_dot(a, b):
    return lax.dot_general(a, b, (((1,), (1,)), ((), ())), preferred_element_type=F32)


def _softmax_tiles_keymajor(scores, load_vt, carry):
    p, stats = [], []
    for s, (m, l, _) in zip(scores, carry):
        m_new = jnp.maximum(m, jnp.max(s, axis=0, keepdims=True))
        alpha = jnp.exp(m - m_new)
        ph = jnp.exp(s - m_new)
        stats.append((m_new, alpha, alpha * l + jnp.sum(ph, axis=0, keepdims=True)))
        p.append(ph.astype(BF16))
    out = []
    for h, ((m_new, alpha, l_new), (_, _, acc)) in enumerate(zip(stats, carry)):
        acc = alpha * acc
        for n, vt in enumerate(load_vt(h)):
            acc = acc + jnp.dot(vt, p[h][n * vt.shape[1]:(n + 1) * vt.shape[1], :], preferred_element_type=F32)
        out.append((m_new, l_new, acc))
    return out


def _softmax_init_keymajor(queries, dv):
    return (jnp.full((1, queries), MASK_VALUE, F32), jnp.zeros((1, queries), F32), jnp.zeros((dv, queries), F32))


def _transposed_bf16(x):
    return x.astype(F32).T.astype(BF16)


def _softmax_tiles(scores, load_v, carry):
    p, stats = [], []
    for s, (m, l, _) in zip(scores, carry):
        m_new = jnp.maximum(m, jnp.max(s, axis=1, keepdims=True))
        alpha = jnp.exp(m - m_new)
        ph = jnp.exp(s - m_new)
        stats.append((m_new, alpha, alpha * l + jnp.sum(ph, axis=1, keepdims=True)))
        p.append(ph.astype(BF16))
    return [(m_new, l_new, alpha * acc + jnp.dot(p[h], load_v(h), preferred_element_type=F32))
            for h, ((m_new, alpha, l_new), (_, _, acc)) in enumerate(zip(stats, carry))]


def _softmax_init(rows, dv):
    return (jnp.full((rows, 1), MASK_VALUE, F32), jnp.zeros((rows, 1), F32), jnp.zeros((rows, dv), F32))


def _causal_attn_kernel(q_ref, k_ref, v_ref, o_ref, *, tq, tk, dk, dv, hpg):
    i = pl.program_id(2)
    n_full = (i * tq) // tk
    heads = range(hpg)
    q = [q_ref[:, h * dk:(h + 1) * dk] for h in heads]

    def tile(j, carry, mask):
        off = pl.multiple_of(j * tk, tk)
        s = [_nt_dot(q[h], k_ref[pl.ds(off, tk), h * dk:(h + 1) * dk]) for h in heads]
        if mask is not None:
            s = [jnp.where(mask, sh, MASK_VALUE) for sh in s]
        return _softmax_tiles(s, lambda h: v_ref[pl.ds(off, tk), h * dv:(h + 1) * dv], carry)

    carry = lax.fori_loop(0, n_full, lambda j, c: tile(j, c, None), [_softmax_init(tq, dv) for _ in heads])
    qpos = i * tq + lax.broadcasted_iota(jnp.int32, (tq, tk), 0)
    kpos = n_full * tk + lax.broadcasted_iota(jnp.int32, (tq, tk), 1)
    carry = tile(n_full, carry, kpos <= qpos)
    for h in heads:
        _, l, acc = carry[h]
        o_ref[:, h * dv:(h + 1) * dv] = (acc / l).astype(o_ref.dtype)


def causal_attention(q, k, v, batch, heads, dk, dv, tq=256, tk=1024, hpg=3):
    t = q.shape[0]
    s = t // batch
    nq = s // tq
    assert tk % tq == 0 and s % tk == 0 and heads % hpg == 0
    est = 4 * hpg * (_nbytes((s, dk), BF16) + _nbytes((s, dv), BF16)) + 8 * hpg * _nbytes((tq, tk), F32)
    return pl.pallas_call(
        functools.partial(_causal_attn_kernel, tq=tq, tk=tk, dk=dk, dv=dv, hpg=hpg),
        grid=(batch, heads // hpg, nq),
        in_specs=[pl.BlockSpec((tq, hpg * dk), lambda b, g, i: (b * nq + i, g)),
                  pl.BlockSpec((s, hpg * dk), lambda b, g, i: (b, g)),
                  pl.BlockSpec((s, hpg * dv), lambda b, g, i: (b, g))],
        out_specs=pl.BlockSpec((tq, hpg * dv), lambda b, g, i: (b * nq + i, g)),
        out_shape=jax.ShapeDtypeStruct((t, heads * dv), BF16),
        compiler_params=_params(("parallel", "parallel", "arbitrary"), est),
    )(q, k, v)


def _moba_select(gate_t, i, nb):
    nq = gate_t.shape[1]
    row = lax.broadcasted_iota(jnp.int32, (nb, nq), 0)
    rank = jnp.zeros((nb, nq), jnp.int32)
    for jj in range(nb):
        gj = gate_t[jj:jj + 1, :]
        ahead = jnp.logical_or(gj > gate_t, jnp.logical_and(gj == gate_t, jj < row))
        rank = rank + jnp.where(jnp.logical_and(ahead, jj < i), 1, 0)
    return jnp.where(jnp.logical_and(row < i, rank < MOBA_TOPK), 1.0, 0.0)


def _moba_kernel(q_ref, k_ref, v_ref, cq_ref, sq_ref, ck_ref, sk_ref, o_ref, kr_scr, vt_scr, km_scr, sel_scr,
                 *, nb, nh, scale):
    i = pl.program_id(1)
    blk = MOBA_BLOCK
    d = HEAD_DIM
    heads = range(nh)

    @pl.when(i == 0)
    def _():
        def prep(j, _):
            rows = pl.ds(pl.multiple_of(j * blk, blk), blk)
            c, s = ck_ref[rows, :], sk_ref[rows, :]
            for h in heads:
                kj = _half_swap(k_ref[rows, h * d:(h + 1) * d].astype(F32), c, s)
                km_scr[pl.ds(h * nb + j, 1), :] = jnp.mean(kj, axis=0, keepdims=True)
                kr_scr[rows, h * d:(h + 1) * d] = kj.astype(BF16)
                vt_scr[j, h * d:(h + 1) * d, :] = _transposed_bf16(v_ref[rows, h * d:(h + 1) * d])
            return 0
        lax.fori_loop(0, nb, prep, 0)

    cq, sq = cq_ref[...], sq_ref[...]
    qb = []
    for h in heads:
        q = _half_swap(q_ref[:, h * d:(h + 1) * d].astype(F32), cq, sq)
        gate_t = lax.dot_general(km_scr[h * nb:(h + 1) * nb, :], q, (((1,), (1,)), ((), ())),
                                 precision=lax.Precision.HIGHEST, preferred_element_type=F32)
        sel_scr[h * nb:(h + 1) * nb, :] = _moba_select(gate_t, i, nb)
        qb.append((q * scale).astype(BF16))

    def tile(j, nblk, carry, bias_fn):
        rows = pl.ds(pl.multiple_of(j * blk, blk), nblk * blk)
        s = [_nt_dot(kr_scr[rows, h * d:(h + 1) * d], qb[h]) + bias_fn(h) for h in heads]
        return _softmax_tiles_keymajor(
            s, lambda h: [vt_scr[j + n, h * d:(h + 1) * d, :] for n in range(nblk)], carry)

    key_id = lax.broadcasted_iota(jnp.int32, (blk, blk), 0)
    qry_id = lax.broadcasted_iota(jnp.int32, (blk, blk), 1)
    causal = jnp.where(key_id <= qry_id, 0.0, MASK_VALUE)
    carry = tile(i, 1, [_softmax_init_keymajor(blk, d) for _ in heads], lambda h: causal)

    def body(p, carry):
        def bias(h):
            rows = [jnp.broadcast_to((1.0 - sel_scr[pl.ds(h * nb + 2 * p + n, 1), :]) * MASK_VALUE, (blk, blk))
                    for n in range(2)]
            return jnp.concatenate(rows, axis=0)
        return tile(2 * p, 2, carry, bias)

    carry = lax.fori_loop(0, (i + 1) // 2, body, carry)
    for h in heads:
        _, l, acc = carry[h]
        o_ref[:, h * d:(h + 1) * d] = (acc / l).T.astype(o_ref.dtype)


def moba_attention(zbc, cos_t, sin_t, batch, group0):
    t = zbc.shape[0]
    s = t // batch
    blk = MOBA_BLOCK
    nb = s // blk
    nh = MOBA_HEADS
    w = nh * HEAD_DIM
    est = 6 * _nbytes((s, w), BF16) + 4 * _nbytes((s, HEAD_DIM), F32) + 12 * nh * _nbytes((blk, blk), F32)
    return pl.pallas_call(
        functools.partial(_moba_kernel, nb=nb, nh=nh, scale=float(HEAD_DIM ** -0.5)),
        grid=(batch, nb),
        in_specs=[pl.BlockSpec((blk, w), lambda b, i: (b * nb + i, group0)),
                  pl.BlockSpec((s, w), lambda b, i: (b, group0 + 1)),
                  pl.BlockSpec((s, w), lambda b, i: (b, group0 + 2)),
                  pl.BlockSpec((blk, HEAD_DIM), lambda b, i: (b * nb + i, 0)),
                  pl.BlockSpec((blk, HEAD_DIM), lambda b, i: (b * nb + i, 0)),
                  pl.BlockSpec((s, HEAD_DIM), lambda b, i: (b, 0)),
                  pl.BlockSpec((s, HEAD_DIM), lambda b, i: (b, 0))],
        out_specs=pl.BlockSpec((blk, w), lambda b, i: (b * nb + i, 0)),
        out_shape=jax.ShapeDtypeStruct((t, w), BF16),
        scratch_shapes=[pltpu.VMEM((s, w), BF16), pltpu.VMEM((nb, w, blk), BF16),
                        pltpu.VMEM((nh * nb, HEAD_DIM), F32), pltpu.VMEM((nh * nb, blk), F32)],
        compiler_params=_params(("parallel", "arbitrary"), est),
    )(zbc, zbc, zbc, cos_t, sin_t, cos_t, sin_t)


def _sb_kernel(q_ref, k_ref, v_ref, u_ref, o_ref, *, tq, nh, scale):
    i = pl.program_id(1)
    d = HEAD_DIM
    heads = range(nh)
    u = u_ref[...]
    qb = [(q_ref[:, h * d:(h + 1) * d].astype(F32) * scale).astype(BF16) for h in heads]

    def tile(j, carry, strict):
        off = pl.multiple_of(j * tq, tq)
        z = [_nt_dot(qb[h], k_ref[pl.ds(off, tq), h * d:(h + 1) * d]) for h in heads]
        hi, lo = [], []
        for h in heads:
            lsm = -(jnp.maximum(z[h], 0.0) + jnp.log(1.0 + jnp.exp(-jnp.abs(z[h]))))
            if strict is not None:
                lsm = jnp.where(strict, lsm, 0.0)
            hi.append(lsm.astype(BF16))
            lo.append((lsm - hi[h].astype(F32)).astype(BF16))
        incl = [jnp.dot(hi[h], u, preferred_element_type=F32) + jnp.dot(lo[h], u, preferred_element_type=F32)
                for h in heads]
        a = []
        for h in heads:
            ah = jnp.exp(jnp.minimum(z[h] + incl[h], 0.0) + carry[h][0])
            if strict is not None:
                ah = jnp.where(strict, ah, 0.0)
            a.append(ah.astype(BF16))
        return [(carry[h][0] + incl[h][:, 0:1],
                 carry[h][1] + jnp.dot(a[h], v_ref[pl.ds(off, tq), h * d:(h + 1) * d], preferred_element_type=F32))
                for h in heads]

    r_id = lax.broadcasted_iota(jnp.int32, (tq, tq), 0)
    c_id = lax.broadcasted_iota(jnp.int32, (tq, tq), 1)
    init = [(jnp.zeros((tq, 1), F32), jnp.zeros((tq, d), F32)) for _ in heads]
    carry = tile(i, init, c_id < r_id)

    def live(carry):
        worst = carry[0][0]
        for h in heads[1:]:
            worst = jnp.maximum(worst, carry[h][0])
        return (jnp.max(worst) > EXP_UNDERFLOW).astype(jnp.int32)

    def body(state):
        n, _, carry = state
        carry = tile(i - 1 - n, carry, None)
        return n + 1, live(carry), carry

    _, _, carry = lax.while_loop(lambda st: jnp.logical_and(st[0] < i, st[1] > 0), body,
                                 (jnp.int32(0), live(carry), carry))
    for h in heads:
        o_ref[:, h * d:(h + 1) * d] = carry[h][1].astype(o_ref.dtype)


def stick_breaking_attention(zbc, batch, group0, tq=256):
    t = zbc.shape[0]
    s = t // batch
    nq = s // tq
    nh = SB_HEADS
    w = nh * HEAD_DIM
    u = (jnp.arange(tq)[:, None] >= jnp.arange(tq)[None, :]).astype(BF16)
    est = 8 * _nbytes((s, w), BF16) + 16 * nh * _nbytes((tq, tq), F32)
    return pl.pallas_call(
        functools.partial(_sb_kernel, tq=tq, nh=nh, scale=float(HEAD_DIM ** -0.5)),
        grid=(batch, nq),
        in_specs=[pl.BlockSpec((tq, w), lambda b, i: (b * nq + i, group0)),
                  pl.BlockSpec((s, w), lambda b, i: (b, group0 + 1)),
                  pl.BlockSpec((s, w), lambda b, i: (b, group0 + 2)),
                  pl.BlockSpec((tq, tq), lambda b, i: (0, 0))],
        out_specs=pl.BlockSpec((tq, w), lambda b, i: (b * nq + i, 0)),
        out_shape=jax.ShapeDtypeStruct((t, w), BF16),
        compiler_params=_params(("parallel", "arbitrary"), est),
    )(zbc, zbc, zbc, u)


def _merge_kernel(oa_ref, ob_ref, oc_ref, ga_ref, gb_ref, gc_ref, wa_ref, wb_ref, wc_ref, o_ref):
    def branch(o, g, w):
        return jax.nn.sigmoid(g[...].astype(F32)) * jnp.dot(o[...], w[...], preferred_element_type=F32)
    o_ref[...] = (branch(oa_ref, ga_ref, wa_ref) + branch(ob_ref, gb_ref, wb_ref)
                  + branch(oc_ref, gc_ref, wc_ref)).astype(o_ref.dtype)


def gated_merge(o_a, o_b, o_c, zg, wa, wb, wc, tm=512):
    m = o_a.shape[0]
    d = wa.shape[1]
    row = lambda a: pl.BlockSpec((tm, a.shape[1]), lambda i: (i, 0))
    full = lambda a: pl.BlockSpec(a.shape, lambda i: (0, 0))
    gate = lambda n: pl.BlockSpec((tm, d), lambda i: (i, n))
    est = 4 * _nbytes((d, d), BF16) + 12 * _nbytes((tm, d), F32)
    return pl.pallas_call(
        _merge_kernel,
        grid=(m // tm,),
        in_specs=[row(o_a), row(o_b), row(o_c), gate(0), gate(1), gate(2), full(wa), full(wb), full(wc)],
        out_specs=pl.BlockSpec((tm, d), lambda i: (i, 0)),
        out_shape=jax.ShapeDtypeStruct((m, d), BF16),
        compiler_params=_params(("parallel",), est),
    )(o_a, o_b, o_c, zg, zg, zg, wa, wb, wc)


def _xattn_kernel(q_ref, k_ref, v_ref, o_ref, *, heads, scale):
    hd = q_ref.shape[1] // heads
    for h in range(heads):
        cols = slice(h * hd, (h + 1) * hd)
        s = _nt_dot(q_ref[:, cols], k_ref[:, cols]) * scale
        p = jnp.exp(s - jnp.max(s, axis=1, keepdims=True))
        o = jnp.dot(p.astype(BF16), v_ref[:, cols], preferred_element_type=F32)
        o_ref[:, cols] = (o / jnp.sum(p, axis=1, keepdims=True)).astype(o_ref.dtype)


def cross_attention(q, k, v, batch, heads, tq=512):
    t, d = q.shape
    s = t // batch
    mlen = k.shape[0] // batch
    nq = s // tq
    est = 8 * _nbytes((tq, d), BF16) + 8 * _nbytes((mlen, d), BF16) + 8 * _nbytes((tq, mlen), F32)
    return pl.pallas_call(
        functools.partial(_xattn_kernel, heads=heads, scale=float((d // heads) ** -0.5)),
        grid=(batch, nq),
        in_specs=[pl.BlockSpec((tq, d), lambda b, i: (b * nq + i, 0)),
                  pl.BlockSpec((mlen, d), lambda b, i: (b, 0)),
                  pl.BlockSpec((mlen, d), lambda b, i: (b, 0))],
        out_specs=pl.BlockSpec((tq, d), lambda b, i: (b * nq + i, 0)),
        out_shape=jax.ShapeDtypeStruct((t, d), BF16),
        compiler_params=_params(("parallel", "parallel"), est),
    )(q, k, v)


def _swiglu_up(x_ref, w1_ref, w3_ref, o_ref):
    x = x_ref[...]
    a = jnp.dot(x, w1_ref[...], preferred_element_type=F32)
    b = jnp.dot(x, w3_ref[...], preferred_element_type=F32)
    o_ref[...] = (a * jax.nn.sigmoid(a) * b).astype(BF16)


def _ffn_up_kernel(x_ref, w1_ref, w3_ref, o_ref):
    _swiglu_up(x_ref, w1_ref, w3_ref, o_ref)


def dense_ffn(x, w1, w3, w2, res, g_next, tm_up=2048, tf=512, tm_down=256):
    m, d = x.shape
    f = w1.shape[1]
    assert f % tf == 0
    est = 2 * (_nbytes((tm_up, d), BF16) + 2 * _nbytes((d, tf), BF16) + _nbytes((tm_up, tf), BF16)) \
        + 4 * _nbytes((tm_up, tf), F32)
    act = pl.pallas_call(
        _ffn_up_kernel,
        grid=(m // tm_up, f // tf),
        in_specs=[pl.BlockSpec((tm_up, d), lambda i, j: (i, 0)),
                  pl.BlockSpec((d, tf), lambda i, j: (0, j)),
                  pl.BlockSpec((d, tf), lambda i, j: (0, j))],
        out_specs=pl.BlockSpec((tm_up, tf), lambda i, j: (i, j)),
        out_shape=jax.ShapeDtypeStruct((m, f), BF16),
        compiler_params=_params(("parallel", "parallel"), est),
    )(x, w1, w3)
    return matmul_res_norm(act, w2, res, g_next, tm=tm_down)


def _router_kernel(h_ref, g_ref, wr_ref, u_ref, idx_ref, gate_ref, rank_ref, count_ref, seen_scr):
    @pl.when(pl.program_id(0) == 0)
    def _():
        seen_scr[...] = jnp.zeros_like(seen_scr)

    hn = _rms(h_ref[...], g_ref[...])
    logits = lax.dot_general(wr_ref[...], hn, (((1,), (1,)), ((), ())),
                             precision=lax.Precision.HIGHEST, preferred_element_type=F32)
    e_id = lax.broadcasted_iota(jnp.int32, logits.shape, 0)
    n_e = logits.shape[0]
    v1 = jnp.max(logits, axis=0, keepdims=True)
    i1 = jnp.min(jnp.where(logits == v1, e_id, n_e), axis=0, keepdims=True)
    rest = jnp.where(e_id == i1, -jnp.inf, logits)
    v2 = jnp.max(rest, axis=0, keepdims=True)
    i2 = jnp.min(jnp.where(rest == v2, e_id, n_e), axis=0, keepdims=True)
    e2 = jnp.exp(v2 - v1)
    idx_ref[0:1, :] = i1
    idx_ref[1:2, :] = i2
    gate_ref[0:1, :] = 1.0 / (1.0 + e2)
    gate_ref[1:2, :] = e2 / (1.0 + e2)
    pick1 = jnp.where(e_id == i1, 1.0, 0.0)
    pick2 = jnp.where(e_id == i2, 1.0, 0.0)
    both = pick1 + pick2
    earlier = jnp.dot(both.astype(BF16), u_ref[...], preferred_element_type=F32) + seen_scr[...]
    rank_ref[0:1, :] = jnp.sum(pick1 * earlier, axis=0, keepdims=True).astype(jnp.int32)
    rank_ref[1:2, :] = jnp.sum(pick2 * earlier, axis=0, keepdims=True).astype(jnp.int32)
    seen_scr[...] += jnp.sum(both, axis=1, keepdims=True)
    count_ref[...] = seen_scr[...].astype(jnp.int32)


def moe_router(h, g, w_router, tm=512):
    m, d = h.shape
    n_e = w_router.shape[1]
    wr_t = w_router.T.astype(F32)
    u = (jnp.arange(tm)[:, None] < jnp.arange(tm)[None, :]).astype(BF16)
    est = 6 * _nbytes((tm, d), F32)
    return pl.pallas_call(
        _router_kernel,
        grid=(m // tm,),
        in_specs=[pl.BlockSpec((tm, d), lambda i: (i, 0)),
                  pl.BlockSpec((1, d), lambda i: (0, 0)),
                  pl.BlockSpec((n_e, d), lambda i: (0, 0)),
                  pl.BlockSpec((tm, tm), lambda i: (0, 0))],
        out_specs=[pl.BlockSpec((MOE_TOPK, tm), lambda i: (0, i)),
                   pl.BlockSpec((MOE_TOPK, tm), lambda i: (0, i)),
                   pl.BlockSpec((MOE_TOPK, tm), lambda i: (0, i)),
                   pl.BlockSpec((n_e, 1), lambda i: (0, 0))],
        out_shape=[jax.ShapeDtypeStruct((MOE_TOPK, m), jnp.int32), jax.ShapeDtypeStruct((MOE_TOPK, m), F32),
                   jax.ShapeDtypeStruct((MOE_TOPK, m), jnp.int32), jax.ShapeDtypeStruct((n_e, 1), jnp.int32)],
        scratch_shapes=[pltpu.VMEM((n_e, 1), F32)],
        compiler_params=_params(("arbitrary",), est),
    )(h, g.reshape(1, d).astype(F32), wr_t, u)


def _row_copy(src_hbm, src_row, dst_ref, dst_row, sem):
    return pltpu.make_async_copy(src_hbm.at[pl.ds(src_row, 1)], dst_ref.at[pl.ds(dst_row, 1)], sem)


PREFETCH_ROWS = 128


def _expert_up_kernel(te_ref, tv_ref, rows_ref, h_hbm, g_ref, w1_ref, w3_ref, o_ref, xbuf, xn_ref, sem):
    t = pl.program_id(0)
    j = pl.program_id(1)
    tm = MOE_TILE
    slot = lax.rem(t, 2)

    def request(tile, first, count):
        dst, dsem = xbuf.at[lax.rem(tile, 2)], sem.at[lax.rem(tile, 2)]

        def start(r, _):
            _row_copy(h_hbm, rows_ref[tile * tm + first + r], dst, first + r, dsem).start()
            return 0
        lax.fori_loop(0, count, start, 0, unroll=8)

    @pl.when(jnp.logical_and(t == 0, j == 0))
    def _():
        request(0, 0, tm)

    nxt = jnp.minimum(t + 1, pl.num_programs(0) - 1)
    @pl.when(jnp.logical_and(jnp.logical_and(t + 1 < pl.num_programs(0), tv_ref[nxt] > 0),
                             j < tm // PREFETCH_ROWS))
    def _():
        request(t + 1, j * PREFETCH_ROWS, PREFETCH_ROWS)

    @pl.when(jnp.logical_and(j == 0, tv_ref[t] > 0))
    def _():
        def wait(r, _):
            _row_copy(h_hbm, 0, xbuf.at[slot], r, sem.at[slot]).wait()
            return 0
        lax.fori_loop(0, tm, wait, 0, unroll=8)
        xn_ref[...] = _rms(xbuf[slot], g_ref[...]).astype(BF16)

    @pl.when(tv_ref[t] > 0)
    def _():
        _swiglu_up(xn_ref, w1_ref, w3_ref, o_ref)

    @pl.when(tv_ref[t] == 0)
    def _():
        o_ref[...] = jnp.zeros_like(o_ref)


def _expert_down_kernel(te_ref, tv_ref, a_ref, w_ref, o_ref):
    t = pl.program_id(0)

    @pl.when(pl.program_id(1) == 0)
    def _():
        o_ref[...] = jnp.zeros_like(o_ref)

    @pl.when(tv_ref[t] > 0)
    def _():
        o_ref[...] += jnp.dot(a_ref[...], w_ref[...], preferred_element_type=F32)


def expert_ffn(h, slot_tok, g, w1, w3, w2, tile_e, tile_valid, tf=1024, nk=4):
    d = h.shape[1]
    n = slot_tok.shape[0]
    f = w1.shape[2]
    nf = f // tf
    tk = f // nk
    tm = MOE_TILE
    assert tm % PREFETCH_ROWS == 0 and nf >= tm // PREFETCH_ROWS
    assert f % tf == 0 and f % nk == 0 and tk % LANES == 0

    def hold(last):
        return lambda t, j, tv: j * tv[t] + last * (1 - tv[t])

    ju = hold(nf - 1)
    est = 2 * (_nbytes((tm, d), F32) + 2 * _nbytes((d, tf), BF16) + _nbytes((tm, tf), BF16)) \
        + _nbytes((tm, d), BF16) + 4 * _nbytes((tm, tf), F32) + _nbytes((tm, d), F32)
    act = pl.pallas_call(
        _expert_up_kernel,
        grid_spec=pltpu.PrefetchScalarGridSpec(
            num_scalar_prefetch=3,
            grid=(n // tm, nf),
            in_specs=[pl.BlockSpec(memory_space=pl.ANY),
                      pl.BlockSpec((1, d), lambda t, j, te, tv, rows: (0, 0)),
                      pl.BlockSpec((None, d, tf), lambda t, j, te, tv, rows: (te[t], 0, ju(t, j, tv))),
                      pl.BlockSpec((None, d, tf), lambda t, j, te, tv, rows: (te[t], 0, ju(t, j, tv)))],
            out_specs=pl.BlockSpec((tm, tf), lambda t, j, te, tv, rows: (t, j)),
            scratch_shapes=[pltpu.VMEM((2, tm, d), F32), pltpu.VMEM((tm, d), BF16),
                            pltpu.SemaphoreType.DMA((2,))],
        ),
        out_shape=jax.ShapeDtypeStruct((n, f), BF16),
        compiler_params=_params(("arbitrary", "arbitrary"), est),
    )(tile_e, tile_valid, slot_tok, h, g.reshape(1, d).astype(F32), w1, w3)
    jd = hold(nk - 1)
    est = 2 * (_nbytes((tm, tk), BF16) + _nbytes((tk, d), BF16) + _nbytes((tm, d), F32)) + 2 * _nbytes((tm, d), F32)
    return pl.pallas_call(
        _expert_down_kernel,
        grid_spec=pltpu.PrefetchScalarGridSpec(
            num_scalar_prefetch=2,
            grid=(n // tm, nk),
            in_specs=[pl.BlockSpec((tm, tk), lambda t, k, te, tv: (t, jd(t, k, tv))),
                      pl.BlockSpec((None, tk, d), lambda t, k, te, tv: (te[t], jd(t, k, tv), 0))],
            out_specs=pl.BlockSpec((tm, d), lambda t, k, te, tv: (t, 0)),
        ),
        out_shape=jax.ShapeDtypeStruct((n, d), F32),
        compiler_params=_params(("arbitrary", "arbitrary"), est),
    )(tile_e, tile_valid, act, w2)


def _combine_kernel(slots_ref, ys_hbm, h_ref, gate_ref, g_ref, o_ref, buf0, buf1, sem):
    base = pl.program_id(0) * GATHER_ROWS

    def start(r, _):
        flat = (base + r) * MOE_TOPK
        _row_copy(ys_hbm, slots_ref[flat], buf0, r, sem).start()
        _row_copy(ys_hbm, slots_ref[flat + 1], buf1, r, sem).start()
        return 0

    def wait(r, _):
        _row_copy(ys_hbm, 0, buf0, r, sem).wait()
        _row_copy(ys_hbm, 0, buf1, r, sem).wait()
        return 0

    lax.fori_loop(0, GATHER_ROWS, start, 0, unroll=8)
    lax.fori_loop(0, GATHER_ROWS, wait, 0, unroll=8)
    gate = gate_ref[...]
    h = h_ref[...] + gate[:, 0:1] * buf0[...] + gate[:, 1:2] * buf1[...]
    o_ref[...] = _rms(h, g_ref[...])


def moe_combine_norm(ys, slot_of_flat, h, gates, g_out):
    m, d = h.shape
    tm = GATHER_ROWS
    return pl.pallas_call(
        _combine_kernel,
        grid_spec=pltpu.PrefetchScalarGridSpec(
            num_scalar_prefetch=1,
            grid=(m // tm,),
            in_specs=[pl.BlockSpec(memory_space=pl.ANY),
                      pl.BlockSpec((tm, d), lambda i, s: (i, 0)),
                      pl.BlockSpec((tm, MOE_TOPK), lambda i, s: (i, 0)),
                      pl.BlockSpec((1, d), lambda i, s: (0, 0))],
            out_specs=pl.BlockSpec((tm, d), lambda i, s: (i, 0)),
            scratch_shapes=[pltpu.VMEM((tm, d), F32), pltpu.VMEM((tm, d), F32), pltpu.SemaphoreType.DMA(())],
        ),
        out_shape=jax.ShapeDtypeStruct((m, d), F32),
        compiler_params=_params(("arbitrary",), 8 * _nbytes((tm, d), F32)),
    )(slot_of_flat, ys, h, gates, g_out.reshape(1, d).astype(F32))


def moe_routing_metadata(top_idx, top_rank, counts, n_tokens):
    tk = n_tokens * MOE_TOPK
    flat_e = top_idx.T.reshape(-1)
    within = top_rank.T.reshape(-1)
    padded = (counts + MOE_TILE - 1) // MOE_TILE * MOE_TILE
    pad_end = jnp.cumsum(padded)
    pad_start = pad_end - padded
    slot_of_flat = (jnp.sum(jnp.where(flat_e[:, None] == jnp.arange(N_EXPERTS)[None, :], pad_start[None, :], 0),
                            axis=1) + within).astype(jnp.int32)
    n_tiles = tk // MOE_TILE + N_EXPERTS
    slot_tok = jnp.zeros((n_tiles * MOE_TILE,), jnp.int32).at[slot_of_flat].set(
        jnp.arange(tk, dtype=jnp.int32) // MOE_TOPK)
    tile_start = jnp.arange(n_tiles, dtype=jnp.int32) * MOE_TILE
    tile_valid = (tile_start < pad_end[-1]).astype(jnp.int32)
    tile_e = jnp.minimum(jnp.sum(tile_start[:, None] >= pad_end[None, :], axis=1), N_EXPERTS - 1)
    last_e = jnp.max(jnp.where(tile_valid > 0, tile_e, 0))
    tile_e = jnp.where(tile_valid > 0, tile_e, last_e).astype(jnp.int32)
    return slot_tok, slot_of_flat, tile_e, tile_valid


def _rope_partner(w):
    half = w.shape[-1] // 2
    return jnp.concatenate([-w[..., half:], w[..., :half]], axis=-1)


def _rope_tables(positions):
    pos = positions.astype(F32).reshape(-1, 1)

    def cs(half):
        inv_freq = ROPE_THETA ** (-jnp.arange(half, dtype=F32) / half)
        ang = pos * inv_freq
        return jnp.cos(ang), jnp.sin(ang)

    c32, s32 = cs(MLA_ROPE // 2)
    zeros = jnp.zeros((pos.shape[0], LANES // 2), F32)
    mla_c = jnp.concatenate([c32, c32, zeros], axis=1)
    mla_s = jnp.concatenate([s32, s32, zeros], axis=1)
    c64, s64 = cs(HEAD_DIM // 2)
    moba_c = jnp.concatenate([c64, c64], axis=1)
    moba_s = jnp.concatenate([-s64, s64], axis=1)
    return mla_c, mla_s, moba_c, moba_s


def _layer_weights(w_in, w_uq, w_ukv):
    o = 0
    w_cq = w_in[:, o:o + MLA_LORA]; o += MLA_LORA
    w_ckv = w_in[:, o:o + MLA_LORA]; o += MLA_LORA
    w_kr = w_in[:, o:o + MLA_ROPE]; o += MLA_ROPE
    nbc = 3 * (MOBA_HEADS + SB_HEADS) * HEAD_DIM
    w_bc = w_in[:, o:o + nbc]; o += nbc
    w_g = w_in[:, o:]
    w1 = jnp.concatenate([w_cq, w_ckv, w_kr, _rope_partner(w_kr)], axis=1)
    uq = w_uq.reshape(MLA_LORA, MLA_HEADS, MLA_NOPE + MLA_ROPE)
    uq_rope = uq[..., MLA_NOPE:]
    wq = jnp.concatenate([uq[..., :MLA_NOPE], uq_rope, _rope_partner(uq_rope)], axis=-1)
    wq = wq.reshape(MLA_LORA, MLA_HEADS * 2 * LANES)
    ukv = w_ukv.reshape(MLA_LORA, MLA_HEADS, MLA_NOPE + MLA_V)
    wkv = jnp.concatenate([ukv[..., :MLA_NOPE].reshape(MLA_LORA, -1), ukv[..., MLA_NOPE:].reshape(MLA_LORA, -1)],
                          axis=1)
    return w1, w_bc, w_g, wq, wkv


def kernel(x, mem, positions, g_mix, w_in, g_cq, g_ckv, w_uq, w_ukv, w_up_a, w_up_b, w_up_c, w_o, g_x, g_mem, w_xq, w_xk, w_xv, w_xo, g_ffn, w_ff1, w_ff3, w_ff2, w_router, w_e1, w_e3, w_e2, g_final):
    batch, seq, d = x.shape
    t = batch * seq
    depth = g_mix.shape[0]
    assert depth == 2, "the final norm is fused into the expert layer, which must come last"
    mem2 = mem.reshape(-1, d)
    mla_c, mla_s, moba_c, moba_s = _rope_tables(positions)
    assert MOBA_HEADS == SB_HEADS

    w_in, w_uq, w_ukv = w_in.astype(BF16), w_uq.astype(BF16), w_ukv.astype(BF16)

    h = x.reshape(t, d)
    hn = rmsnorm(h, g_mix[0], BF16)
    out = None
    for l in range(depth):
        w1, w_bc, w_g, wq, wkv = _layer_weights(w_in[l], w_uq[l], w_ukv[l])
        z1 = matmul(hn, w1, BF16, tn=w1.shape[1])
        zbc = matmul(hn, w_bc, BF16, tn=1280)
        zg = matmul(hn, w_g, BF16, tm=2048, tn=1024)
        q_a, k_a, v_a = mla_prep(z1, g_cq[l], g_ckv[l], wq, wkv, mla_c, mla_s)
        o_a = causal_attention(q_a, k_a, v_a, batch, MLA_HEADS, 2 * LANES, MLA_V)
        o_b = moba_attention(zbc, moba_c, moba_s, batch, 0)
        o_c = stick_breaking_attention(zbc, batch, 3)
        merged = gated_merge(o_a, o_b, o_c, zg, w_up_a[l].astype(BF16), w_up_b[l].astype(BF16),
                             w_up_c[l].astype(BF16))
        h, hn = matmul_res_norm(merged, w_o[l].astype(BF16), h, g_x[l])
        mn = rmsnorm(mem2, g_mem[l], BF16)
        q_x = matmul(hn, w_xq[l].astype(BF16), BF16, tm=2048)
        k_x = matmul(mn, w_xk[l].astype(BF16), BF16)
        v_x = matmul(mn, w_xv[l].astype(BF16), BF16)
        o_x = cross_attention(q_x, k_x, v_x, batch, XATTN_HEADS)
        h, hn = matmul_res_norm(o_x, w_xo[l].astype(BF16), h, g_ffn[l])
        g_next = g_mix[l + 1] if l + 1 < depth else g_final
        if l % 2 == 0:
            e = l // 2
            h, hn = dense_ffn(hn, w_ff1[e].astype(BF16), w_ff3[e].astype(BF16), w_ff2[e].astype(BF16), h, g_next)
            out = hn
        else:
            e = l // 2
            top_idx, top_gate, top_rank, counts = moe_router(h, g_ffn[l], w_router[e])
            slot_tok, slot_of_flat, tile_e, tile_valid = moe_routing_metadata(top_idx, top_rank, counts[:, 0], t)
            ys = expert_ffn(h, slot_tok, g_ffn[l],w_e1[e].astype(BF16), w_e3[e].astype(BF16), w_e2[e].astype(BF16),
                            tile_e, tile_valid)
            out = moe_combine_norm(ys, slot_of_flat, h, top_gate.T, g_next)
    return out.reshape(batch, seq, d).astype(x.dtype)
```

```python
import functools

import numpy as np
import jax
import jax.numpy as jnp
from jax import lax
from jax.experimental import pallas as pl
from jax.experimental.pallas import tpu as pltpu

F32 = jnp.float32
BF16 = jnp.bfloat16

HEAD_DIM = 128
MLA_HEADS = 6
MLA_LORA = 512
MLA_NOPE = 128
MLA_ROPE = 64
MLA_V = 128
MOBA_HEADS = 5
MOBA_BLOCK = 256
MOBA_TOPK = 3
SB_HEADS = 5
N_BRANCH = 3
ROPE_THETA = 10000.0
XATTN_HEADS = 4
N_EXPERTS = 8
MOE_TOPK = 2
RMS_EPS = 1e-6

LANES = 128
V7X_VMEM_BYTES = 64 * 1024 * 1024
VMEM_CEILING = V7X_VMEM_BYTES - 8 * 1024 * 1024

MASK_VALUE = -1e30
EXP_UNDERFLOW = -104.0

MOE_TILE = 512
GATHER_ROWS = 256


def _params(sem, est_bytes):
    limit = int(min(VMEM_CEILING, max(32 * 1024 * 1024, est_bytes * 5 // 4)))
    return pltpu.CompilerParams(dimension_semantics=sem, vmem_limit_bytes=limit)


def _nbytes(shape, dtype):
    return int(np.prod(shape)) * jnp.dtype(dtype).itemsize


def _rms(x, g):
    return x * lax.rsqrt(jnp.mean(x * x, axis=-1, keepdims=True) + RMS_EPS) * g


def _rmsnorm_kernel(x_ref, g_ref, o_ref):
    o_ref[...] = _rms(x_ref[...].astype(F32), g_ref[...]).astype(o_ref.dtype)


def rmsnorm(x, g, out_dtype, tm=512):
    m, d = x.shape
    tm = min(tm, m)
    return pl.pallas_call(
        _rmsnorm_kernel,
        grid=(m // tm,),
        in_specs=[pl.BlockSpec((tm, d), lambda i: (i, 0)),
                  pl.BlockSpec((1, d), lambda i: (0, 0))],
        out_specs=pl.BlockSpec((tm, d), lambda i: (i, 0)),
        out_shape=jax.ShapeDtypeStruct((m, d), out_dtype),
        compiler_params=_params(("parallel",), 4 * _nbytes((tm, d), F32)),
    )(x, g.reshape(1, d).astype(F32))


def _mm_kernel(a_ref, w_ref, o_ref):
    o_ref[...] = jnp.dot(a_ref[...], w_ref[...], preferred_element_type=F32).astype(o_ref.dtype)


def matmul(a, w, out_dtype, tm=1024, tn=1024, cols=None):
    m, k = a.shape
    c0, c1 = cols if cols is not None else (0, w.shape[1])
    n = c1 - c0
    tm, tn = min(tm, m), min(tn, n)
    assert m % tm == 0 and n % tn == 0 and c0 % tn == 0, (m, n, tm, tn, c0)
    j0 = c0 // tn
    est = 2 * (_nbytes((tm, k), a.dtype) + _nbytes((k, tn), w.dtype) + _nbytes((tm, tn), out_dtype)) \
        + _nbytes((tm, tn), F32)
    return pl.pallas_call(
        _mm_kernel,
        grid=(m // tm, n // tn),
        in_specs=[pl.BlockSpec((tm, k), lambda i, j: (i, 0)),
                  pl.BlockSpec((k, tn), lambda i, j: (0, j0 + j))],
        out_specs=pl.BlockSpec((tm, tn), lambda i, j: (i, j)),
        out_shape=jax.ShapeDtypeStruct((m, n), out_dtype),
        compiler_params=_params(("parallel", "parallel"), est),
    )(a, w)


def _mm_res_norm_kernel(a_ref, w_ref, r_ref, g_ref, h_ref, hn_ref):
    h = r_ref[...] + jnp.dot(a_ref[...], w_ref[...], preferred_element_type=F32)
    h_ref[...] = h
    hn_ref[...] = _rms(h, g_ref[...]).astype(hn_ref.dtype)


def matmul_res_norm(a, w, res, g, tm=512):
    m, k = a.shape
    n = w.shape[1]
    est = 2 * (_nbytes((tm, k), a.dtype) + 2 * _nbytes((tm, n), F32) + _nbytes((tm, n), BF16)) \
        + _nbytes((k, n), w.dtype) + 2 * _nbytes((tm, n), F32)
    return pl.pallas_call(
        _mm_res_norm_kernel,
        grid=(m // tm,),
        in_specs=[pl.BlockSpec((tm, k), lambda i: (i, 0)),
                  pl.BlockSpec((k, n), lambda i: (0, 0), pipeline_mode=pl.Buffered(1)),
                  pl.BlockSpec((tm, n), lambda i: (i, 0)),
                  pl.BlockSpec((1, n), lambda i: (0, 0))],
        out_specs=[pl.BlockSpec((tm, n), lambda i: (i, 0)),
                   pl.BlockSpec((tm, n), lambda i: (i, 0))],
        out_shape=[jax.ShapeDtypeStruct((m, n), F32), jax.ShapeDtypeStruct((m, n), BF16)],
        compiler_params=_params(("parallel",), est),
    )(a, w, res, g.reshape(1, n).astype(F32))


def _half_swap(y, c, s):
    return y * c + pltpu.roll(y, LANES // 2, 1) * s


def _mla_prep_kernel(z_ref, gq_ref, gkv_ref, wq_ref, wkv_ref, c_ref, s_ref, q_ref, k_ref, v_ref, *, scale):
    z = z_ref[...].astype(F32)
    c = c_ref[...]
    s = s_ref[...]
    nq = _rms(z[:, :MLA_LORA], gq_ref[...]).astype(BF16)
    nkv = _rms(z[:, MLA_LORA:2 * MLA_LORA], gkv_ref[...]).astype(BF16)
    q = jnp.dot(nq, wq_ref[...], preferred_element_type=F32)
    kv = jnp.dot(nkv, wkv_ref[...], preferred_element_type=F32)
    k_pe = _half_swap(z[:, 2 * MLA_LORA:], c, s).astype(BF16)
    for h in range(MLA_HEADS):
        lo = 2 * LANES * h
        q_ref[:, lo:lo + LANES] = (q[:, lo:lo + LANES] * scale).astype(BF16)
        q_ref[:, lo + LANES:lo + 2 * LANES] = (_half_swap(q[:, lo + LANES:lo + 2 * LANES], c, s) * scale).astype(BF16)
        k_ref[:, lo:lo + LANES] = kv[:, LANES * h:LANES * (h + 1)].astype(BF16)
        k_ref[:, lo + LANES:lo + 2 * LANES] = k_pe
    v_ref[...] = kv[:, MLA_HEADS * MLA_NOPE:].astype(BF16)


def mla_prep(z1, g_cq, g_ckv, wq, wkv, cos_t, sin_t, tm=512):
    m, zc = z1.shape
    nq, nkv = wq.shape[1], wkv.shape[1]
    scale = float((MLA_NOPE + MLA_ROPE) ** -0.5)
    row = lambda c: pl.BlockSpec((tm, c), lambda i: (i, 0))
    full = lambda a: pl.BlockSpec(a.shape, lambda i: (0, 0))
    g_cq = g_cq.reshape(1, -1).astype(F32)
    g_ckv = g_ckv.reshape(1, -1).astype(F32)
    est = 4 * _nbytes((tm, nq), F32) + 4 * (_nbytes(wq.shape, BF16) + _nbytes(wkv.shape, BF16))
    return pl.pallas_call(
        functools.partial(_mla_prep_kernel, scale=scale),
        grid=(m // tm,),
        in_specs=[row(zc), full(g_cq), full(g_ckv), full(wq), full(wkv), row(LANES), row(LANES)],
        out_specs=[row(nq), row(nq), row(MLA_HEADS * MLA_V)],
        out_shape=[jax.ShapeDtypeStruct((m, nq), BF16), jax.ShapeDtypeStruct((m, nq), BF16),
                   jax.ShapeDtypeStruct((m, MLA_HEADS * MLA_V), BF16)],
        compiler_params=_params(("parallel",), est),
    )(z1, g_cq, g_ckv, wq, wkv, cos_t, sin_t)


def _nt_dot(a, b):
    return lax.dot_general(a, b, (((1,), (1,)), ((), ())), preferred_element_type=F32)


def _softmax_tiles_keymajor(scores, load_vt, carry):
    p, stats = [], []
    for s, (m, l, _) in zip(scores, carry):
        m_new = jnp.maximum(m, jnp.max(s, axis=0, keepdims=True))
        alpha = jnp.exp(m - m_new)
        ph = jnp.exp(s - m_new)
        stats.append((m_new, alpha, alpha * l + jnp.sum(ph, axis=0, keepdims=True)))
        p.append(ph.astype(BF16))
    out = []
    for h, ((m_new, alpha, l_new), (_, _, acc)) in enumerate(zip(stats, carry)):
        acc = alpha * acc
        for n, vt in enumerate(load_vt(h)):
            acc = acc + jnp.dot(vt, p[h][n * vt.shape[1]:(n + 1) * vt.shape[1], :], preferred_element_type=F32)
        out.append((m_new, l_new, acc))
    return out


def _softmax_init_keymajor(queries, dv):
    return (jnp.full((1, queries), MASK_VALUE, F32), jnp.zeros((1, queries), F32), jnp.zeros((dv, queries), F32))


def _transposed_bf16(x):
    return x.astype(F32).T.astype(BF16)


def _softmax_tiles(scores, load_v, carry):
    p, stats = [], []
    for s, (m, l, _) in zip(scores, carry):
        m_new = jnp.maximum(m, jnp.max(s, axis=1, keepdims=True))
        alpha = jnp.exp(m - m_new)
        ph = jnp.exp(s - m_new)
        stats.append((m_new, alpha, alpha * l + jnp.sum(ph, axis=1, keepdims=True)))
        p.append(ph.astype(BF16))
    return [(m_new, l_new, alpha * acc + jnp.dot(p[h], load_v(h), preferred_element_type=F32))
            for h, ((m_new, alpha, l_new), (_, _, acc)) in enumerate(zip(stats, carry))]


def _softmax_init(rows, dv):
    return (jnp.full((rows, 1), MASK_VALUE, F32), jnp.zeros((rows, 1), F32), jnp.zeros((rows, dv), F32))


def _causal_attn_kernel(q_ref, k_ref, v_ref, o_ref, *, tq, tk, dk, dv, hpg):
    i = pl.program_id(2)
    n_full = (i * tq) // tk
    heads = range(hpg)
    q = [q_ref[:, h * dk:(h + 1) * dk] for h in heads]

    def tile(j, carry, mask):
        off = pl.multiple_of(j * tk, tk)
        s = [_nt_dot(q[h], k_ref[pl.ds(off, tk), h * dk:(h + 1) * dk]) for h in heads]
        if mask is not None:
            s = [jnp.where(mask, sh, MASK_VALUE) for sh in s]
        return _softmax_tiles(s, lambda h: v_ref[pl.ds(off, tk), h * dv:(h + 1) * dv], carry)

    carry = lax.fori_loop(0, n_full, lambda j, c: tile(j, c, None), [_softmax_init(tq, dv) for _ in heads])
    qpos = i * tq + lax.broadcasted_iota(jnp.int32, (tq, tk), 0)
    kpos = n_full * tk + lax.broadcasted_iota(jnp.int32, (tq, tk), 1)
    carry = tile(n_full, carry, kpos <= qpos)
    for h in heads:
        _, l, acc = carry[h]
        o_ref[:, h * dv:(h + 1) * dv] = (acc / l).astype(o_ref.dtype)


def causal_attention(q, k, v, batch, heads, dk, dv, tq=256, tk=1024, hpg=3):
    t = q.shape[0]
    s = t // batch
    nq = s // tq
    assert tk % tq == 0 and s % tk == 0 and heads % hpg == 0
    est = 4 * hpg * (_nbytes((s, dk), BF16) + _nbytes((s, dv), BF16)) + 8 * hpg * _nbytes((tq, tk), F32)
    return pl.pallas_call(
        functools.partial(_causal_attn_kernel, tq=tq, tk=tk, dk=dk, dv=dv, hpg=hpg),
        grid=(batch, heads // hpg, nq),
        in_specs=[pl.BlockSpec((tq, hpg * dk), lambda b, g, i: (b * nq + i, g)),
                  pl.BlockSpec((s, hpg * dk), lambda b, g, i: (b, g)),
                  pl.BlockSpec((s, hpg * dv), lambda b, g, i: (b, g))],
        out_specs=pl.BlockSpec((tq, hpg * dv), lambda b, g, i: (b * nq + i, g)),
        out_shape=jax.ShapeDtypeStruct((t, heads * dv), BF16),
        compiler_params=_params(("parallel", "parallel", "arbitrary"), est),
    )(q, k, v)


def _moba_select(gate_t, i, nb):
    nq = gate_t.shape[1]
    row = lax.broadcasted_iota(jnp.int32, (nb, nq), 0)
    rank = jnp.zeros((nb, nq), jnp.int32)
    for jj in range(nb):
        gj = gate_t[jj:jj + 1, :]
        ahead = jnp.logical_or(gj > gate_t, jnp.logical_and(gj == gate_t, jj < row))
        rank = rank + jnp.where(jnp.logical_and(ahead, jj < i), 1, 0)
    return jnp.where(jnp.logical_and(row < i, rank < MOBA_TOPK), 1.0, 0.0)


def _moba_kernel(q_ref, k_ref, v_ref, cq_ref, sq_ref, ck_ref, sk_ref, o_ref, kr_scr, vt_scr, km_scr, sel_scr,
                 *, nb, nh, scale):
    i = pl.program_id(1)
    blk = MOBA_BLOCK
    d = HEAD_DIM
    heads = range(nh)

    @pl.when(i == 0)
    def _():
        def prep(j, _):
            rows = pl.ds(pl.multiple_of(j * blk, blk), blk)
            c, s = ck_ref[rows, :], sk_ref[rows, :]
            for h in heads:
                kj = _half_swap(k_ref[rows, h * d:(h + 1) * d].astype(F32), c, s)
                km_scr[pl.ds(h * nb + j, 1), :] = jnp.mean(kj, axis=0, keepdims=True)
                kr_scr[rows, h * d:(h + 1) * d] = kj.astype(BF16)
                vt_scr[j, h * d:(h + 1) * d, :] = _transposed_bf16(v_ref[rows, h * d:(h + 1) * d])
            return 0
        lax.fori_loop(0, nb, prep, 0)

    cq, sq = cq_ref[...], sq_ref[...]
    qb = []
    for h in heads:
        q = _half_swap(q_ref[:, h * d:(h + 1) * d].astype(F32), cq, sq)
        gate_t = lax.dot_general(km_scr[h * nb:(h + 1) * nb, :], q, (((1,), (1,)), ((), ())),
                                 precision=lax.Precision.HIGHEST, preferred_element_type=F32)
        sel_scr[h * nb:(h + 1) * nb, :] = _moba_select(gate_t, i, nb)
        qb.append((q * scale).astype(BF16))

    def tile(j, nblk, carry, bias_fn):
        rows = pl.ds(pl.multiple_of(j * blk, blk), nblk * blk)
        s = [_nt_dot(kr_scr[rows, h * d:(h + 1) * d], qb[h]) + bias_fn(h) for h in heads]
        return _softmax_tiles_keymajor(
            s, lambda h: [vt_scr[j + n, h * d:(h + 1) * d, :] for n in range(nblk)], carry)

    key_id = lax.broadcasted_iota(jnp.int32, (blk, blk), 0)
    qry_id = lax.broadcasted_iota(jnp.int32, (blk, blk), 1)
    causal = jnp.where(key_id <= qry_id, 0.0, MASK_VALUE)
    carry = tile(i, 1, [_softmax_init_keymajor(blk, d) for _ in heads], lambda h: causal)

    def body(p, carry):
        def bias(h):
            rows = [jnp.broadcast_to((1.0 - sel_scr[pl.ds(h * nb + 2 * p + n, 1), :]) * MASK_VALUE, (blk, blk))
                    for n in range(2)]
            return jnp.concatenate(rows, axis=0)
        return tile(2 * p, 2, carry, bias)

    carry = lax.fori_loop(0, (i + 1) // 2, body, carry)
    for h in heads:
        _, l, acc = carry[h]
        o_ref[:, h * d:(h + 1) * d] = (acc / l).T.astype(o_ref.dtype)


def moba_attention(zbc, cos_t, sin_t, batch, group0):
    t = zbc.shape[0]
    s = t // batch
    blk = MOBA_BLOCK
    nb = s // blk
    nh = MOBA_HEADS
    w = nh * HEAD_DIM
    est = 6 * _nbytes((s, w), BF16) + 4 * _nbytes((s, HEAD_DIM), F32) + 12 * nh * _nbytes((blk, blk), F32)
    return pl.pallas_call(
        functools.partial(_moba_kernel, nb=nb, nh=nh, scale=float(HEAD_DIM ** -0.5)),
        grid=(batch, nb),
        in_specs=[pl.BlockSpec((blk, w), lambda b, i: (b * nb + i, group0)),
                  pl.BlockSpec((s, w), lambda b, i: (b, group0 + 1)),
                  pl.BlockSpec((s, w), lambda b, i: (b, group0 + 2)),
                  pl.BlockSpec((blk, HEAD_DIM), lambda b, i: (b * nb + i, 0)),
                  pl.BlockSpec((blk, HEAD_DIM), lambda b, i: (b * nb + i, 0)),
                  pl.BlockSpec((s, HEAD_DIM), lambda b, i: (b, 0)),
                  pl.BlockSpec((s, HEAD_DIM), lambda b, i: (b, 0))],
        out_specs=pl.BlockSpec((blk, w), lambda b, i: (b * nb + i, 0)),
        out_shape=jax.ShapeDtypeStruct((t, w), BF16),
        scratch_shapes=[pltpu.VMEM((s, w), BF16), pltpu.VMEM((nb, w, blk), BF16),
                        pltpu.VMEM((nh * nb, HEAD_DIM), F32), pltpu.VMEM((nh * nb, blk), F32)],
        compiler_params=_params(("parallel", "arbitrary"), est),
    )(zbc, zbc, zbc, cos_t, sin_t, cos_t, sin_t)


def _sb_kernel(q_ref, k_ref, v_ref, u_ref, o_ref, *, tq, nh, scale):
    i = pl.program_id(1)
    d = HEAD_DIM
    heads = range(nh)
    u = u_ref[...]
    qb = [(q_ref[:, h * d:(h + 1) * d].astype(F32) * scale).astype(BF16) for h in heads]

    def tile(j, carry, strict):
        off = pl.multiple_of(j * tq, tq)
        z = [_nt_dot(qb[h], k_ref[pl.ds(off, tq), h * d:(h + 1) * d]) for h in heads]
        hi, lo = [], []
        for h in heads:
            lsm = -(jnp.maximum(z[h], 0.0) + jnp.log(1.0 + jnp.exp(-jnp.abs(z[h]))))
            if strict is not None:
                lsm = jnp.where(strict, lsm, 0.0)
            lsm_hi, lsm_lo = _split_bf16(lsm)
            hi.append(lsm_hi)
            lo.append(lsm_lo)
        incl = [jnp.dot(hi[h], u, preferred_element_type=F32) + jnp.dot(lo[h], u, preferred_element_type=F32)
                for h in heads]
        a = []
        for h in heads:
            ah = jnp.exp(jnp.minimum(z[h] + incl[h], 0.0) + carry[h][0])
            if strict is not None:
                ah = jnp.where(strict, ah, 0.0)
            a.append(ah.astype(BF16))
        return [(carry[h][0] + incl[h][:, 0:1],
                 carry[h][1] + jnp.dot(a[h], v_ref[pl.ds(off, tq), h * d:(h + 1) * d], preferred_element_type=F32))
                for h in heads]

    r_id = lax.broadcasted_iota(jnp.int32, (tq, tq), 0)
    c_id = lax.broadcasted_iota(jnp.int32, (tq, tq), 1)
    init = [(jnp.zeros((tq, 1), F32), jnp.zeros((tq, d), F32)) for _ in heads]
    carry = tile(i, init, c_id < r_id)

    def live(carry):
        worst = carry[0][0]
        for h in heads[1:]:
            worst = jnp.maximum(worst, carry[h][0])
        return (jnp.max(worst) > EXP_UNDERFLOW).astype(jnp.int32)

    def body(state):
        n, _, carry = state
        carry = tile(i - 1 - n, carry, None)
        return n + 1, live(carry), carry

    _, _, carry = lax.while_loop(lambda st: jnp.logical_and(st[0] < i, st[1] > 0), body,
                                 (jnp.int32(0), live(carry), carry))
    for h in heads:
        o_ref[:, h * d:(h + 1) * d] = carry[h][1].astype(o_ref.dtype)


def stick_breaking_attention(zbc, batch, group0, tq=256):
    t = zbc.shape[0]
    s = t // batch
    nq = s // tq
    nh = SB_HEADS
    w = nh * HEAD_DIM
    u = (jnp.arange(tq)[:, None] >= jnp.arange(tq)[None, :]).astype(BF16)
    est = 8 * _nbytes((s, w), BF16) + 16 * nh * _nbytes((tq, tq), F32)
    return pl.pallas_call(
        functools.partial(_sb_kernel, tq=tq, nh=nh, scale=float(HEAD_DIM ** -0.5)),
        grid=(batch, nq),
        in_specs=[pl.BlockSpec((tq, w), lambda b, i: (b * nq + i, group0)),
                  pl.BlockSpec((s, w), lambda b, i: (b, group0 + 1)),
                  pl.BlockSpec((s, w), lambda b, i: (b, group0 + 2)),
                  pl.BlockSpec((tq, tq), lambda b, i: (0, 0))],
        out_specs=pl.BlockSpec((tq, w), lambda b, i: (b * nq + i, 0)),
        out_shape=jax.ShapeDtypeStruct((t, w), BF16),
        compiler_params=_params(("parallel", "arbitrary"), est),
    )(zbc, zbc, zbc, u)


def _merge_kernel(oa_ref, ob_ref, oc_ref, ga_ref, gb_ref, gc_ref, wa_ref, wb_ref, wc_ref, o_ref):
    def branch(o, g, w):
        return jax.nn.sigmoid(g[...].astype(F32)) * jnp.dot(o[...], w[...], preferred_element_type=F32)
    o_ref[...] = (branch(oa_ref, ga_ref, wa_ref) + branch(ob_ref, gb_ref, wb_ref)
                  + branch(oc_ref, gc_ref, wc_ref)).astype(o_ref.dtype)


def gated_merge(o_a, o_b, o_c, zg, wa, wb, wc, tm=512):
    m = o_a.shape[0]
    d = wa.shape[1]
    row = lambda a: pl.BlockSpec((tm, a.shape[1]), lambda i: (i, 0))
    full = lambda a: pl.BlockSpec(a.shape, lambda i: (0, 0))
    gate = lambda n: pl.BlockSpec((tm, d), lambda i: (i, n))
    est = 4 * _nbytes((d, d), BF16) + 12 * _nbytes((tm, d), F32)
    return pl.pallas_call(
        _merge_kernel,
        grid=(m // tm,),
        in_specs=[row(o_a), row(o_b), row(o_c), gate(0), gate(1), gate(2), full(wa), full(wb), full(wc)],
        out_specs=pl.BlockSpec((tm, d), lambda i: (i, 0)),
        out_shape=jax.ShapeDtypeStruct((m, d), BF16),
        compiler_params=_params(("parallel",), est),
    )(o_a, o_b, o_c, zg, zg, zg, wa, wb, wc)


def _xattn_kernel(q_ref, k_ref, v_ref, o_ref, *, heads, scale):
    hd = q_ref.shape[1] // heads
    for h in range(heads):
        cols = slice(h * hd, (h + 1) * hd)
        s = _nt_dot(q_ref[:, cols], k_ref[:, cols]) * scale
        p = jnp.exp(s - jnp.max(s, axis=1, keepdims=True))
        o = jnp.dot(p.astype(BF16), v_ref[:, cols], preferred_element_type=F32)
        o_ref[:, cols] = (o / jnp.sum(p, axis=1, keepdims=True)).astype(o_ref.dtype)


def cross_attention(q, k, v, batch, heads, tq=512):
    t, d = q.shape
    s = t // batch
    mlen = k.shape[0] // batch
    nq = s // tq
    est = 8 * _nbytes((tq, d), BF16) + 8 * _nbytes((mlen, d), BF16) + 8 * _nbytes((tq, mlen), F32)
    return pl.pallas_call(
        functools.partial(_xattn_kernel, heads=heads, scale=float((d // heads) ** -0.5)),
        grid=(batch, nq),
        in_specs=[pl.BlockSpec((tq, d), lambda b, i: (b * nq + i, 0)),
                  pl.BlockSpec((mlen, d), lambda b, i: (b, 0)),
                  pl.BlockSpec((mlen, d), lambda b, i: (b, 0))],
        out_specs=pl.BlockSpec((tq, d), lambda b, i: (b * nq + i, 0)),
        out_shape=jax.ShapeDtypeStruct((t, d), BF16),
        compiler_params=_params(("parallel", "parallel"), est),
    )(q, k, v)


def _swiglu_up(x_ref, w1_ref, w3_ref, o_ref):
    x = x_ref[...]
    a = jnp.dot(x, w1_ref[...], preferred_element_type=F32)
    b = jnp.dot(x, w3_ref[...], preferred_element_type=F32)
    o_ref[...] = (a * jax.nn.sigmoid(a) * b).astype(BF16)


def _ffn_up_kernel(x_ref, w1_ref, w3_ref, o_ref):
    _swiglu_up(x_ref, w1_ref, w3_ref, o_ref)


def dense_ffn(x, w1, w3, w2, res, g_next, tm_up=1024, tf=512, tm_down=256):
    m, d = x.shape
    f = w1.shape[1]
    assert f % tf == 0
    est = 2 * (_nbytes((tm_up, d), BF16) + 2 * _nbytes((d, tf), BF16) + _nbytes((tm_up, tf), BF16)) \
        + 4 * _nbytes((tm_up, tf), F32)
    act = pl.pallas_call(
        _ffn_up_kernel,
        grid=(m // tm_up, f // tf),
        in_specs=[pl.BlockSpec((tm_up, d), lambda i, j: (i, 0)),
                  pl.BlockSpec((d, tf), lambda i, j: (0, j)),
                  pl.BlockSpec((d, tf), lambda i, j: (0, j))],
        out_specs=pl.BlockSpec((tm_up, tf), lambda i, j: (i, j)),
        out_shape=jax.ShapeDtypeStruct((m, f), BF16),
        compiler_params=_params(("parallel", "parallel"), est),
    )(x, w1, w3)
    return matmul_res_norm(act, w2, res, g_next, tm=tm_down)


def _split_bf16(x):
    hi = x.astype(BF16)
    return hi, (x - hi.astype(F32)).astype(BF16)


def _router_kernel(h_ref, g_ref, wh_ref, wl_ref, u_ref, idx_ref, gate_ref, rank_ref, count_ref, seen_scr):
    @pl.when(pl.program_id(0) == 0)
    def _():
        seen_scr[...] = jnp.zeros_like(seen_scr)

    hn = _rms(h_ref[...], g_ref[...])
    n_e = count_ref.shape[0]
    hn_hi, hn_lo = _split_bf16(hn)
    logits = (_nt_dot(hn_hi, wh_ref[...]) + _nt_dot(hn_lo, wh_ref[...]) + _nt_dot(hn_hi, wl_ref[...])).T[:n_e, :]
    e_id = lax.broadcasted_iota(jnp.int32, logits.shape, 0)
    v1 = jnp.max(logits, axis=0, keepdims=True)
    i1 = jnp.min(jnp.where(logits == v1, e_id, n_e), axis=0, keepdims=True)
    rest = jnp.where(e_id == i1, -jnp.inf, logits)
    v2 = jnp.max(rest, axis=0, keepdims=True)
    i2 = jnp.min(jnp.where(rest == v2, e_id, n_e), axis=0, keepdims=True)
    e2 = jnp.exp(v2 - v1)
    idx_ref[0:1, :] = i1
    idx_ref[1:2, :] = i2
    gate_ref[0:1, :] = 1.0 / (1.0 + e2)
    gate_ref[1:2, :] = e2 / (1.0 + e2)
    pick1 = jnp.where(e_id == i1, 1.0, 0.0)
    pick2 = jnp.where(e_id == i2, 1.0, 0.0)
    both = pick1 + pick2
    earlier = jnp.dot(both.astype(BF16), u_ref[...], preferred_element_type=F32) + seen_scr[...]
    rank_ref[0:1, :] = jnp.sum(pick1 * earlier, axis=0, keepdims=True).astype(jnp.int32)
    rank_ref[1:2, :] = jnp.sum(pick2 * earlier, axis=0, keepdims=True).astype(jnp.int32)
    seen_scr[...] += jnp.sum(both, axis=1, keepdims=True)
    count_ref[...] = seen_scr[...].astype(jnp.int32)


def moe_router(h, g, w_router, tm=512):
    m, d = h.shape
    n_e = w_router.shape[1]
    wr_t = jnp.zeros((LANES, d), F32).at[:n_e].set(w_router.T.astype(F32))
    wr_hi, wr_lo = _split_bf16(wr_t)
    u =(jnp.arange(tm)[:, None] < jnp.arange(tm)[None, :]).astype(BF16)
    est = 6 * _nbytes((tm, d), F32)
    return pl.pallas_call(
        _router_kernel,
        grid=(m // tm,),
        in_specs=[pl.BlockSpec((tm, d), lambda i: (i, 0)),
                  pl.BlockSpec((1, d), lambda i: (0, 0)),
                  pl.BlockSpec((LANES, d), lambda i: (0, 0)),
                  pl.BlockSpec((LANES, d), lambda i: (0, 0)),
                  pl.BlockSpec((tm, tm), lambda i: (0, 0))],
        out_specs=[pl.BlockSpec((MOE_TOPK, tm), lambda i: (0, i)),
                   pl.BlockSpec((MOE_TOPK, tm), lambda i: (0, i)),
                   pl.BlockSpec((MOE_TOPK, tm), lambda i: (0, i)),
                   pl.BlockSpec((n_e, 1), lambda i: (0, 0))],
        out_shape=[jax.ShapeDtypeStruct((MOE_TOPK, m), jnp.int32), jax.ShapeDtypeStruct((MOE_TOPK, m), F32),
                   jax.ShapeDtypeStruct((MOE_TOPK, m), jnp.int32), jax.ShapeDtypeStruct((n_e, 1), jnp.int32)],
        scratch_shapes=[pltpu.VMEM((n_e, 1), F32)],
        compiler_params=_params(("arbitrary",), est),
    )(h, g.reshape(1, d).astype(F32), wr_hi, wr_lo, u)


def _row_copy(src_hbm, src_row, dst_ref, dst_row, sem):
    return pltpu.make_async_copy(src_hbm.at[pl.ds(src_row, 1)], dst_ref.at[pl.ds(dst_row, 1)], sem)


PREFETCH_ROWS = 128


def _expert_up_kernel(te_ref, tv_ref, rows_ref, h_hbm, g_ref, w1_ref, w3_ref, o_ref, xbuf, xn_ref, sem):
    t = pl.program_id(0)
    j = pl.program_id(1)
    tm = MOE_TILE
    slot = lax.rem(t, 2)

    def request(tile, first, count):
        dst, dsem = xbuf.at[lax.rem(tile, 2)], sem.at[lax.rem(tile, 2)]

        def start(r, _):
            _row_copy(h_hbm, rows_ref[tile * tm + first + r], dst, first + r, dsem).start()
            return 0
        lax.fori_loop(0, count, start, 0, unroll=8)

    @pl.when(jnp.logical_and(t == 0, j == 0))
    def _():
        request(0, 0, tm)

    nxt = jnp.minimum(t + 1, pl.num_programs(0) - 1)
    @pl.when(jnp.logical_and(jnp.logical_and(t + 1 < pl.num_programs(0), tv_ref[nxt] > 0),
                             j < tm // PREFETCH_ROWS))
    def _():
        request(t + 1, j * PREFETCH_ROWS, PREFETCH_ROWS)

    @pl.when(jnp.logical_and(j == 0, tv_ref[t] > 0))
    def _():
        def wait(r, _):
            _row_copy(h_hbm, 0, xbuf.at[slot], r, sem.at[slot]).wait()
            return 0
        lax.fori_loop(0, tm, wait, 0, unroll=8)
        xn_ref[...] = _rms(xbuf[slot], g_ref[...]).astype(BF16)

    @pl.when(tv_ref[t] > 0)
    def _():
        _swiglu_up(xn_ref, w1_ref, w3_ref, o_ref)

    @pl.when(tv_ref[t] == 0)
    def _():
        o_ref[...] = jnp.zeros_like(o_ref)


def _expert_down_kernel(te_ref, tv_ref, a_ref, w_ref, o_ref):
    t = pl.program_id(0)

    @pl.when(pl.program_id(1) == 0)
    def _():
        o_ref[...] = jnp.zeros_like(o_ref)

    @pl.when(tv_ref[t] > 0)
    def _():
        o_ref[...] += jnp.dot(a_ref[...], w_ref[...], preferred_element_type=F32)


def expert_ffn(h, slot_tok, g, w1, w3, w2, tile_e, tile_valid, tf=1024, nk=4):
    d = h.shape[1]
    n = slot_tok.shape[0]
    f = w1.shape[2]
    nf = f // tf
    tk = f // nk
    tm = MOE_TILE
    assert tm % PREFETCH_ROWS == 0 and nf >= tm // PREFETCH_ROWS
    assert f % tf == 0 and f % nk == 0 and tk % LANES == 0

    def hold(last):
        return lambda t, j, tv: j * tv[t] + last * (1 - tv[t])

    ju = hold(nf - 1)
    est = 2 * (_nbytes((tm, d), F32) + 2 * _nbytes((d, tf), BF16) + _nbytes((tm, tf), BF16)) \
        + _nbytes((tm, d), BF16) + 4 * _nbytes((tm, tf), F32) + _nbytes((tm, d), F32)
    act = pl.pallas_call(
        _expert_up_kernel,
        grid_spec=pltpu.PrefetchScalarGridSpec(
            num_scalar_prefetch=3,
            grid=(n // tm, nf),
            in_specs=[pl.BlockSpec(memory_space=pl.ANY),
                      pl.BlockSpec((1, d), lambda t, j, te, tv, rows: (0, 0)),
                      pl.BlockSpec((None, d, tf), lambda t, j, te, tv, rows: (te[t], 0, ju(t, j, tv))),
                      pl.BlockSpec((None, d, tf), lambda t, j, te, tv, rows: (te[t], 0, ju(t, j, tv)))],
            out_specs=pl.BlockSpec((tm, tf), lambda t, j, te, tv, rows: (t, j)),
            scratch_shapes=[pltpu.VMEM((2, tm, d), F32), pltpu.VMEM((tm, d), BF16),
                            pltpu.SemaphoreType.DMA((2,))],
        ),
        out_shape=jax.ShapeDtypeStruct((n, f), BF16),
        compiler_params=_params(("arbitrary", "arbitrary"), est),
    )(tile_e, tile_valid, slot_tok, h, g.reshape(1, d).astype(F32), w1, w3)
    jd = hold(nk - 1)
    est = 2 * (_nbytes((tm, tk), BF16) + _nbytes((tk, d), BF16) + _nbytes((tm, d), F32)) + 2 * _nbytes((tm, d), F32)
    return pl.pallas_call(
        _expert_down_kernel,
        grid_spec=pltpu.PrefetchScalarGridSpec(
            num_scalar_prefetch=2,
            grid=(n // tm, nk),
            in_specs=[pl.BlockSpec((tm, tk), lambda t, k, te, tv: (t, jd(t, k, tv))),
                      pl.BlockSpec((None, tk, d), lambda t, k, te, tv: (te[t], jd(t, k, tv), 0))],
            out_specs=pl.BlockSpec((tm, d), lambda t, k, te, tv: (t, 0)),
        ),
        out_shape=jax.ShapeDtypeStruct((n, d), F32),
        compiler_params=_params(("arbitrary", "arbitrary"), est),
    )(tile_e, tile_valid, act, w2)


def _combine_kernel(slots_ref, ys_hbm, h_ref, gate_ref, g_ref, o_ref, buf0, buf1, sem):
    base = pl.program_id(0) * GATHER_ROWS

    def start(r, _):
        flat = (base + r) * MOE_TOPK
        _row_copy(ys_hbm, slots_ref[flat], buf0, r, sem).start()
        _row_copy(ys_hbm, slots_ref[flat + 1], buf1, r, sem).start()
        return 0

    def wait(r, _):
        _row_copy(ys_hbm, 0, buf0, r, sem).wait()
        _row_copy(ys_hbm, 0, buf1, r, sem).wait()
        return 0

    lax.fori_loop(0, GATHER_ROWS, start, 0, unroll=8)
    lax.fori_loop(0, GATHER_ROWS, wait, 0, unroll=8)
    gate = gate_ref[...]
    h = h_ref[...] + gate[:, 0:1] * buf0[...] + gate[:, 1:2] * buf1[...]
    o_ref[...] = _rms(h, g_ref[...])


def moe_combine_norm(ys, slot_of_flat, h, gates, g_out):
    m, d = h.shape
    tm = GATHER_ROWS
    return pl.pallas_call(
        _combine_kernel,
        grid_spec=pltpu.PrefetchScalarGridSpec(
            num_scalar_prefetch=1,
            grid=(m // tm,),
            in_specs=[pl.BlockSpec(memory_space=pl.ANY),
                      pl.BlockSpec((tm, d), lambda i, s: (i, 0)),
                      pl.BlockSpec((tm, MOE_TOPK), lambda i, s: (i, 0)),
                      pl.BlockSpec((1, d), lambda i, s: (0, 0))],
            out_specs=pl.BlockSpec((tm, d), lambda i, s: (i, 0)),
            scratch_shapes=[pltpu.VMEM((tm, d), F32), pltpu.VMEM((tm, d), F32), pltpu.SemaphoreType.DMA(())],
        ),
        out_shape=jax.ShapeDtypeStruct((m, d), F32),
        compiler_params=_params(("arbitrary",), 8 * _nbytes((tm, d), F32)),
    )(slot_of_flat, ys, h, gates, g_out.reshape(1, d).astype(F32))


def moe_routing_metadata(top_idx, top_rank, counts, n_tokens):
    tk = n_tokens * MOE_TOPK
    flat_e = top_idx.T.reshape(-1)
    within = top_rank.T.reshape(-1)
    padded = (counts + MOE_TILE - 1) // MOE_TILE * MOE_TILE
    pad_end = jnp.cumsum(padded)
    pad_start = pad_end - padded
    slot_of_flat = (jnp.sum(jnp.where(flat_e[:, None] == jnp.arange(N_EXPERTS)[None, :], pad_start[None, :], 0),
                            axis=1) + within).astype(jnp.int32)
    n_tiles = tk // MOE_TILE + N_EXPERTS
    slot_tok = jnp.zeros((n_tiles * MOE_TILE,), jnp.int32).at[slot_of_flat].set(
        jnp.arange(tk, dtype=jnp.int32) // MOE_TOPK)
    tile_start = jnp.arange(n_tiles, dtype=jnp.int32) * MOE_TILE
    tile_valid = (tile_start < pad_end[-1]).astype(jnp.int32)
    tile_e = jnp.minimum(jnp.sum(tile_start[:, None] >= pad_end[None, :], axis=1), N_EXPERTS - 1)
    last_e = jnp.max(jnp.where(tile_valid > 0, tile_e, 0))
    tile_e = jnp.where(tile_valid > 0, tile_e, last_e).astype(jnp.int32)
    return slot_tok, slot_of_flat, tile_e, tile_valid


def _rope_partner(w):
    half = w.shape[-1] // 2
    return jnp.concatenate([-w[..., half:], w[..., :half]], axis=-1)


def _rope_tables(positions):
    pos = positions.astype(F32).reshape(-1, 1)

    def cs(half):
        inv_freq = ROPE_THETA ** (-jnp.arange(half, dtype=F32) / half)
        ang = pos * inv_freq
        return jnp.cos(ang), jnp.sin(ang)

    c32, s32 = cs(MLA_ROPE // 2)
    zeros = jnp.zeros((pos.shape[0], LANES // 2), F32)
    mla_c = jnp.concatenate([c32, c32, zeros], axis=1)
    mla_s = jnp.concatenate([s32, s32, zeros], axis=1)
    c64, s64 = cs(HEAD_DIM // 2)
    moba_c = jnp.concatenate([c64, c64], axis=1)
    moba_s = jnp.concatenate([-s64, s64], axis=1)
    return mla_c, mla_s, moba_c, moba_s


N_LATENT = 2 * MLA_LORA + 2 * MLA_ROPE
N_QKV = 3 * (MOBA_HEADS + SB_HEADS) * HEAD_DIM


def _layer_weights(w_in, w_uq, w_ukv):
    lat = 2 * MLA_LORA + MLA_ROPE
    w_kr = w_in[:, 2 * MLA_LORA:lat]
    w_all = jnp.concatenate([w_in[:, lat + N_QKV:], w_in[:, lat:lat + N_QKV], w_in[:, :lat], _rope_partner(w_kr)],
                            axis=1).astype(BF16)
    uq = w_uq.reshape(MLA_LORA, MLA_HEADS, MLA_NOPE + MLA_ROPE)
    uq_rope = uq[..., MLA_NOPE:]
    wq = jnp.concatenate([uq[..., :MLA_NOPE], uq_rope, _rope_partner(uq_rope)], axis=-1)
    wq = wq.reshape(MLA_LORA, MLA_HEADS * 2 * LANES).astype(BF16)
    ukv = w_ukv.reshape(MLA_LORA, MLA_HEADS, MLA_NOPE + MLA_V)
    wkv = jnp.concatenate([ukv[..., :MLA_NOPE].reshape(MLA_LORA, -1), ukv[..., MLA_NOPE:].reshape(MLA_LORA, -1)],
                          axis=1).astype(BF16)
    return w_all, wq, wkv


def kernel(x, mem, positions, g_mix, w_in, g_cq, g_ckv, w_uq, w_ukv, w_up_a, w_up_b, w_up_c, w_o, g_x, g_mem, w_xq, w_xk, w_xv, w_xo, g_ffn, w_ff1, w_ff3, w_ff2, w_router, w_e1, w_e3, w_e2, g_final):
    batch, seq, d = x.shape
    t = batch * seq
    depth = g_mix.shape[0]
    assert depth == 2, "the final norm is fused into the expert layer, which must come last"
    mem2 = mem.reshape(-1, d)
    mla_c, mla_s, moba_c, moba_s = _rope_tables(positions)
    assert MOBA_HEADS == SB_HEADS

    n_gate = N_BRANCH * d

    h = x.reshape(t, d)
    hn = rmsnorm(h, g_mix[0], BF16)
    out = None
    for l in range(depth):
        w_all, wq, wkv = _layer_weights(w_in[l], w_uq[l], w_ukv[l])
        zg = matmul(hn, w_all, BF16, tm=2048, tn=1024, cols=(0, n_gate))
        zbc = matmul(hn, w_all, BF16, tm=2048, tn=768, cols=(n_gate, n_gate + N_QKV))
        z1 = matmul(hn, w_all, BF16, tm=2048, tn=384, cols=(n_gate + N_QKV, n_gate + N_QKV + N_LATENT))
        q_a, k_a, v_a = mla_prep(z1, g_cq[l], g_ckv[l], wq, wkv, mla_c, mla_s)
        o_a = causal_attention(q_a, k_a, v_a, batch, MLA_HEADS, 2 * LANES, MLA_V)
        o_b = moba_attention(zbc, moba_c, moba_s, batch, 0)
        o_c = stick_breaking_attention(zbc, batch, 3)
        merged = gated_merge(o_a, o_b, o_c, zg, w_up_a[l].astype(BF16), w_up_b[l].astype(BF16),
                             w_up_c[l].astype(BF16))
        h, hn = matmul_res_norm(merged, w_o[l].astype(BF16), h, g_x[l])
        mn = rmsnorm(mem2, g_mem[l], BF16)
        q_x = matmul(hn, w_xq[l].astype(BF16), BF16, tm=2048)
        k_x = matmul(mn, w_xk[l].astype(BF16), BF16)
        v_x = matmul(mn, w_xv[l].astype(BF16), BF16)
        o_x = cross_attention(q_x, k_x, v_x, batch, XATTN_HEADS)
        h, hn = matmul_res_norm(o_x, w_xo[l].astype(BF16), h, g_ffn[l])
        g_next = g_mix[l + 1] if l + 1 < depth else g_final
        if l % 2 == 0:
            e = l // 2
            h, hn = dense_ffn(hn, w_ff1[e].astype(BF16), w_ff3[e].astype(BF16), w_ff2[e].astype(BF16), h, g_next)
            out = hn
        else:
            e = l // 2
            top_idx, top_gate, top_rank, counts = moe_router(h, g_ffn[l], w_router[e])
            slot_tok, slot_of_flat, tile_e, tile_valid = moe_routing_metadata(top_idx, top_rank, counts[:, 0], t)
            ys = expert_ffn(h, slot_tok, g_ffn[l],w_e1[e].astype(BF16), w_e3[e].astype(BF16), w_e2[e].astype(BF16),
                            tile_e, tile_valid)
            out = moe_combine_norm(ys, slot_of_flat, h, top_gate.T, g_next)
    return out.reshape(batch, seq, d).astype(x.dtype)
```

```python
import functools

import numpy as np
import jax
import jax.numpy as jnp
from jax import lax
from jax.experimental import pallas as pl
from jax.experimental.pallas import tpu as pltpu

F32 = jnp.float32
BF16 = jnp.bfloat16

HEAD_DIM = 128
MLA_HEADS = 6
MLA_LORA = 512
MLA_NOPE = 128
MLA_ROPE = 64
MLA_V = 128
MOBA_HEADS = 5
MOBA_BLOCK = 256
MOBA_TOPK = 3
SB_HEADS = 5
N_BRANCH = 3
ROPE_THETA = 10000.0
XATTN_HEADS = 4
N_EXPERTS = 8
MOE_TOPK = 2
RMS_EPS = 1e-6

LANES = 128
V7X_VMEM_BYTES = 64 * 1024 * 1024
VMEM_CEILING = V7X_VMEM_BYTES - 8 * 1024 * 1024

MASK_VALUE = -1e30
EXP_UNDERFLOW = -104.0

MOE_TILE = 512
GATHER_ROWS = 256


def _params(sem, est_bytes):
    limit = int(min(VMEM_CEILING, max(32 * 1024 * 1024, est_bytes * 5 // 4)))
    return pltpu.CompilerParams(dimension_semantics=sem, vmem_limit_bytes=limit)


def _nbytes(shape, dtype):
    return int(np.prod(shape)) * jnp.dtype(dtype).itemsize


def _rms(x, g):
    return x * lax.rsqrt(jnp.mean(x * x, axis=-1, keepdims=True) + RMS_EPS) * g


def _rmsnorm_kernel(x_ref, g_ref, o_ref):
    o_ref[...] = _rms(x_ref[...].astype(F32), g_ref[...]).astype(o_ref.dtype)


def rmsnorm(x, g, out_dtype, tm=512):
    m, d = x.shape
    tm = min(tm, m)
    return pl.pallas_call(
        _rmsnorm_kernel,
        grid=(m // tm,),
        in_specs=[pl.BlockSpec((tm, d), lambda i: (i, 0)),
                  pl.BlockSpec((1, d), lambda i: (0, 0))],
        out_specs=pl.BlockSpec((tm, d), lambda i: (i, 0)),
        out_shape=jax.ShapeDtypeStruct((m, d), out_dtype),
        compiler_params=_params(("parallel",), 4 * _nbytes((tm, d), F32)),
    )(x, g.reshape(1, d).astype(F32))


def _mm_kernel(a_ref, w_ref, o_ref):
    o_ref[...] = jnp.dot(a_ref[...], w_ref[...], preferred_element_type=F32).astype(o_ref.dtype)


def matmul(a, w, out_dtype, tm=1024, tn=1024, cols=None):
    m, k = a.shape
    c0, c1 = cols if cols is not None else (0, w.shape[1])
    n = c1 - c0
    tm, tn = min(tm, m), min(tn, n)
    assert m % tm == 0 and n % tn == 0 and c0 % tn == 0, (m, n, tm, tn, c0)
    j0 = c0 // tn
    est = 2 * (_nbytes((tm, k), a.dtype) + _nbytes((k, tn), w.dtype) + _nbytes((tm, tn), out_dtype)) \
        + _nbytes((tm, tn), F32)
    return pl.pallas_call(
        _mm_kernel,
        grid=(m // tm, n // tn),
        in_specs=[pl.BlockSpec((tm, k), lambda i, j: (i, 0)),
                  pl.BlockSpec((k, tn), lambda i, j: (0, j0 + j))],
        out_specs=pl.BlockSpec((tm, tn), lambda i, j: (i, j)),
        out_shape=jax.ShapeDtypeStruct((m, n), out_dtype),
        compiler_params=_params(("parallel", "parallel"), est),
    )(a, w)


def _mm_res_norm_kernel(a_ref, w_ref, r_ref, g_ref, h_ref, hn_ref):
    h = r_ref[...] + jnp.dot(a_ref[...], w_ref[...], preferred_element_type=F32)
    h_ref[...] = h
    hn_ref[...] = _rms(h, g_ref[...]).astype(hn_ref.dtype)


def matmul_res_norm(a, w, res, g, tm=512):
    m, k = a.shape
    n = w.shape[1]
    est = 2 * (_nbytes((tm, k), a.dtype) + 2 * _nbytes((tm, n), F32) + _nbytes((tm, n), BF16)) \
        + _nbytes((k, n), w.dtype) + 2 * _nbytes((tm, n), F32)
    return pl.pallas_call(
        _mm_res_norm_kernel,
        grid=(m // tm,),
        in_specs=[pl.BlockSpec((tm, k), lambda i: (i, 0)),
                  pl.BlockSpec((k, n), lambda i: (0, 0), pipeline_mode=pl.Buffered(1)),
                  pl.BlockSpec((tm, n), lambda i: (i, 0)),
                  pl.BlockSpec((1, n), lambda i: (0, 0))],
        out_specs=[pl.BlockSpec((tm, n), lambda i: (i, 0)),
                   pl.BlockSpec((tm, n), lambda i: (i, 0))],
        out_shape=[jax.ShapeDtypeStruct((m, n), F32), jax.ShapeDtypeStruct((m, n), BF16)],
        compiler_params=_params(("parallel",), est),
    )(a, w, res, g.reshape(1, n).astype(F32))


def _half_swap(y, c, s):
    return y * c + pltpu.roll(y, LANES // 2, 1) * s


def _mla_prep_kernel(z_ref, gq_ref, gkv_ref, wq_ref, wkv_ref, c_ref, s_ref, q_ref, k_ref, v_ref, *, scale):
    z = z_ref[...].astype(F32)
    c = c_ref[...]
    s = s_ref[...]
    nq = _rms(z[:, :MLA_LORA], gq_ref[...]).astype(BF16)
    nkv = _rms(z[:, MLA_LORA:2 * MLA_LORA], gkv_ref[...]).astype(BF16)
    q = jnp.dot(nq, wq_ref[...], preferred_element_type=F32)
    kv = jnp.dot(nkv, wkv_ref[...], preferred_element_type=F32)
    k_pe = _half_swap(z[:, 2 * MLA_LORA:], c, s).astype(BF16)
    for h in range(MLA_HEADS):
        lo = 2 * LANES * h
        q_ref[:, lo:lo + LANES] = (q[:, lo:lo + LANES] * scale).astype(BF16)
        q_ref[:, lo + LANES:lo + 2 * LANES] = (_half_swap(q[:, lo + LANES:lo + 2 * LANES], c, s) * scale).astype(BF16)
        k_ref[:, lo:lo + LANES] = kv[:, LANES * h:LANES * (h + 1)].astype(BF16)
        k_ref[:, lo + LANES:lo + 2 * LANES] = k_pe
    v_ref[...] = kv[:, MLA_HEADS * MLA_NOPE:].astype(BF16)


def mla_prep(z1, g_cq, g_ckv, wq, wkv, cos_t, sin_t, tm=512):
    m, zc = z1.shape
    nq, nkv = wq.shape[1], wkv.shape[1]
    scale = float((MLA_NOPE + MLA_ROPE) ** -0.5)
    row = lambda c: pl.BlockSpec((tm, c), lambda i: (i, 0))
    full = lambda a: pl.BlockSpec(a.shape, lambda i: (0, 0))
    g_cq = g_cq.reshape(1, -1).astype(F32)
    g_ckv = g_ckv.reshape(1, -1).astype(F32)
    est = 4 * _nbytes((tm, nq), F32) + 4 * (_nbytes(wq.shape, BF16) + _nbytes(wkv.shape, BF16))
    return pl.pallas_call(
        functools.partial(_mla_prep_kernel, scale=scale),
        grid=(m // tm,),
        in_specs=[row(zc), full(g_cq), full(g_ckv), full(wq), full(wkv), row(LANES), row(LANES)],
        out_specs=[row(nq), row(nq), row(MLA_HEADS * MLA_V)],
        out_shape=[jax.ShapeDtypeStruct((m, nq), BF16), jax.ShapeDtypeStruct((m, nq), BF16),
                   jax.ShapeDtypeStruct((m, MLA_HEADS * MLA_V), BF16)],
        compiler_params=_params(("parallel",), est),
    )(z1, g_cq, g_ckv, wq, wkv, cos_t, sin_t)


def _nt_dot(a, b):
    return lax.dot_general(a, b, (((1,), (1,)), ((), ())), preferred_element_type=F32)


def _softmax_tiles_keymajor(scores, load_vt, carry):
    p, stats = [], []
    for s, (m, l, _) in zip(scores, carry):
        m_new = jnp.maximum(m, jnp.max(s, axis=0, keepdims=True))
        alpha = jnp.exp(m - m_new)
        ph = jnp.exp(s - m_new)
        stats.append((m_new, alpha, alpha * l + jnp.sum(ph, axis=0, keepdims=True)))
        p.append(ph.astype(BF16))
    out = []
    for h, ((m_new, alpha, l_new), (_, _, acc)) in enumerate(zip(stats, carry)):
        acc = alpha * acc
        for n, vt in enumerate(load_vt(h)):
            acc = acc + jnp.dot(vt, p[h][n * vt.shape[1]:(n + 1) * vt.shape[1], :], preferred_element_type=F32)
        out.append((m_new, l_new, acc))
    return out


def _softmax_init_keymajor(queries, dv):
    return (jnp.full((1, queries), MASK_VALUE, F32), jnp.zeros((1, queries), F32), jnp.zeros((dv, queries), F32))


def _transposed_bf16(x):
    return x.astype(F32).T.astype(BF16)


def _softmax_tiles(scores, load_v, carry):
    p, stats = [], []
    for s, (m, l, _) in zip(scores, carry):
        m_new = jnp.maximum(m, jnp.max(s, axis=1, keepdims=True))
        alpha = jnp.exp(m - m_new)
        ph = jnp.exp(s - m_new)
        stats.append((m_new, alpha, alpha * l + jnp.sum(ph, axis=1, keepdims=True)))
        p.append(ph.astype(BF16))
    return [(m_new, l_new, alpha * acc + jnp.dot(p[h], load_v(h), preferred_element_type=F32))
            for h, ((m_new, alpha, l_new), (_, _, acc)) in enumerate(zip(stats, carry))]


def _softmax_init(rows, dv):
    return (jnp.full((rows, 1), MASK_VALUE, F32), jnp.zeros((rows, 1), F32), jnp.zeros((rows, dv), F32))


def _causal_attn_kernel(q_ref, k_ref, v_ref, o_ref, *, tq, tk, dk, dv, hpg):
    i = pl.program_id(2)
    n_full = (i * tq) // tk
    heads = range(hpg)
    q = [q_ref[:, h * dk:(h + 1) * dk] for h in heads]

    def tile(j, carry, mask):
        off = pl.multiple_of(j * tk, tk)
        s = [_nt_dot(q[h], k_ref[pl.ds(off, tk), h * dk:(h + 1) * dk]) for h in heads]
        if mask is not None:
            s = [jnp.where(mask, sh, MASK_VALUE) for sh in s]
        return _softmax_tiles(s, lambda h: v_ref[pl.ds(off, tk), h * dv:(h + 1) * dv], carry)

    carry = lax.fori_loop(0, n_full, lambda j, c: tile(j, c, None), [_softmax_init(tq, dv) for _ in heads])
    qpos = i * tq + lax.broadcasted_iota(jnp.int32, (tq, tk), 0)
    kpos = n_full * tk + lax.broadcasted_iota(jnp.int32, (tq, tk), 1)
    carry = tile(n_full, carry, kpos <= qpos)
    for h in heads:
        _, l, acc = carry[h]
        o_ref[:, h * dv:(h + 1) * dv] = (acc / l).astype(o_ref.dtype)


def causal_attention(q, k, v, batch, heads, dk, dv, tq=256, tk=1024, hpg=3):
    t = q.shape[0]
    s = t // batch
    nq = s // tq
    assert tk % tq == 0 and s % tk == 0 and heads % hpg == 0
    est = 4 * hpg * (_nbytes((s, dk), BF16) + _nbytes((s, dv), BF16)) + 8 * hpg * _nbytes((tq, tk), F32)
    return pl.pallas_call(
        functools.partial(_causal_attn_kernel, tq=tq, tk=tk, dk=dk, dv=dv, hpg=hpg),
        grid=(batch, heads // hpg, nq),
        in_specs=[pl.BlockSpec((tq, hpg * dk), lambda b, g, i: (b * nq + i, g)),
                  pl.BlockSpec((s, hpg * dk), lambda b, g, i: (b, g)),
                  pl.BlockSpec((s, hpg * dv), lambda b, g, i: (b, g))],
        out_specs=pl.BlockSpec((tq, hpg * dv), lambda b, g, i: (b * nq + i, g)),
        out_shape=jax.ShapeDtypeStruct((t, heads * dv), BF16),
        compiler_params=_params(("parallel", "parallel", "arbitrary"), est),
    )(q, k, v)


def _moba_select(gate_t, i, nb):
    nq = gate_t.shape[1]
    row = lax.broadcasted_iota(jnp.int32, (nb, nq), 0)
    rank = jnp.zeros((nb, nq), jnp.int32)
    for jj in range(nb):
        gj = gate_t[jj:jj + 1, :]
        ahead = jnp.logical_or(gj > gate_t, jnp.logical_and(gj == gate_t, jj < row))
        rank = rank + jnp.where(jnp.logical_and(ahead, jj < i), 1, 0)
    return jnp.where(jnp.logical_and(row < i, rank < MOBA_TOPK), 1.0, 0.0)


def _moba_kernel(q_ref, k_ref, v_ref, cq_ref, sq_ref, ck_ref, sk_ref, o_ref, kr_scr, vt_scr, km_scr, sel_scr,
                 *, nb, nh, scale):
    i = pl.program_id(1)
    blk = MOBA_BLOCK
    d = HEAD_DIM
    heads = range(nh)

    @pl.when(i == 0)
    def _():
        def prep(j, _):
            rows = pl.ds(pl.multiple_of(j * blk, blk), blk)
            c, s = ck_ref[rows, :], sk_ref[rows, :]
            for h in heads:
                kj = _half_swap(k_ref[rows, h * d:(h + 1) * d].astype(F32), c, s)
                km_scr[pl.ds(h * nb + j, 1), :] = jnp.mean(kj, axis=0, keepdims=True)
                kr_scr[rows, h * d:(h + 1) * d] = kj.astype(BF16)
                vt_scr[j, h * d:(h + 1) * d, :] = _transposed_bf16(v_ref[rows, h * d:(h + 1) * d])
            return 0
        lax.fori_loop(0, nb, prep, 0)

    cq, sq = cq_ref[...], sq_ref[...]
    qb = []
    for h in heads:
        q = _half_swap(q_ref[:, h * d:(h + 1) * d].astype(F32), cq, sq)
        gate_t = lax.dot_general(km_scr[h * nb:(h + 1) * nb, :], q, (((1,), (1,)), ((), ())),
                                 precision=lax.Precision.HIGHEST, preferred_element_type=F32)
        sel_scr[h * nb:(h + 1) * nb, :] = _moba_select(gate_t, i, nb)
        qb.append((q * scale).astype(BF16))

    def tile(j, nblk, carry, bias_fn):
        rows = pl.ds(pl.multiple_of(j * blk, blk), nblk * blk)
        s = [_nt_dot(kr_scr[rows, h * d:(h + 1) * d], qb[h]) + bias_fn(h) for h in heads]
        return _softmax_tiles_keymajor(
            s, lambda h: [vt_scr[j + n, h * d:(h + 1) * d, :] for n in range(nblk)], carry)

    key_id = lax.broadcasted_iota(jnp.int32, (blk, blk), 0)
    qry_id = lax.broadcasted_iota(jnp.int32, (blk, blk), 1)
    causal = jnp.where(key_id <= qry_id, 0.0, MASK_VALUE)
    carry = tile(i, 1, [_softmax_init_keymajor(blk, d) for _ in heads], lambda h: causal)

    def body(p, carry):
        def bias(h):
            rows = [jnp.broadcast_to((1.0 - sel_scr[pl.ds(h * nb + 2 * p + n, 1), :]) * MASK_VALUE, (blk, blk))
                    for n in range(2)]
            return jnp.concatenate(rows, axis=0)
        return tile(2 * p, 2, carry, bias)

    carry = lax.fori_loop(0, (i + 1) // 2, body, carry)
    for h in heads:
        _, l, acc = carry[h]
        o_ref[:, h * d:(h + 1) * d] = (acc / l).T.astype(o_ref.dtype)


def moba_attention(zbc, cos_t, sin_t, batch, group0):
    t = zbc.shape[0]
    s = t // batch
    blk = MOBA_BLOCK
    nb = s // blk
    nh = MOBA_HEADS
    w = nh * HEAD_DIM
    est = 6 * _nbytes((s, w), BF16) + 4 * _nbytes((s, HEAD_DIM), F32) + 12 * nh * _nbytes((blk, blk), F32)
    return pl.pallas_call(
        functools.partial(_moba_kernel, nb=nb, nh=nh, scale=float(HEAD_DIM ** -0.5)),
        grid=(batch, nb),
        in_specs=[pl.BlockSpec((blk, w), lambda b, i: (b * nb + i, group0)),
                  pl.BlockSpec((s, w), lambda b, i: (b, group0 + 1)),
                  pl.BlockSpec((s, w), lambda b, i: (b, group0 + 2)),
                  pl.BlockSpec((blk, HEAD_DIM), lambda b, i: (b * nb + i, 0)),
                  pl.BlockSpec((blk, HEAD_DIM), lambda b, i: (b * nb + i, 0)),
                  pl.BlockSpec((s, HEAD_DIM), lambda b, i: (b, 0)),
                  pl.BlockSpec((s, HEAD_DIM), lambda b, i: (b, 0))],
        out_specs=pl.BlockSpec((blk, w), lambda b, i: (b * nb + i, 0)),
        out_shape=jax.ShapeDtypeStruct((t, w), BF16),
        scratch_shapes=[pltpu.VMEM((s, w), BF16), pltpu.VMEM((nb, w, blk), BF16),
                        pltpu.VMEM((nh * nb, HEAD_DIM), F32), pltpu.VMEM((nh * nb, blk), F32)],
        compiler_params=_params(("parallel", "arbitrary"), est),
    )(zbc, zbc, zbc, cos_t, sin_t, cos_t, sin_t)


def _sb_kernel(q_ref, k_ref, v_ref, u_ref, o_ref, *, tq, nh, scale):
    i = pl.program_id(1)
    d = HEAD_DIM
    heads = range(nh)
    u = u_ref[...]
    qb = [(q_ref[:, h * d:(h + 1) * d].astype(F32) * scale).astype(BF16) for h in heads]

    def tile(j, carry, strict):
        off = pl.multiple_of(j * tq, tq)
        z = [_nt_dot(qb[h], k_ref[pl.ds(off, tq), h * d:(h + 1) * d]) for h in heads]
        hi, lo = [], []
        for h in heads:
            lsm = -(jnp.maximum(z[h], 0.0) + jnp.log(1.0 + jnp.exp(-jnp.abs(z[h]))))
            if strict is not None:
                lsm = jnp.where(strict, lsm, 0.0)
            lsm_hi, lsm_lo = _split_bf16(lsm)
            hi.append(lsm_hi)
            lo.append(lsm_lo)
        incl = [jnp.dot(hi[h], u, preferred_element_type=F32) + jnp.dot(lo[h], u, preferred_element_type=F32)
                for h in heads]
        a = []
        for h in heads:
            ah = jnp.exp(jnp.minimum(z[h] + incl[h], 0.0) + carry[h][0])
            if strict is not None:
                ah = jnp.where(strict, ah, 0.0)
            a.append(ah.astype(BF16))
        return [(carry[h][0] + incl[h][:, 0:1],
                 carry[h][1] + jnp.dot(a[h], v_ref[pl.ds(off, tq), h * d:(h + 1) * d], preferred_element_type=F32))
                for h in heads]

    r_id = lax.broadcasted_iota(jnp.int32, (tq, tq), 0)
    c_id = lax.broadcasted_iota(jnp.int32, (tq, tq), 1)
    init = [(jnp.zeros((tq, 1), F32), jnp.zeros((tq, d), F32)) for _ in heads]
    carry = tile(i, init, c_id < r_id)

    def live(carry):
        worst = carry[0][0]
        for h in heads[1:]:
            worst = jnp.maximum(worst, carry[h][0])
        return (jnp.max(worst) > EXP_UNDERFLOW).astype(jnp.int32)

    def body(state):
        n, _, carry = state
        carry = tile(i - 1 - n, carry, None)
        return n + 1, live(carry), carry

    _, _, carry = lax.while_loop(lambda st: jnp.logical_and(st[0] < i, st[1] > 0), body,
                                 (jnp.int32(0), live(carry), carry))
    for h in heads:
        o_ref[:, h * d:(h + 1) * d] = carry[h][1].astype(o_ref.dtype)


def stick_breaking_attention(zbc, batch, group0, tq=256):
    t = zbc.shape[0]
    s = t // batch
    nq = s // tq
    nh = SB_HEADS
    w = nh * HEAD_DIM
    u = (jnp.arange(tq)[:, None] >= jnp.arange(tq)[None, :]).astype(BF16)
    est = 8 * _nbytes((s, w), BF16) + 16 * nh * _nbytes((tq, tq), F32)
    return pl.pallas_call(
        functools.partial(_sb_kernel, tq=tq, nh=nh, scale=float(HEAD_DIM ** -0.5)),
        grid=(batch, nq),
        in_specs=[pl.BlockSpec((tq, w), lambda b, i: (b * nq + i, group0)),
                  pl.BlockSpec((s, w), lambda b, i: (b, group0 + 1)),
                  pl.BlockSpec((s, w), lambda b, i: (b, group0 + 2)),
                  pl.BlockSpec((tq, tq), lambda b, i: (0, 0))],
        out_specs=pl.BlockSpec((tq, w), lambda b, i: (b * nq + i, 0)),
        out_shape=jax.ShapeDtypeStruct((t, w), BF16),
        compiler_params=_params(("parallel", "arbitrary"), est),
    )(zbc, zbc, zbc, u)


def _merge_kernel(oa_ref, ob_ref, oc_ref, ga_ref, gb_ref, gc_ref, wa_ref, wb_ref, wc_ref, o_ref):
    def branch(o, g, w):
        return jax.nn.sigmoid(g[...].astype(F32)) * jnp.dot(o[...], w[...], preferred_element_type=F32)
    o_ref[...] = (branch(oa_ref, ga_ref, wa_ref) + branch(ob_ref, gb_ref, wb_ref)
                  + branch(oc_ref, gc_ref, wc_ref)).astype(o_ref.dtype)


def gated_merge(o_a, o_b, o_c, zg, wa, wb, wc, tm=512):
    m = o_a.shape[0]
    d = wa.shape[1]
    row = lambda a: pl.BlockSpec((tm, a.shape[1]), lambda i: (i, 0))
    full = lambda a: pl.BlockSpec(a.shape, lambda i: (0, 0))
    gate = lambda n: pl.BlockSpec((tm, d), lambda i: (i, n))
    est = 4 * _nbytes((d, d), BF16) + 12 * _nbytes((tm, d), F32)
    return pl.pallas_call(
        _merge_kernel,
        grid=(m // tm,),
        in_specs=[row(o_a), row(o_b), row(o_c), gate(0), gate(1), gate(2), full(wa), full(wb), full(wc)],
        out_specs=pl.BlockSpec((tm, d), lambda i: (i, 0)),
        out_shape=jax.ShapeDtypeStruct((m, d), BF16),
        compiler_params=_params(("parallel",), est),
    )(o_a, o_b, o_c, zg, zg, zg, wa, wb, wc)


def _xattn_kernel(q_ref, k_ref, v_ref, o_ref, *, heads, scale):
    hd = q_ref.shape[1] // heads
    for h in range(heads):
        cols = slice(h * hd, (h + 1) * hd)
        s = _nt_dot(q_ref[:, cols], k_ref[:, cols]) * scale
        p = jnp.exp(s - jnp.max(s, axis=1, keepdims=True))
        o = jnp.dot(p.astype(BF16), v_ref[:, cols], preferred_element_type=F32)
        o_ref[:, cols] = (o / jnp.sum(p, axis=1, keepdims=True)).astype(o_ref.dtype)


def cross_attention(q, k, v, batch, heads, tq=512):
    t, d = q.shape
    s = t // batch
    mlen = k.shape[0] // batch
    nq = s // tq
    est = 8 * _nbytes((tq, d), BF16) + 8 * _nbytes((mlen, d), BF16) + 8 * _nbytes((tq, mlen), F32)
    return pl.pallas_call(
        functools.partial(_xattn_kernel, heads=heads, scale=float((d // heads) ** -0.5)),
        grid=(batch, nq),
        in_specs=[pl.BlockSpec((tq, d), lambda b, i: (b * nq + i, 0)),
                  pl.BlockSpec((mlen, d), lambda b, i: (b, 0)),
                  pl.BlockSpec((mlen, d), lambda b, i: (b, 0))],
        out_specs=pl.BlockSpec((tq, d), lambda b, i: (b * nq + i, 0)),
        out_shape=jax.ShapeDtypeStruct((t, d), BF16),
        compiler_params=_params(("parallel", "parallel"), est),
    )(q, k, v)


def _swiglu_up(x_ref, w1_ref, w3_ref, o_ref):
    x = x_ref[...]
    a = jnp.dot(x, w1_ref[...], preferred_element_type=F32)
    b = jnp.dot(x, w3_ref[...], preferred_element_type=F32)
    o_ref[...] = (a * jax.nn.sigmoid(a) * b).astype(BF16)


def _ffn_up_kernel(x_ref, w1_ref, w3_ref, o_ref):
    _swiglu_up(x_ref, w1_ref, w3_ref, o_ref)


def dense_ffn(x, w1, w3, w2, res, g_next, tm_up=1024, tf=512, tm_down=256):
    m, d = x.shape
    f = w1.shape[1]
    assert f % tf == 0
    est = 2 * (_nbytes((tm_up, d), BF16) + 2 * _nbytes((d, tf), BF16) + _nbytes((tm_up, tf), BF16)) \
        + 4 * _nbytes((tm_up, tf), F32)
    act = pl.pallas_call(
        _ffn_up_kernel,
        grid=(m // tm_up, f // tf),
        in_specs=[pl.BlockSpec((tm_up, d), lambda i, j: (i, 0)),
                  pl.BlockSpec((d, tf), lambda i, j: (0, j)),
                  pl.BlockSpec((d, tf), lambda i, j: (0, j))],
        out_specs=pl.BlockSpec((tm_up, tf), lambda i, j: (i, j)),
        out_shape=jax.ShapeDtypeStruct((m, f), BF16),
        compiler_params=_params(("parallel", "parallel"), est),
    )(x, w1, w3)
    return matmul_res_norm(act, w2, res, g_next, tm=tm_down)


def _split_bf16(x):
    hi = x.astype(BF16)
    return hi, (x - hi.astype(F32)).astype(BF16)


def _router_kernel(h_ref, g_ref, wh_ref, wl_ref, u_ref, idx_ref, gate_ref, rank_ref, count_ref, seen_scr):
    @pl.when(pl.program_id(0) == 0)
    def _():
        seen_scr[...] = jnp.zeros_like(seen_scr)

    hn = _rms(h_ref[...], g_ref[...])
    n_e = count_ref.shape[0]
    hn_hi, hn_lo = _split_bf16(hn)
    logits = (_nt_dot(hn_hi, wh_ref[...]) + _nt_dot(hn_lo, wh_ref[...]) + _nt_dot(hn_hi, wl_ref[...])).T[:n_e, :]
    e_id = lax.broadcasted_iota(jnp.int32, logits.shape, 0)
    v1 = jnp.max(logits, axis=0, keepdims=True)
    i1 = jnp.min(jnp.where(logits == v1, e_id, n_e), axis=0, keepdims=True)
    rest = jnp.where(e_id == i1, -jnp.inf, logits)
    v2 = jnp.max(rest, axis=0, keepdims=True)
    i2 = jnp.min(jnp.where(rest == v2, e_id, n_e), axis=0, keepdims=True)
    e2 = jnp.exp(v2 - v1)
    idx_ref[0:1, :] = i1
    idx_ref[1:2, :] = i2
    gate_ref[0:1, :] = 1.0 / (1.0 + e2)
    gate_ref[1:2, :] = e2 / (1.0 + e2)
    pick1 = jnp.where(e_id == i1, 1.0, 0.0)
    pick2 = jnp.where(e_id == i2, 1.0, 0.0)
    both = pick1 + pick2
    earlier = jnp.dot(both.astype(BF16), u_ref[...], preferred_element_type=F32) + seen_scr[...]
    rank_ref[0:1, :] = jnp.sum(pick1 * earlier, axis=0, keepdims=True).astype(jnp.int32)
    rank_ref[1:2, :] = jnp.sum(pick2 * earlier, axis=0, keepdims=True).astype(jnp.int32)
    seen_scr[...] += jnp.sum(both, axis=1, keepdims=True)
    count_ref[...] = seen_scr[...].astype(jnp.int32)


def moe_router(h, g, w_router, tm=512):
    m, d = h.shape
    n_e = w_router.shape[1]
    wr_t = jnp.zeros((LANES, d), F32).at[:n_e].set(w_router.T.astype(F32))
    wr_hi, wr_lo = _split_bf16(wr_t)
    u =(jnp.arange(tm)[:, None] < jnp.arange(tm)[None, :]).astype(BF16)
    est = 6 * _nbytes((tm, d), F32)
    return pl.pallas_call(
        _router_kernel,
        grid=(m // tm,),
        in_specs=[pl.BlockSpec((tm, d), lambda i: (i, 0)),
                  pl.BlockSpec((1, d), lambda i: (0, 0)),
                  pl.BlockSpec((LANES, d), lambda i: (0, 0)),
                  pl.BlockSpec((LANES, d), lambda i: (0, 0)),
                  pl.BlockSpec((tm, tm), lambda i: (0, 0))],
        out_specs=[pl.BlockSpec((MOE_TOPK, tm), lambda i: (0, i)),
                   pl.BlockSpec((MOE_TOPK, tm), lambda i: (0, i)),
                   pl.BlockSpec((MOE_TOPK, tm), lambda i: (0, i)),
                   pl.BlockSpec((n_e, 1), lambda i: (0, 0))],
        out_shape=[jax.ShapeDtypeStruct((MOE_TOPK, m), jnp.int32), jax.ShapeDtypeStruct((MOE_TOPK, m), F32),
                   jax.ShapeDtypeStruct((MOE_TOPK, m), jnp.int32), jax.ShapeDtypeStruct((n_e, 1), jnp.int32)],
        scratch_shapes=[pltpu.VMEM((n_e, 1), F32)],
        compiler_params=_params(("arbitrary",), est),
    )(h, g.reshape(1, d).astype(F32), wr_hi, wr_lo, u)


def _row_copy(src_hbm, src_row, dst_ref, dst_row, sem):
    return pltpu.make_async_copy(src_hbm.at[pl.ds(src_row, 1)], dst_ref.at[pl.ds(dst_row, 1)], sem)


PREFETCH_ROWS = 128


def _expert_up_kernel(te_ref, tv_ref, rows_ref, h_hbm, g_ref, w1_ref, w3_ref, o_ref, xbuf, xn_ref, sem):
    t = pl.program_id(0)
    j = pl.program_id(1)
    tm = MOE_TILE
    slot = lax.rem(t, 2)

    def request(tile, first, count):
        dst, dsem = xbuf.at[lax.rem(tile, 2)], sem.at[lax.rem(tile, 2)]

        def start(r, _):
            _row_copy(h_hbm, rows_ref[tile * tm + first + r], dst, first + r, dsem).start()
            return 0
        lax.fori_loop(0, count, start, 0, unroll=8)

    @pl.when(jnp.logical_and(t == 0, j == 0))
    def _():
        request(0, 0, tm)

    nxt = jnp.minimum(t + 1, pl.num_programs(0) - 1)
    @pl.when(jnp.logical_and(jnp.logical_and(t + 1 < pl.num_programs(0), tv_ref[nxt] > 0),
                             j < tm // PREFETCH_ROWS))
    def _():
        request(t + 1, j * PREFETCH_ROWS, PREFETCH_ROWS)

    @pl.when(jnp.logical_and(j == 0, tv_ref[t] > 0))
    def _():
        def wait(r, _):
            _row_copy(h_hbm, 0, xbuf.at[slot], r, sem.at[slot]).wait()
            return 0
        lax.fori_loop(0, tm, wait, 0, unroll=8)
        xn_ref[...] = _rms(xbuf[slot], g_ref[...]).astype(BF16)

    @pl.when(tv_ref[t] > 0)
    def _():
        _swiglu_up(xn_ref, w1_ref, w3_ref, o_ref)

    @pl.when(tv_ref[t] == 0)
    def _():
        o_ref[...] = jnp.zeros_like(o_ref)


def _expert_down_kernel(te_ref, tv_ref, a_ref, w_ref, o_ref):
    t = pl.program_id(0)

    @pl.when(pl.program_id(1) == 0)
    def _():
        o_ref[...] = jnp.zeros_like(o_ref)

    @pl.when(tv_ref[t] > 0)
    def _():
        o_ref[...] += jnp.dot(a_ref[...], w_ref[...], preferred_element_type=F32)


def expert_ffn(h, slot_tok, g, w1, w3, w2, tile_e, tile_valid, tf=1024, nk=4):
    d = h.shape[1]
    n = slot_tok.shape[0]
    f = w1.shape[2]
    nf = f // tf
    tk = f // nk
    tm = MOE_TILE
    assert tm % PREFETCH_ROWS == 0 and nf >= tm // PREFETCH_ROWS
    assert f % tf == 0 and f % nk == 0 and tk % LANES == 0

    def hold(last):
        return lambda t, j, tv: j * tv[t] + last * (1 - tv[t])

    ju = hold(nf - 1)
    est = 2 * (_nbytes((tm, d), F32) + 2 * _nbytes((d, tf), BF16) + _nbytes((tm, tf), BF16)) \
        + _nbytes((tm, d), BF16) + 4 * _nbytes((tm, tf), F32) + _nbytes((tm, d), F32)
    act = pl.pallas_call(
        _expert_up_kernel,
        grid_spec=pltpu.PrefetchScalarGridSpec(
            num_scalar_prefetch=3,
            grid=(n // tm, nf),
            in_specs=[pl.BlockSpec(memory_space=pl.ANY),
                      pl.BlockSpec((1, d), lambda t, j, te, tv, rows: (0, 0)),
                      pl.BlockSpec((None, d, tf), lambda t, j, te, tv, rows: (te[t], 0, ju(t, j, tv))),
                      pl.BlockSpec((None, d, tf), lambda t, j, te, tv, rows: (te[t], 0, ju(t, j, tv)))],
            out_specs=pl.BlockSpec((tm, tf), lambda t, j, te, tv, rows: (t, j)),
            scratch_shapes=[pltpu.VMEM((2, tm, d), F32), pltpu.VMEM((tm, d), BF16),
                            pltpu.SemaphoreType.DMA((2,))],
        ),
        out_shape=jax.ShapeDtypeStruct((n, f), BF16),
        compiler_params=_params(("arbitrary", "arbitrary"), est),
    )(tile_e, tile_valid, slot_tok, h, g.reshape(1, d).astype(F32), w1, w3)
    jd = hold(nk - 1)
    est = 2 * (_nbytes((tm, tk), BF16) + _nbytes((tk, d), BF16) + _nbytes((tm, d), F32)) + 2 * _nbytes((tm, d), F32)
    return pl.pallas_call(
        _expert_down_kernel,
        grid_spec=pltpu.PrefetchScalarGridSpec(
            num_scalar_prefetch=2,
            grid=(n // tm, nk),
            in_specs=[pl.BlockSpec((tm, tk), lambda t, k, te, tv: (t, jd(t, k, tv))),
                      pl.BlockSpec((None, tk, d), lambda t, k, te, tv: (te[t], jd(t, k, tv), 0))],
            out_specs=pl.BlockSpec((tm, d), lambda t, k, te, tv: (t, 0)),
        ),
        out_shape=jax.ShapeDtypeStruct((n, d), F32),
        compiler_params=_params(("arbitrary", "arbitrary"), est),
    )(tile_e, tile_valid, act, w2)


def _combine_kernel(slots_ref, ys_hbm, h_ref, gate_ref, g_ref, o_ref, buf0, buf1, sem):
    base = pl.program_id(0) * GATHER_ROWS

    def start(r, _):
        flat = (base + r) * MOE_TOPK
        _row_copy(ys_hbm, slots_ref[flat], buf0, r, sem).start()
        _row_copy(ys_hbm, slots_ref[flat + 1], buf1, r, sem).start()
        return 0

    def wait(r, _):
        _row_copy(ys_hbm, 0, buf0, r, sem).wait()
        _row_copy(ys_hbm, 0, buf1, r, sem).wait()
        return 0

    lax.fori_loop(0, GATHER_ROWS, start, 0, unroll=8)
    lax.fori_loop(0, GATHER_ROWS, wait, 0, unroll=8)
    gate = gate_ref[...]
    h = h_ref[...] + gate[:, 0:1] * buf0[...] + gate[:, 1:2] * buf1[...]
    o_ref[...] = _rms(h, g_ref[...])


def moe_combine_norm(ys, slot_of_flat, h, gates, g_out):
    m, d = h.shape
    tm = GATHER_ROWS
    return pl.pallas_call(
        _combine_kernel,
        grid_spec=pltpu.PrefetchScalarGridSpec(
            num_scalar_prefetch=1,
            grid=(m // tm,),
            in_specs=[pl.BlockSpec(memory_space=pl.ANY),
                      pl.BlockSpec((tm, d), lambda i, s: (i, 0)),
                      pl.BlockSpec((tm, MOE_TOPK), lambda i, s: (i, 0)),
                      pl.BlockSpec((1, d), lambda i, s: (0, 0))],
            out_specs=pl.BlockSpec((tm, d), lambda i, s: (i, 0)),
            scratch_shapes=[pltpu.VMEM((tm, d), F32), pltpu.VMEM((tm, d), F32), pltpu.SemaphoreType.DMA(())],
        ),
        out_shape=jax.ShapeDtypeStruct((m, d), F32),
        compiler_params=_params(("arbitrary",), 8 * _nbytes((tm, d), F32)),
    )(slot_of_flat, ys, h, gates, g_out.reshape(1, d).astype(F32))


def moe_routing_metadata(top_idx, top_rank, counts, n_tokens):
    tk = n_tokens * MOE_TOPK
    flat_e = top_idx.T.reshape(-1)
    within = top_rank.T.reshape(-1)
    padded = (counts + MOE_TILE - 1) // MOE_TILE * MOE_TILE
    pad_end = jnp.cumsum(padded)
    pad_start = pad_end - padded
    slot_of_flat = (jnp.sum(jnp.where(flat_e[:, None] == jnp.arange(N_EXPERTS)[None, :], pad_start[None, :], 0),
                            axis=1) + within).astype(jnp.int32)
    n_tiles = tk // MOE_TILE + N_EXPERTS
    slot_tok = jnp.zeros((n_tiles * MOE_TILE,), jnp.int32).at[slot_of_flat].set(
        jnp.arange(tk, dtype=jnp.int32) // MOE_TOPK)
    tile_start = jnp.arange(n_tiles, dtype=jnp.int32) * MOE_TILE
    tile_valid = (tile_start < pad_end[-1]).astype(jnp.int32)
    tile_e = jnp.minimum(jnp.sum(tile_start[:, None] >= pad_end[None, :], axis=1), N_EXPERTS - 1)
    last_e = jnp.max(jnp.where(tile_valid > 0, tile_e, 0))
    tile_e = jnp.where(tile_valid > 0, tile_e, last_e).astype(jnp.int32)
    return slot_tok, slot_of_flat, tile_e, tile_valid


def _rope_partner(w):
    half = w.shape[-1] // 2
    return jnp.concatenate([-w[..., half:], w[..., :half]], axis=-1)


def _rope_tables(positions):
    pos = positions.astype(F32).reshape(-1, 1)

    def cs(half):
        inv_freq = ROPE_THETA ** (-jnp.arange(half, dtype=F32) / half)
        ang = pos * inv_freq
        return jnp.cos(ang), jnp.sin(ang)

    c32, s32 = cs(MLA_ROPE // 2)
    zeros = jnp.zeros((pos.shape[0], LANES // 2), F32)
    mla_c = jnp.concatenate([c32, c32, zeros], axis=1)
    mla_s = jnp.concatenate([s32, s32, zeros], axis=1)
    c64, s64 = cs(HEAD_DIM // 2)
    moba_c = jnp.concatenate([c64, c64], axis=1)
    moba_s = jnp.concatenate([-s64, s64], axis=1)
    return mla_c, mla_s, moba_c, moba_s


N_LATENT_IN = 2 * MLA_LORA + MLA_ROPE
N_QKV = 3 * (MOBA_HEADS + SB_HEADS) * HEAD_DIM


def _cast_shift_kernel(a_ref, b_ref, o_ref):
    half = LANES // 2
    o_ref[...] = jnp.concatenate([a_ref[:, half:], b_ref[:, :half]], axis=1).astype(o_ref.dtype)


def cast_columns_from(w, layer, start, tn=256):
    _, k, n = w.shape
    half = LANES // 2
    base = start - half
    width = n - start
    assert start % LANES == half and base % tn == 0 and width % tn == 0
    return pl.pallas_call(
        _cast_shift_kernel,
        grid=(width // tn,),
        in_specs=[pl.BlockSpec((None, k, tn), lambda j: (layer, 0, base // tn + j)),
                  pl.BlockSpec((None, k, LANES), lambda j: (layer, 0, (base + tn) // LANES + j * (tn // LANES)))],
        out_specs=pl.BlockSpec((k, tn), lambda j: (0, j)),
        out_shape=jax.ShapeDtypeStruct((k, width), BF16),
        compiler_params=_params(("parallel",), 8 * _nbytes((k, tn), F32)),
    )(w, w)


def _layer_weights(w_in, w_uq, w_ukv):
    w_kr = w_in[:, 2 * MLA_LORA:N_LATENT_IN]
    w1 = jnp.concatenate([w_in[:, :N_LATENT_IN], _rope_partner(w_kr)], axis=1).astype(BF16)
    uq = w_uq.reshape(MLA_LORA, MLA_HEADS, MLA_NOPE + MLA_ROPE)
    uq_rope = uq[..., MLA_NOPE:]
    wq = jnp.concatenate([uq[..., :MLA_NOPE], uq_rope, _rope_partner(uq_rope)], axis=-1)
    wq = wq.reshape(MLA_LORA, MLA_HEADS * 2 * LANES).astype(BF16)
    ukv = w_ukv.reshape(MLA_LORA, MLA_HEADS, MLA_NOPE + MLA_V)
    wkv = jnp.concatenate([ukv[..., :MLA_NOPE].reshape(MLA_LORA, -1), ukv[..., MLA_NOPE:].reshape(MLA_LORA, -1)],
                          axis=1).astype(BF16)
    return w1, wq, wkv


def kernel(x, mem, positions, g_mix, w_in, g_cq, g_ckv, w_uq, w_ukv, w_up_a, w_up_b, w_up_c, w_o, g_x, g_mem, w_xq, w_xk, w_xv, w_xo, g_ffn, w_ff1, w_ff3, w_ff2, w_router, w_e1, w_e3, w_e2, g_final):
    batch, seq, d = x.shape
    t = batch * seq
    depth = g_mix.shape[0]
    assert depth == 2, "the final norm is fused into the expert layer, which must come last"
    mem2 = mem.reshape(-1, d)
    mla_c, mla_s, moba_c, moba_s = _rope_tables(positions)
    assert MOBA_HEADS == SB_HEADS

    n_gate = N_BRANCH * d

    h = x.reshape(t, d)
    hn = rmsnorm(h, g_mix[0], BF16)
    out = None
    for l in range(depth):
        w1, wq, wkv = _layer_weights(w_in[l], w_uq[l], w_ukv[l])
        w_rest = cast_columns_from(w_in, l, N_LATENT_IN)
        z1 = matmul(hn, w1, BF16, tn=w1.shape[1])
        zbc = matmul(hn, w_rest, BF16, tn=1280, cols=(0, N_QKV))
        zg = matmul(hn, w_rest, BF16, tm=2048, tn=768, cols=(N_QKV, N_QKV + n_gate))
        q_a, k_a, v_a = mla_prep(z1, g_cq[l], g_ckv[l], wq, wkv, mla_c, mla_s)
        o_a = causal_attention(q_a, k_a, v_a, batch, MLA_HEADS, 2 * LANES, MLA_V)
        o_b = moba_attention(zbc, moba_c, moba_s, batch, 0)
        o_c = stick_breaking_attention(zbc, batch, 3)
        merged = gated_merge(o_a, o_b, o_c, zg, w_up_a[l].astype(BF16), w_up_b[l].astype(BF16),
                             w_up_c[l].astype(BF16))
        h, hn = matmul_res_norm(merged, w_o[l].astype(BF16), h, g_x[l])
        mn = rmsnorm(mem2, g_mem[l], BF16)
        q_x = matmul(hn, w_xq[l].astype(BF16), BF16, tm=2048)
        k_x = matmul(mn, w_xk[l].astype(BF16), BF16)
        v_x = matmul(mn, w_xv[l].astype(BF16), BF16)
        o_x = cross_attention(q_x, k_x, v_x, batch, XATTN_HEADS)
        h, hn = matmul_res_norm(o_x, w_xo[l].astype(BF16), h, g_ffn[l])
        g_next = g_mix[l + 1] if l + 1 < depth else g_final
        if l % 2 == 0:
            e = l // 2
            h, hn = dense_ffn(hn, w_ff1[e].astype(BF16), w_ff3[e].astype(BF16), w_ff2[e].astype(BF16), h, g_next)
            out = hn
        else:
            e = l // 2
            top_idx, top_gate, top_rank, counts = moe_router(h, g_ffn[l], w_router[e])
            slot_tok, slot_of_flat, tile_e, tile_valid = moe_routing_metadata(top_idx, top_rank, counts[:, 0], t)
            ys = expert_ffn(h, slot_tok, g_ffn[l],w_e1[e].astype(BF16), w_e3[e].astype(BF16), w_e2[e].astype(BF16),
                            tile_e, tile_valid)
            out = moe_combine_norm(ys, slot_of_flat, h, top_gate.T, g_next)
    return out.reshape(batch, seq, d).astype(x.dtype)
```

```python
import functools

import numpy as np
import jax
import jax.numpy as jnp
from jax import lax
from jax.experimental import pallas as pl
from jax.experimental.pallas import tpu as pltpu

F32 = jnp.float32
BF16 = jnp.bfloat16

HEAD_DIM = 128
MLA_HEADS = 6
MLA_LORA = 512
MLA_NOPE = 128
MLA_ROPE = 64
MLA_V = 128
MOBA_HEADS = 5
MOBA_BLOCK = 256
MOBA_TOPK = 3
SB_HEADS = 5
N_BRANCH = 3
ROPE_THETA = 10000.0
XATTN_HEADS = 4
N_EXPERTS = 8
MOE_TOPK = 2
RMS_EPS = 1e-6

LANES = 128
V7X_VMEM_BYTES = 64 * 1024 * 1024
VMEM_CEILING = V7X_VMEM_BYTES - 8 * 1024 * 1024

MASK_VALUE = -1e30
EXP_UNDERFLOW = -104.0

MOE_TILE = 512
GATHER_ROWS = 256


def _params(sem, est_bytes):
    limit = int(min(VMEM_CEILING, max(32 * 1024 * 1024, est_bytes * 5 // 4)))
    return pltpu.CompilerParams(dimension_semantics=sem, vmem_limit_bytes=limit)


def _nbytes(shape, dtype):
    return int(np.prod(shape)) * jnp.dtype(dtype).itemsize


def _rms(x, g):
    return x * lax.rsqrt(jnp.mean(x * x, axis=-1, keepdims=True) + RMS_EPS) * g


def _rmsnorm_kernel(x_ref, g_ref, o_ref):
    o_ref[...] = _rms(x_ref[...].astype(F32), g_ref[...]).astype(o_ref.dtype)


def rmsnorm(x, g, out_dtype, tm=512):
    m, d = x.shape
    tm = min(tm, m)
    return pl.pallas_call(
        _rmsnorm_kernel,
        grid=(m // tm,),
        in_specs=[pl.BlockSpec((tm, d), lambda i: (i, 0)),
                  pl.BlockSpec((1, d), lambda i: (0, 0))],
        out_specs=pl.BlockSpec((tm, d), lambda i: (i, 0)),
        out_shape=jax.ShapeDtypeStruct((m, d), out_dtype),
        compiler_params=_params(("parallel",), 4 * _nbytes((tm, d), F32)),
    )(x, g.reshape(1, d).astype(F32))


def _mm_kernel(a_ref, w_ref, o_ref):
    o_ref[...] = jnp.dot(a_ref[...], w_ref[...], preferred_element_type=F32).astype(o_ref.dtype)


def matmul(a, w, out_dtype, tm=1024, tn=1024):
    m, k = a.shape
    n = w.shape[1]
    tm, tn = min(tm, m), min(tn, n)
    assert m % tm == 0 and n % tn == 0, (m, n, tm, tn)
    est = 2 * (_nbytes((tm, k), a.dtype) + _nbytes((k, tn), w.dtype) + _nbytes((tm, tn), out_dtype)) \
        + _nbytes((tm, tn), F32)
    return pl.pallas_call(
        _mm_kernel,
        grid=(m // tm, n // tn),
        in_specs=[pl.BlockSpec((tm, k), lambda i, j: (i, 0)),
                  pl.BlockSpec((k, tn), lambda i, j: (0, j))],
        out_specs=pl.BlockSpec((tm, tn), lambda i, j: (i, j)),
        out_shape=jax.ShapeDtypeStruct((m, n), out_dtype),
        compiler_params=_params(("parallel", "parallel"), est),
    )(a, w)


def _mm_res_norm_kernel(a_ref, w_ref, r_ref, g_ref, h_ref, hn_ref):
    h = r_ref[...] + jnp.dot(a_ref[...], w_ref[...], preferred_element_type=F32)
    h_ref[...] = h
    hn_ref[...] = _rms(h, g_ref[...]).astype(hn_ref.dtype)


def matmul_res_norm(a, w, res, g, tm=512):
    m, k = a.shape
    n = w.shape[1]
    est = 2 * (_nbytes((tm, k), a.dtype) + 2 * _nbytes((tm, n), F32) + _nbytes((tm, n), BF16)) \
        + _nbytes((k, n), w.dtype) + 2 * _nbytes((tm, n), F32)
    return pl.pallas_call(
        _mm_res_norm_kernel,
        grid=(m // tm,),
        in_specs=[pl.BlockSpec((tm, k), lambda i: (i, 0)),
                  pl.BlockSpec((k, n), lambda i: (0, 0), pipeline_mode=pl.Buffered(1)),
                  pl.BlockSpec((tm, n), lambda i: (i, 0)),
                  pl.BlockSpec((1, n), lambda i: (0, 0))],
        out_specs=[pl.BlockSpec((tm, n), lambda i: (i, 0)),
                   pl.BlockSpec((tm, n), lambda i: (i, 0))],
        out_shape=[jax.ShapeDtypeStruct((m, n), F32), jax.ShapeDtypeStruct((m, n), BF16)],
        compiler_params=_params(("parallel",), est),
    )(a, w, res, g.reshape(1, n).astype(F32))


def _half_swap(y, c, s):
    return y * c + pltpu.roll(y, LANES // 2, 1) * s


def _mla_prep_kernel(z_ref, gq_ref, gkv_ref, wq_ref, wkv_ref, c_ref, s_ref, q_ref, k_ref, v_ref, *, scale):
    z = z_ref[...].astype(F32)
    c = c_ref[...]
    s = s_ref[...]
    nq = _rms(z[:, :MLA_LORA], gq_ref[...]).astype(BF16)
    nkv = _rms(z[:, MLA_LORA:2 * MLA_LORA], gkv_ref[...]).astype(BF16)
    q = jnp.dot(nq, wq_ref[...], preferred_element_type=F32)
    kv = jnp.dot(nkv, wkv_ref[...], preferred_element_type=F32)
    k_pe = _half_swap(z[:, 2 * MLA_LORA:], c, s).astype(BF16)
    for h in range(MLA_HEADS):
        lo = 2 * LANES * h
        q_ref[:, lo:lo + LANES] = (q[:, lo:lo + LANES] * scale).astype(BF16)
        q_ref[:, lo + LANES:lo + 2 * LANES] = (_half_swap(q[:, lo + LANES:lo + 2 * LANES], c, s) * scale).astype(BF16)
        k_ref[:, lo:lo + LANES] = kv[:, LANES * h:LANES * (h + 1)].astype(BF16)
        k_ref[:, lo + LANES:lo + 2 * LANES] = k_pe
    v_ref[...] = kv[:, MLA_HEADS * MLA_NOPE:].astype(BF16)


def mla_prep(z1, g_cq, g_ckv, wq, wkv, cos_t, sin_t, tm=512):
    m, zc = z1.shape
    nq, nkv = wq.shape[1], wkv.shape[1]
    scale = float((MLA_NOPE + MLA_ROPE) ** -0.5)
    row = lambda c: pl.BlockSpec((tm, c), lambda i: (i, 0))
    full = lambda a: pl.BlockSpec(a.shape, lambda i: (0, 0))
    g_cq = g_cq.reshape(1, -1).astype(F32)
    g_ckv = g_ckv.reshape(1, -1).astype(F32)
    est = 4 * _nbytes((tm, nq), F32) + 4 * (_nbytes(wq.shape, BF16) + _nbytes(wkv.shape, BF16))
    return pl.pallas_call(
        functools.partial(_mla_prep_kernel, scale=scale),
        grid=(m // tm,),
        in_specs=[row(zc), full(g_cq), full(g_ckv), full(wq), full(wkv), row(LANES), row(LANES)],
        out_specs=[row(nq), row(nq), row(MLA_HEADS * MLA_V)],
        out_shape=[jax.ShapeDtypeStruct((m, nq), BF16), jax.ShapeDtypeStruct((m, nq), BF16),
                   jax.ShapeDtypeStruct((m, MLA_HEADS * MLA_V), BF16)],
        compiler_params=_params(("parallel",), est),
    )(z1, g_cq, g_ckv, wq, wkv, cos_t, sin_t)


def _nt_dot(a, b):
    return lax.dot_general(a, b, (((1,), (1,)), ((), ())), preferred_element_type=F32)


def _softmax_tiles_keymajor(scores, load_vt, carry):
    p, stats = [], []
    for s, (m, l, _) in zip(scores, carry):
        m_new = jnp.maximum(m, jnp.max(s, axis=0, keepdims=True))
        alpha = jnp.exp(m - m_new)
        ph = jnp.exp(s - m_new)
        stats.append((m_new, alpha, alpha * l + jnp.sum(ph, axis=0, keepdims=True)))
        p.append(ph.astype(BF16))
    out = []
    for h, ((m_new, alpha, l_new), (_, _, acc)) in enumerate(zip(stats, carry)):
        acc = alpha * acc
        for n, vt in enumerate(load_vt(h)):
            acc = acc + jnp.dot(vt, p[h][n * vt.shape[1]:(n + 1) * vt.shape[1], :], preferred_element_type=F32)
        out.append((m_new, l_new, acc))
    return out


def _softmax_init_keymajor(queries, dv):
    return (jnp.full((1, queries), MASK_VALUE, F32), jnp.zeros((1, queries), F32), jnp.zeros((dv, queries), F32))


def _transposed_bf16(x):
    return x.astype(F32).T.astype(BF16)


def _softmax_tiles(scores, load_v, carry):
    p, stats = [], []
    for s, (m, l, _) in zip(scores, carry):
        m_new = jnp.maximum(m, jnp.max(s, axis=1, keepdims=True))
        alpha = jnp.exp(m - m_new)
        ph = jnp.exp(s - m_new)
        stats.append((m_new, alpha, alpha * l + jnp.sum(ph, axis=1, keepdims=True)))
        p.append(ph.astype(BF16))
    return [(m_new, l_new, alpha * acc + jnp.dot(p[h], load_v(h), preferred_element_type=F32))
            for h, ((m_new, alpha, l_new), (_, _, acc)) in enumerate(zip(stats, carry))]


def _softmax_init(rows, dv):
    return (jnp.full((rows, 1), MASK_VALUE, F32), jnp.zeros((rows, 1), F32), jnp.zeros((rows, dv), F32))


def _causal_attn_kernel(q_ref, k_ref, v_ref, o_ref, *, tq, tk, dk, dv, hpg):
    i = pl.program_id(2)
    n_full = (i * tq) // tk
    heads = range(hpg)
    q = [q_ref[:, h * dk:(h + 1) * dk] for h in heads]

    def tile(j, carry, mask):
        off = pl.multiple_of(j * tk, tk)
        s = [_nt_dot(q[h], k_ref[pl.ds(off, tk), h * dk:(h + 1) * dk]) for h in heads]
        if mask is not None:
            s = [jnp.where(mask, sh, MASK_VALUE) for sh in s]
        return _softmax_tiles(s, lambda h: v_ref[pl.ds(off, tk), h * dv:(h + 1) * dv], carry)

    carry = lax.fori_loop(0, n_full, lambda j, c: tile(j, c, None), [_softmax_init(tq, dv) for _ in heads])
    qpos = i * tq + lax.broadcasted_iota(jnp.int32, (tq, tk), 0)
    kpos = n_full * tk + lax.broadcasted_iota(jnp.int32, (tq, tk), 1)
    carry = tile(n_full, carry, kpos <= qpos)
    for h in heads:
        _, l, acc = carry[h]
        o_ref[:, h * dv:(h + 1) * dv] = (acc / l).astype(o_ref.dtype)


def causal_attention(q, k, v, batch, heads, dk, dv, tq=256, tk=1024, hpg=3):
    t = q.shape[0]
    s = t // batch
    nq = s // tq
    assert tk % tq == 0 and s % tk == 0 and heads % hpg == 0
    est = 4 * hpg * (_nbytes((s, dk), BF16) + _nbytes((s, dv), BF16)) + 8 * hpg * _nbytes((tq, tk), F32)
    return pl.pallas_call(
        functools.partial(_causal_attn_kernel, tq=tq, tk=tk, dk=dk, dv=dv, hpg=hpg),
        grid=(batch, heads // hpg, nq),
        in_specs=[pl.BlockSpec((tq, hpg * dk), lambda b, g, i: (b * nq + i, g)),
                  pl.BlockSpec((s, hpg * dk), lambda b, g, i: (b, g)),
                  pl.BlockSpec((s, hpg * dv), lambda b, g, i: (b, g))],
        out_specs=pl.BlockSpec((tq, hpg * dv), lambda b, g, i: (b * nq + i, g)),
        out_shape=jax.ShapeDtypeStruct((t, heads * dv), BF16),
        compiler_params=_params(("parallel", "parallel", "arbitrary"), est),
    )(q, k, v)


def _moba_select(gate_t, i, nb):
    nq = gate_t.shape[1]
    row = lax.broadcasted_iota(jnp.int32, (nb, nq), 0)
    rank = jnp.zeros((nb, nq), jnp.int32)
    for jj in range(nb):
        gj = gate_t[jj:jj + 1, :]
        ahead = jnp.logical_or(gj > gate_t, jnp.logical_and(gj == gate_t, jj < row))
        rank = rank + jnp.where(jnp.logical_and(ahead, jj < i), 1, 0)
    return jnp.where(jnp.logical_and(row < i, rank < MOBA_TOPK), 1.0, 0.0)


def _moba_kernel(q_ref, k_ref, v_ref, cq_ref, sq_ref, ck_ref, sk_ref, o_ref, kr_scr, vt_scr, km_scr, sel_scr,
                 *, nb, nh, scale):
    i = pl.program_id(1)
    blk = MOBA_BLOCK
    d = HEAD_DIM
    heads = range(nh)

    @pl.when(i == 0)
    def _():
        def prep(j, _):
            rows = pl.ds(pl.multiple_of(j * blk, blk), blk)
            c, s = ck_ref[rows, :], sk_ref[rows, :]
            for h in heads:
                kj = _half_swap(k_ref[rows, h * d:(h + 1) * d].astype(F32), c, s)
                km_scr[pl.ds(h * nb + j, 1), :] = jnp.mean(kj, axis=0, keepdims=True)
                kr_scr[rows, h * d:(h + 1) * d] = kj.astype(BF16)
                vt_scr[j, h * d:(h + 1) * d, :] = _transposed_bf16(v_ref[rows, h * d:(h + 1) * d])
            return 0
        lax.fori_loop(0, nb, prep, 0)

    cq, sq = cq_ref[...], sq_ref[...]
    qb = []
    for h in heads:
        q = _half_swap(q_ref[:, h * d:(h + 1) * d].astype(F32), cq, sq)
        gate_t = lax.dot_general(km_scr[h * nb:(h + 1) * nb, :], q, (((1,), (1,)), ((), ())),
                                 precision=lax.Precision.HIGHEST, preferred_element_type=F32)
        sel_scr[h * nb:(h + 1) * nb, :] = _moba_select(gate_t, i, nb)
        qb.append((q * scale).astype(BF16))

    def tile(j, nblk, carry, bias_fn):
        rows = pl.ds(pl.multiple_of(j * blk, blk), nblk * blk)
        s = [_nt_dot(kr_scr[rows, h * d:(h + 1) * d], qb[h]) + bias_fn(h) for h in heads]
        return _softmax_tiles_keymajor(
            s, lambda h: [vt_scr[j + n, h * d:(h + 1) * d, :] for n in range(nblk)], carry)

    key_id = lax.broadcasted_iota(jnp.int32, (blk, blk), 0)
    qry_id = lax.broadcasted_iota(jnp.int32, (blk, blk), 1)
    causal = jnp.where(key_id <= qry_id, 0.0, MASK_VALUE)
    carry = tile(i, 1, [_softmax_init_keymajor(blk, d) for _ in heads], lambda h: causal)

    def body(p, carry):
        def bias(h):
            rows = [jnp.broadcast_to((1.0 - sel_scr[pl.ds(h * nb + 2 * p + n, 1), :]) * MASK_VALUE, (blk, blk))
                    for n in range(2)]
            return jnp.concatenate(rows, axis=0)
        return tile(2 * p, 2, carry, bias)

    carry = lax.fori_loop(0, (i + 1) // 2, body, carry)
    for h in heads:
        _, l, acc = carry[h]
        o_ref[:, h * d:(h + 1) * d] = (acc / l).T.astype(o_ref.dtype)


def moba_attention(zbc, cos_t, sin_t, batch, group0):
    t = zbc.shape[0]
    s = t // batch
    blk = MOBA_BLOCK
    nb = s // blk
    nh = MOBA_HEADS
    w = nh * HEAD_DIM
    est = 6 * _nbytes((s, w), BF16) + 4 * _nbytes((s, HEAD_DIM), F32) + 12 * nh * _nbytes((blk, blk), F32)
    return pl.pallas_call(
        functools.partial(_moba_kernel, nb=nb, nh=nh, scale=float(HEAD_DIM ** -0.5)),
        grid=(batch, nb),
        in_specs=[pl.BlockSpec((blk, w), lambda b, i: (b * nb + i, group0)),
                  pl.BlockSpec((s, w), lambda b, i: (b, group0 + 1)),
                  pl.BlockSpec((s, w), lambda b, i: (b, group0 + 2)),
                  pl.BlockSpec((blk, HEAD_DIM), lambda b, i: (b * nb + i, 0)),
                  pl.BlockSpec((blk, HEAD_DIM), lambda b, i: (b * nb + i, 0)),
                  pl.BlockSpec((s, HEAD_DIM), lambda b, i: (b, 0)),
                  pl.BlockSpec((s, HEAD_DIM), lambda b, i: (b, 0))],
        out_specs=pl.BlockSpec((blk, w), lambda b, i: (b * nb + i, 0)),
        out_shape=jax.ShapeDtypeStruct((t, w), BF16),
        scratch_shapes=[pltpu.VMEM((s, w), BF16), pltpu.VMEM((nb, w, blk), BF16),
                        pltpu.VMEM((nh * nb, HEAD_DIM), F32), pltpu.VMEM((nh * nb, blk), F32)],
        compiler_params=_params(("parallel", "arbitrary"), est),
    )(zbc, zbc, zbc, cos_t, sin_t, cos_t, sin_t)


def _sb_kernel(q_ref, k_ref, v_ref, u_ref, o_ref, *, tq, nh, scale):
    i = pl.program_id(1)
    d = HEAD_DIM
    heads = range(nh)
    u = u_ref[...]
    qb = [(q_ref[:, h * d:(h + 1) * d].astype(F32) * scale).astype(BF16) for h in heads]

    def tile(j, carry, strict):
        off = pl.multiple_of(j * tq, tq)
        z = [_nt_dot(qb[h], k_ref[pl.ds(off, tq), h * d:(h + 1) * d]) for h in heads]
        hi, lo = [], []
        for h in heads:
            lsm = -(jnp.maximum(z[h], 0.0) + jnp.log(1.0 + jnp.exp(-jnp.abs(z[h]))))
            if strict is not None:
                lsm = jnp.where(strict, lsm, 0.0)
            lsm_hi, lsm_lo = _split_bf16(lsm)
            hi.append(lsm_hi)
            lo.append(lsm_lo)
        incl = [jnp.dot(hi[h], u, preferred_element_type=F32) + jnp.dot(lo[h], u, preferred_element_type=F32)
                for h in heads]
        a = []
        for h in heads:
            ah = jnp.exp(jnp.minimum(z[h] + incl[h], 0.0) + carry[h][0])
            if strict is not None:
                ah = jnp.where(strict, ah, 0.0)
            a.append(ah.astype(BF16))
        return [(carry[h][0] + incl[h][:, 0:1],
                 carry[h][1] + jnp.dot(a[h], v_ref[pl.ds(off, tq), h * d:(h + 1) * d], preferred_element_type=F32))
                for h in heads]

    r_id = lax.broadcasted_iota(jnp.int32, (tq, tq), 0)
    c_id = lax.broadcasted_iota(jnp.int32, (tq, tq), 1)
    init = [(jnp.zeros((tq, 1), F32), jnp.zeros((tq, d), F32)) for _ in heads]
    carry = tile(i, init, c_id < r_id)

    def live(carry):
        worst = carry[0][0]
        for h in heads[1:]:
            worst = jnp.maximum(worst, carry[h][0])
        return (jnp.max(worst) > EXP_UNDERFLOW).astype(jnp.int32)

    def body(state):
        n, _, carry = state
        carry = tile(i - 1 - n, carry, None)
        return n + 1, live(carry), carry

    _, _, carry = lax.while_loop(lambda st: jnp.logical_and(st[0] < i, st[1] > 0), body,
                                 (jnp.int32(0), live(carry), carry))
    for h in heads:
        o_ref[:, h * d:(h + 1) * d] = carry[h][1].astype(o_ref.dtype)


def stick_breaking_attention(zbc, batch, group0, tq=256):
    t = zbc.shape[0]
    s = t // batch
    nq = s // tq
    nh = SB_HEADS
    w = nh * HEAD_DIM
    u = (jnp.arange(tq)[:, None] >= jnp.arange(tq)[None, :]).astype(BF16)
    est = 8 * _nbytes((s, w), BF16) + 16 * nh * _nbytes((tq, tq), F32)
    return pl.pallas_call(
        functools.partial(_sb_kernel, tq=tq, nh=nh, scale=float(HEAD_DIM ** -0.5)),
        grid=(batch, nq),
        in_specs=[pl.BlockSpec((tq, w), lambda b, i: (b * nq + i, group0)),
                  pl.BlockSpec((s, w), lambda b, i: (b, group0 + 1)),
                  pl.BlockSpec((s, w), lambda b, i: (b, group0 + 2)),
                  pl.BlockSpec((tq, tq), lambda b, i: (0, 0))],
        out_specs=pl.BlockSpec((tq, w), lambda b, i: (b * nq + i, 0)),
        out_shape=jax.ShapeDtypeStruct((t, w), BF16),
        compiler_params=_params(("parallel", "arbitrary"), est),
    )(zbc, zbc, zbc, u)


def _merge_kernel(oa_ref, ob_ref, oc_ref, ga_ref, gb_ref, gc_ref, wa_ref, wb_ref, wc_ref, o_ref):
    def branch(o, g, w):
        return jax.nn.sigmoid(g[...].astype(F32)) * jnp.dot(o[...], w[...], preferred_element_type=F32)
    o_ref[...] = (branch(oa_ref, ga_ref, wa_ref) + branch(ob_ref, gb_ref, wb_ref)
                  + branch(oc_ref, gc_ref, wc_ref)).astype(o_ref.dtype)


def gated_merge(o_a, o_b, o_c, zg, wa, wb, wc, tm=512):
    m = o_a.shape[0]
    d = wa.shape[1]
    row = lambda a: pl.BlockSpec((tm, a.shape[1]), lambda i: (i, 0))
    full = lambda a: pl.BlockSpec(a.shape, lambda i: (0, 0))
    gate = lambda n: pl.BlockSpec((tm, d), lambda i: (i, n))
    est = 4 * _nbytes((d, d), BF16) + 12 * _nbytes((tm, d), F32)
    return pl.pallas_call(
        _merge_kernel,
        grid=(m // tm,),
        in_specs=[row(o_a), row(o_b), row(o_c), gate(0), gate(1), gate(2), full(wa), full(wb), full(wc)],
        out_specs=pl.BlockSpec((tm, d), lambda i: (i, 0)),
        out_shape=jax.ShapeDtypeStruct((m, d), BF16),
        compiler_params=_params(("parallel",), est),
    )(o_a, o_b, o_c, zg, zg, zg, wa, wb, wc)


def _xattn_kernel(q_ref, k_ref, v_ref, o_ref, *, heads, scale):
    hd = q_ref.shape[1] // heads
    for h in range(heads):
        cols = slice(h * hd, (h + 1) * hd)
        s = _nt_dot(q_ref[:, cols], k_ref[:, cols]) * scale
        p = jnp.exp(s - jnp.max(s, axis=1, keepdims=True))
        o = jnp.dot(p.astype(BF16), v_ref[:, cols], preferred_element_type=F32)
        o_ref[:, cols] = (o / jnp.sum(p, axis=1, keepdims=True)).astype(o_ref.dtype)


def cross_attention(q, k, v, batch, heads, tq=512):
    t, d = q.shape
    s = t // batch
    mlen = k.shape[0] // batch
    nq = s // tq
    est = 8 * _nbytes((tq, d), BF16) + 8 * _nbytes((mlen, d), BF16) + 8 * _nbytes((tq, mlen), F32)
    return pl.pallas_call(
        functools.partial(_xattn_kernel, heads=heads, scale=float((d // heads) ** -0.5)),
        grid=(batch, nq),
        in_specs=[pl.BlockSpec((tq, d), lambda b, i: (b * nq + i, 0)),
                  pl.BlockSpec((mlen, d), lambda b, i: (b, 0)),
                  pl.BlockSpec((mlen, d), lambda b, i: (b, 0))],
        out_specs=pl.BlockSpec((tq, d), lambda b, i: (b * nq + i, 0)),
        out_shape=jax.ShapeDtypeStruct((t, d), BF16),
        compiler_params=_params(("parallel", "parallel"), est),
    )(q, k, v)


def _swiglu_up(x_ref, w1_ref, w3_ref, o_ref):
    x = x_ref[...]
    a = jnp.dot(x, w1_ref[...], preferred_element_type=F32)
    b = jnp.dot(x, w3_ref[...], preferred_element_type=F32)
    o_ref[...] = (a * jax.nn.sigmoid(a) * b).astype(BF16)


def _ffn_up_kernel(x_ref, w1_ref, w3_ref, o_ref):
    _swiglu_up(x_ref, w1_ref, w3_ref, o_ref)


def dense_ffn(x, w1, w3, w2, res, g_next, tm_up=1024, tf=512, tm_down=256):
    m, d = x.shape
    f = w1.shape[1]
    assert f % tf == 0
    est = 2 * (_nbytes((tm_up, d), BF16) + 2 * _nbytes((d, tf), BF16) + _nbytes((tm_up, tf), BF16)) \
        + 4 * _nbytes((tm_up, tf), F32)
    act = pl.pallas_call(
        _ffn_up_kernel,
        grid=(m // tm_up, f // tf),
        in_specs=[pl.BlockSpec((tm_up, d), lambda i, j: (i, 0)),
                  pl.BlockSpec((d, tf), lambda i, j: (0, j)),
                  pl.BlockSpec((d, tf), lambda i, j: (0, j))],
        out_specs=pl.BlockSpec((tm_up, tf), lambda i, j: (i, j)),
        out_shape=jax.ShapeDtypeStruct((m, f), BF16),
        compiler_params=_params(("parallel", "parallel"), est),
    )(x, w1, w3)
    return matmul_res_norm(act, w2, res, g_next, tm=tm_down)


def _split_bf16(x):
    hi = x.astype(BF16)
    return hi, (x - hi.astype(F32)).astype(BF16)


def _router_kernel(h_ref, g_ref, wh_ref, wl_ref, u_ref, idx_ref, gate_ref, rank_ref, count_ref, seen_scr):
    @pl.when(pl.program_id(0) == 0)
    def _():
        seen_scr[...] = jnp.zeros_like(seen_scr)

    hn = _rms(h_ref[...], g_ref[...])
    n_e = count_ref.shape[0]
    hn_hi, hn_lo = _split_bf16(hn)
    logits = (_nt_dot(hn_hi, wh_ref[...]) + _nt_dot(hn_lo, wh_ref[...]) + _nt_dot(hn_hi, wl_ref[...])).T[:n_e, :]
    e_id = lax.broadcasted_iota(jnp.int32, logits.shape, 0)
    v1 = jnp.max(logits, axis=0, keepdims=True)
    i1 = jnp.min(jnp.where(logits == v1, e_id, n_e), axis=0, keepdims=True)
    rest = jnp.where(e_id == i1, -jnp.inf, logits)
    v2 = jnp.max(rest, axis=0, keepdims=True)
    i2 = jnp.min(jnp.where(rest == v2, e_id, n_e), axis=0, keepdims=True)
    e2 = jnp.exp(v2 - v1)
    idx_ref[0:1, :] = i1
    idx_ref[1:2, :] = i2
    gate_ref[0:1, :] = 1.0 / (1.0 + e2)
    gate_ref[1:2, :] = e2 / (1.0 + e2)
    pick1 = jnp.where(e_id == i1, 1.0, 0.0)
    pick2 = jnp.where(e_id == i2, 1.0, 0.0)
    both = pick1 + pick2
    earlier = jnp.dot(both.astype(BF16), u_ref[...], preferred_element_type=F32) + seen_scr[...]
    rank_ref[0:1, :] = jnp.sum(pick1 * earlier, axis=0, keepdims=True).astype(jnp.int32)
    rank_ref[1:2, :] = jnp.sum(pick2 * earlier, axis=0, keepdims=True).astype(jnp.int32)
    seen_scr[...] += jnp.sum(both, axis=1, keepdims=True)
    count_ref[...] = seen_scr[...].astype(jnp.int32)


def moe_router(h, g, w_router, tm=512):
    m, d = h.shape
    n_e = w_router.shape[1]
    wr_t = jnp.zeros((LANES, d), F32).at[:n_e].set(w_router.T.astype(F32))
    wr_hi, wr_lo = _split_bf16(wr_t)
    u = (jnp.arange(tm)[:, None] < jnp.arange(tm)[None, :]).astype(BF16)
    est = 6 * _nbytes((tm, d), F32)
    return pl.pallas_call(
        _router_kernel,
        grid=(m // tm,),
        in_specs=[pl.BlockSpec((tm, d), lambda i: (i, 0)),
                  pl.BlockSpec((1, d), lambda i: (0, 0)),
                  pl.BlockSpec((LANES, d), lambda i: (0, 0)),
                  pl.BlockSpec((LANES, d), lambda i: (0, 0)),
                  pl.BlockSpec((tm, tm), lambda i: (0, 0))],
        out_specs=[pl.BlockSpec((MOE_TOPK, tm), lambda i: (0, i)),
                   pl.BlockSpec((MOE_TOPK, tm), lambda i: (0, i)),
                   pl.BlockSpec((MOE_TOPK, tm), lambda i: (0, i)),
                   pl.BlockSpec((n_e, 1), lambda i: (0, 0))],
        out_shape=[jax.ShapeDtypeStruct((MOE_TOPK, m), jnp.int32), jax.ShapeDtypeStruct((MOE_TOPK, m), F32),
                   jax.ShapeDtypeStruct((MOE_TOPK, m), jnp.int32), jax.ShapeDtypeStruct((n_e, 1), jnp.int32)],
        scratch_shapes=[pltpu.VMEM((n_e, 1), F32)],
        compiler_params=_params(("arbitrary",), est),
    )(h, g.reshape(1, d).astype(F32), wr_hi, wr_lo, u)


def _row_copy(src_hbm, src_row, dst_ref, dst_row, sem):
    return pltpu.make_async_copy(src_hbm.at[pl.ds(src_row, 1)], dst_ref.at[pl.ds(dst_row, 1)], sem)


PREFETCH_ROWS = 128


def _expert_up_kernel(te_ref, tv_ref, rows_ref, h_hbm, g_ref, w1_ref, w3_ref, o_ref, xbuf, xn_ref, sem):
    t = pl.program_id(0)
    j = pl.program_id(1)
    tm = MOE_TILE
    slot = lax.rem(t, 2)

    def request(tile, first, count):
        dst, dsem = xbuf.at[lax.rem(tile, 2)], sem.at[lax.rem(tile, 2)]

        def start(r, _):
            _row_copy(h_hbm, rows_ref[tile * tm + first + r], dst, first + r, dsem).start()
            return 0
        lax.fori_loop(0, count, start, 0, unroll=8)

    @pl.when(jnp.logical_and(t == 0, j == 0))
    def _():
        request(0, 0, tm)

    nxt = jnp.minimum(t + 1, pl.num_programs(0) - 1)
    @pl.when(jnp.logical_and(jnp.logical_and(t + 1 < pl.num_programs(0), tv_ref[nxt] > 0),
                             j < tm // PREFETCH_ROWS))
    def _():
        request(t + 1, j * PREFETCH_ROWS, PREFETCH_ROWS)

    @pl.when(jnp.logical_and(j == 0, tv_ref[t] > 0))
    def _():
        def wait(r, _):
            _row_copy(h_hbm, 0, xbuf.at[slot], r, sem.at[slot]).wait()
            return 0
        lax.fori_loop(0, tm, wait, 0, unroll=8)
        xn_ref[...] = _rms(xbuf[slot], g_ref[...]).astype(BF16)

    @pl.when(tv_ref[t] > 0)
    def _():
        _swiglu_up(xn_ref, w1_ref, w3_ref, o_ref)

    @pl.when(tv_ref[t] == 0)
    def _():
        o_ref[...] = jnp.zeros_like(o_ref)


def _expert_down_kernel(te_ref, tv_ref, a_ref, w_ref, o_ref):
    t = pl.program_id(0)

    @pl.when(pl.program_id(1) == 0)
    def _():
        o_ref[...] = jnp.zeros_like(o_ref)

    @pl.when(tv_ref[t] > 0)
    def _():
        o_ref[...] += jnp.dot(a_ref[...], w_ref[...], preferred_element_type=F32)


def expert_ffn(h, slot_tok, g, w1, w3, w2, tile_e, tile_valid, tf=1024, nk=4):
    d = h.shape[1]
    n = slot_tok.shape[0]
    f = w1.shape[2]
    nf = f // tf
    tk = f // nk
    tm = MOE_TILE
    assert tm % PREFETCH_ROWS == 0 and nf >= tm // PREFETCH_ROWS
    assert f % tf == 0 and f % nk == 0 and tk % LANES == 0

    def hold(last):
        return lambda t, j, tv: j * tv[t] + last * (1 - tv[t])

    ju = hold(nf - 1)
    est = 2 * (_nbytes((tm, d), F32) + 2 * _nbytes((d, tf), BF16) + _nbytes((tm, tf), BF16)) \
        + _nbytes((tm, d), BF16) + 4 * _nbytes((tm, tf), F32) + _nbytes((tm, d), F32)
    act = pl.pallas_call(
        _expert_up_kernel,
        grid_spec=pltpu.PrefetchScalarGridSpec(
            num_scalar_prefetch=3,
            grid=(n // tm, nf),
            in_specs=[pl.BlockSpec(memory_space=pl.ANY),
                      pl.BlockSpec((1, d), lambda t, j, te, tv, rows: (0, 0)),
                      pl.BlockSpec((None, d, tf), lambda t, j, te, tv, rows: (te[t], 0, ju(t, j, tv))),
                      pl.BlockSpec((None, d, tf), lambda t, j, te, tv, rows: (te[t], 0, ju(t, j, tv)))],
            out_specs=pl.BlockSpec((tm, tf), lambda t, j, te, tv, rows: (t, j)),
            scratch_shapes=[pltpu.VMEM((2, tm, d), F32), pltpu.VMEM((tm, d), BF16),
                            pltpu.SemaphoreType.DMA((2,))],
        ),
        out_shape=jax.ShapeDtypeStruct((n, f), BF16),
        compiler_params=_params(("arbitrary", "arbitrary"), est),
    )(tile_e, tile_valid, slot_tok, h, g.reshape(1, d).astype(F32), w1, w3)
    jd = hold(nk - 1)
    est = 2 * (_nbytes((tm, tk), BF16) + _nbytes((tk, d), BF16) + _nbytes((tm, d), F32)) + 2 * _nbytes((tm, d), F32)
    return pl.pallas_call(
        _expert_down_kernel,
        grid_spec=pltpu.PrefetchScalarGridSpec(
            num_scalar_prefetch=2,
            grid=(n // tm, nk),
            in_specs=[pl.BlockSpec((tm, tk), lambda t, k, te, tv: (t, jd(t, k, tv))),
                      pl.BlockSpec((None, tk, d), lambda t, k, te, tv: (te[t], jd(t, k, tv), 0))],
            out_specs=pl.BlockSpec((tm, d), lambda t, k, te, tv: (t, 0)),
        ),
        out_shape=jax.ShapeDtypeStruct((n, d), F32),
        compiler_params=_params(("arbitrary", "arbitrary"), est),
    )(tile_e, tile_valid, act, w2)


def _combine_kernel(slots_ref, ys_hbm, h_ref, gate_ref, g_ref, o_ref, buf0, buf1, sem):
    base = pl.program_id(0) * GATHER_ROWS

    def start(r, _):
        flat = (base + r) * MOE_TOPK
        _row_copy(ys_hbm, slots_ref[flat], buf0, r, sem).start()
        _row_copy(ys_hbm, slots_ref[flat + 1], buf1, r, sem).start()
        return 0

    def wait(r, _):
        _row_copy(ys_hbm, 0, buf0, r, sem).wait()
        _row_copy(ys_hbm, 0, buf1, r, sem).wait()
        return 0

    lax.fori_loop(0, GATHER_ROWS, start, 0, unroll=8)
    lax.fori_loop(0, GATHER_ROWS, wait, 0, unroll=8)
    gate = gate_ref[...]
    h = h_ref[...] + gate[:, 0:1] * buf0[...] + gate[:, 1:2] * buf1[...]
    o_ref[...] = _rms(h, g_ref[...])


def moe_combine_norm(ys, slot_of_flat, h, gates, g_out):
    m, d = h.shape
    tm = GATHER_ROWS
    return pl.pallas_call(
        _combine_kernel,
        grid_spec=pltpu.PrefetchScalarGridSpec(
            num_scalar_prefetch=1,
            grid=(m // tm,),
            in_specs=[pl.BlockSpec(memory_space=pl.ANY),
                      pl.BlockSpec((tm, d), lambda i, s: (i, 0)),
                      pl.BlockSpec((tm, MOE_TOPK), lambda i, s: (i, 0)),
                      pl.BlockSpec((1, d), lambda i, s: (0, 0))],
            out_specs=pl.BlockSpec((tm, d), lambda i, s: (i, 0)),
            scratch_shapes=[pltpu.VMEM((tm, d), F32), pltpu.VMEM((tm, d), F32), pltpu.SemaphoreType.DMA(())],
        ),
        out_shape=jax.ShapeDtypeStruct((m, d), F32),
        compiler_params=_params(("arbitrary",), 8 * _nbytes((tm, d), F32)),
    )(slot_of_flat, ys, h, gates, g_out.reshape(1, d).astype(F32))


def moe_routing_metadata(top_idx, top_rank, counts, n_tokens):
    tk = n_tokens * MOE_TOPK
    flat_e = top_idx.T.reshape(-1)
    within = top_rank.T.reshape(-1)
    padded = (counts + MOE_TILE - 1) // MOE_TILE * MOE_TILE
    pad_end = jnp.cumsum(padded)
    pad_start = pad_end - padded
    slot_of_flat = (jnp.sum(jnp.where(flat_e[:, None] == jnp.arange(N_EXPERTS)[None, :], pad_start[None, :], 0),
                            axis=1) + within).astype(jnp.int32)
    n_tiles = tk // MOE_TILE + N_EXPERTS
    slot_tok = jnp.zeros((n_tiles * MOE_TILE,), jnp.int32).at[slot_of_flat].set(
        jnp.arange(tk, dtype=jnp.int32) // MOE_TOPK)
    tile_start = jnp.arange(n_tiles, dtype=jnp.int32) * MOE_TILE
    tile_valid = (tile_start < pad_end[-1]).astype(jnp.int32)
    tile_e = jnp.minimum(jnp.sum(tile_start[:, None] >= pad_end[None, :], axis=1), N_EXPERTS - 1)
    last_e = jnp.max(jnp.where(tile_valid > 0, tile_e, 0))
    tile_e = jnp.where(tile_valid > 0, tile_e, last_e).astype(jnp.int32)
    return slot_tok, slot_of_flat, tile_e, tile_valid


def _rope_partner(w):
    half = w.shape[-1] // 2
    return jnp.concatenate([-w[..., half:], w[..., :half]], axis=-1)


def _rope_tables(positions):
    pos = positions.astype(F32).reshape(-1, 1)

    def cs(half):
        inv_freq = ROPE_THETA ** (-jnp.arange(half, dtype=F32) / half)
        ang = pos * inv_freq
        return jnp.cos(ang), jnp.sin(ang)

    c32, s32 = cs(MLA_ROPE // 2)
    zeros = jnp.zeros((pos.shape[0], LANES // 2), F32)
    mla_c = jnp.concatenate([c32, c32, zeros], axis=1)
    mla_s = jnp.concatenate([s32, s32, zeros], axis=1)
    c64, s64 = cs(HEAD_DIM // 2)
    moba_c = jnp.concatenate([c64, c64], axis=1)
    moba_s = jnp.concatenate([-s64, s64], axis=1)
    return mla_c, mla_s, moba_c, moba_s


def _layer_weights(w_in, w_uq, w_ukv):
    o = 0
    w_cq = w_in[:, o:o + MLA_LORA]; o += MLA_LORA
    w_ckv = w_in[:, o:o + MLA_LORA]; o += MLA_LORA
    w_kr = w_in[:, o:o + MLA_ROPE]; o += MLA_ROPE
    nbc = 3 * (MOBA_HEADS + SB_HEADS) * HEAD_DIM
    w_bc = w_in[:, o:o + nbc]; o += nbc
    w_g = w_in[:, o:]
    w1 = jnp.concatenate([w_cq, w_ckv, w_kr, _rope_partner(w_kr)], axis=1)
    uq = w_uq.reshape(MLA_LORA, MLA_HEADS, MLA_NOPE + MLA_ROPE)
    uq_rope = uq[..., MLA_NOPE:]
    wq = jnp.concatenate([uq[..., :MLA_NOPE], uq_rope, _rope_partner(uq_rope)], axis=-1)
    wq = wq.reshape(MLA_LORA, MLA_HEADS * 2 * LANES)
    ukv = w_ukv.reshape(MLA_LORA, MLA_HEADS, MLA_NOPE + MLA_V)
    wkv = jnp.concatenate([ukv[..., :MLA_NOPE].reshape(MLA_LORA, -1), ukv[..., MLA_NOPE:].reshape(MLA_LORA, -1)],
                          axis=1)
    return w1, w_bc, w_g, wq, wkv


def kernel(x, mem, positions, g_mix, w_in, g_cq, g_ckv, w_uq, w_ukv, w_up_a, w_up_b, w_up_c, w_o, g_x, g_mem, w_xq, w_xk, w_xv, w_xo, g_ffn, w_ff1, w_ff3, w_ff2, w_router, w_e1, w_e3, w_e2, g_final):
    batch, seq, d = x.shape
    t = batch * seq
    depth = g_mix.shape[0]
    assert depth == 2, "the final norm is fused into the expert layer, which must come last"
    mem2 = mem.reshape(-1, d)
    mla_c, mla_s, moba_c, moba_s = _rope_tables(positions)
    assert MOBA_HEADS == SB_HEADS

    w_in, w_uq, w_ukv = w_in.astype(BF16), w_uq.astype(BF16), w_ukv.astype(BF16)

    h = x.reshape(t, d)
    hn = rmsnorm(h, g_mix[0], BF16)
    out = None
    for l in range(depth):
        w1, w_bc, w_g, wq, wkv = _layer_weights(w_in[l], w_uq[l], w_ukv[l])
        z1 = matmul(hn, w1, BF16, tn=w1.shape[1])
        zbc = matmul(hn, w_bc, BF16, tn=1280)
        zg = matmul(hn, w_g, BF16, tm=2048, tn=1024)
        q_a, k_a, v_a = mla_prep(z1, g_cq[l], g_ckv[l], wq, wkv, mla_c, mla_s)
        o_a = causal_attention(q_a, k_a, v_a, batch, MLA_HEADS, 2 * LANES, MLA_V)
        o_b = moba_attention(zbc, moba_c, moba_s, batch, 0)
        o_c = stick_breaking_attention(zbc, batch, 3)
        merged = gated_merge(o_a, o_b, o_c, zg, w_up_a[l].astype(BF16), w_up_b[l].astype(BF16),
                             w_up_c[l].astype(BF16))
        h, hn = matmul_res_norm(merged, w_o[l].astype(BF16), h, g_x[l])
        mn = rmsnorm(mem2, g_mem[l], BF16)
        q_x = matmul(hn, w_xq[l].astype(BF16), BF16, tm=2048)
        k_x = matmul(mn, w_xk[l].astype(BF16), BF16)
        v_x = matmul(mn, w_xv[l].astype(BF16), BF16)
        o_x = cross_attention(q_x, k_x, v_x, batch, XATTN_HEADS)
        h, hn = matmul_res_norm(o_x, w_xo[l].astype(BF16), h, g_ffn[l])
        g_next = g_mix[l + 1] if l + 1 < depth else g_final
        if l % 2 == 0:
            e = l // 2
            h, hn = dense_ffn(hn, w_ff1[e].astype(BF16), w_ff3[e].astype(BF16), w_ff2[e].astype(BF16), h, g_next)
            out = hn
        else:
            e = l // 2
            top_idx, top_gate, top_rank, counts = moe_router(h, g_ffn[l], w_router[e])
            slot_tok, slot_of_flat, tile_e, tile_valid = moe_routing_metadata(top_idx, top_rank, counts[:, 0], t)
            ys = expert_ffn(h, slot_tok, g_ffn[l],w_e1[e].astype(BF16), w_e3[e].astype(BF16), w_e2[e].astype(BF16),
                            tile_e, tile_valid)
            out = moe_combine_norm(ys, slot_of_flat, h, top_gate.T, g_next)
    return out.reshape(batch, seq, d).astype(x.dtype)
```

```python
import functools

import numpy as np
import jax
import jax.numpy as jnp
from jax import lax
from jax.experimental import pallas as pl
from jax.experimental.pallas import tpu as pltpu

F32 = jnp.float32
BF16 = jnp.bfloat16

HEAD_DIM = 128
MLA_HEADS = 6
MLA_LORA = 512
MLA_NOPE = 128
MLA_ROPE = 64
MLA_V = 128
MOBA_HEADS = 5
MOBA_BLOCK = 256
MOBA_TOPK = 3
SB_HEADS = 5
N_BRANCH = 3
ROPE_THETA = 10000.0
XATTN_HEADS = 4
N_EXPERTS = 8
MOE_TOPK = 2
RMS_EPS = 1e-6

LANES = 128
V7X_VMEM_BYTES = 64 * 1024 * 1024
VMEM_CEILING = V7X_VMEM_BYTES - 8 * 1024 * 1024

MASK_VALUE = -1e30
EXP_UNDERFLOW = -104.0

MOE_TILE = 1024
GATHER_ROWS = 256


def _params(sem, est_bytes):
    limit = int(min(VMEM_CEILING, max(32 * 1024 * 1024, est_bytes * 5 // 4)))
    return pltpu.CompilerParams(dimension_semantics=sem, vmem_limit_bytes=limit)


def _nbytes(shape, dtype):
    return int(np.prod(shape)) * jnp.dtype(dtype).itemsize


def _rms(x, g):
    return x * lax.rsqrt(jnp.mean(x * x, axis=-1, keepdims=True) + RMS_EPS) * g


def _rmsnorm_kernel(x_ref, g_ref, o_ref):
    o_ref[...] = _rms(x_ref[...].astype(F32), g_ref[...]).astype(o_ref.dtype)


def rmsnorm(x, g, out_dtype, tm=512):
    m, d = x.shape
    tm = min(tm, m)
    return pl.pallas_call(
        _rmsnorm_kernel,
        grid=(m // tm,),
        in_specs=[pl.BlockSpec((tm, d), lambda i: (i, 0)),
                  pl.BlockSpec((1, d), lambda i: (0, 0))],
        out_specs=pl.BlockSpec((tm, d), lambda i: (i, 0)),
        out_shape=jax.ShapeDtypeStruct((m, d), out_dtype),
        compiler_params=_params(("parallel",), 4 * _nbytes((tm, d), F32)),
    )(x, g.reshape(1, d).astype(F32))


def _mm_kernel(a_ref, w_ref, o_ref):
    o_ref[...] = jnp.dot(a_ref[...], w_ref[...], preferred_element_type=F32).astype(o_ref.dtype)


def matmul(a, w, out_dtype, tm=1024, tn=1024):
    m, k = a.shape
    n = w.shape[1]
    tm, tn = min(tm, m), min(tn, n)
    assert m % tm == 0 and n % tn == 0, (m, n, tm, tn)
    est = 2 * (_nbytes((tm, k), a.dtype) + _nbytes((k, tn), w.dtype) + _nbytes((tm, tn), out_dtype)) \
        + _nbytes((tm, tn), F32)
    return pl.pallas_call(
        _mm_kernel,
        grid=(m // tm, n // tn),
        in_specs=[pl.BlockSpec((tm, k), lambda i, j: (i, 0)),
                  pl.BlockSpec((k, tn), lambda i, j: (0, j))],
        out_specs=pl.BlockSpec((tm, tn), lambda i, j: (i, j)),
        out_shape=jax.ShapeDtypeStruct((m, n), out_dtype),
        compiler_params=_params(("parallel", "parallel"), est),
    )(a, w)


def _mm_res_norm_kernel(a_ref, w_ref, r_ref, g_ref, h_ref, hn_ref):
    h = r_ref[...] + jnp.dot(a_ref[...], w_ref[...], preferred_element_type=F32)
    h_ref[...] = h
    hn_ref[...] = _rms(h, g_ref[...]).astype(hn_ref.dtype)


def matmul_res_norm(a, w, res, g, tm=512):
    m, k = a.shape
    n = w.shape[1]
    est = 2 * (_nbytes((tm, k), a.dtype) + 2 * _nbytes((tm, n), F32) + _nbytes((tm, n), BF16)) \
        + _nbytes((k, n), w.dtype) + 2 * _nbytes((tm, n), F32)
    return pl.pallas_call(
        _mm_res_norm_kernel,
        grid=(m // tm,),
        in_specs=[pl.BlockSpec((tm, k), lambda i: (i, 0)),
                  pl.BlockSpec((k, n), lambda i: (0, 0), pipeline_mode=pl.Buffered(1)),
                  pl.BlockSpec((tm, n), lambda i: (i, 0)),
                  pl.BlockSpec((1, n), lambda i: (0, 0))],
        out_specs=[pl.BlockSpec((tm, n), lambda i: (i, 0)),
                   pl.BlockSpec((tm, n), lambda i: (i, 0))],
        out_shape=[jax.ShapeDtypeStruct((m, n), F32), jax.ShapeDtypeStruct((m, n), BF16)],
        compiler_params=_params(("parallel",), est),
    )(a, w, res, g.reshape(1, n).astype(F32))


def _half_swap(y, c, s):
    return y * c + pltpu.roll(y, LANES // 2, 1) * s


def _mla_prep_kernel(z_ref, gq_ref, gkv_ref, wq_ref, wkv_ref, c_ref, s_ref, q_ref, k_ref, v_ref, *, scale):
    z = z_ref[...].astype(F32)
    c = c_ref[...]
    s = s_ref[...]
    nq = _rms(z[:, :MLA_LORA], gq_ref[...]).astype(BF16)
    nkv = _rms(z[:, MLA_LORA:2 * MLA_LORA], gkv_ref[...]).astype(BF16)
    q = jnp.dot(nq, wq_ref[...], preferred_element_type=F32)
    kv = jnp.dot(nkv, wkv_ref[...], preferred_element_type=F32)
    k_pe = _half_swap(z[:, 2 * MLA_LORA:], c, s).astype(BF16)
    for h in range(MLA_HEADS):
        lo = 2 * LANES * h
        q_ref[:, lo:lo + LANES] = (q[:, lo:lo + LANES] * scale).astype(BF16)
        q_ref[:, lo + LANES:lo + 2 * LANES] = (_half_swap(q[:, lo + LANES:lo + 2 * LANES], c, s) * scale).astype(BF16)
        k_ref[:, lo:lo + LANES] = kv[:, LANES * h:LANES * (h + 1)].astype(BF16)
        k_ref[:, lo + LANES:lo + 2 * LANES] = k_pe
    v_ref[...] = kv[:, MLA_HEADS * MLA_NOPE:].astype(BF16)


def mla_prep(z1, g_cq, g_ckv, wq, wkv, cos_t, sin_t, tm=512):
    m, zc = z1.shape
    nq, nkv = wq.shape[1], wkv.shape[1]
    scale = float((MLA_NOPE + MLA_ROPE) ** -0.5)
    row = lambda c: pl.BlockSpec((tm, c), lambda i: (i, 0))
    full = lambda a: pl.BlockSpec(a.shape, lambda i: (0, 0))
    g_cq = g_cq.reshape(1, -1).astype(F32)
    g_ckv = g_ckv.reshape(1, -1).astype(F32)
    est = 4 * _nbytes((tm, nq), F32) + 4 * (_nbytes(wq.shape, BF16) + _nbytes(wkv.shape, BF16))
    return pl.pallas_call(
        functools.partial(_mla_prep_kernel, scale=scale),
        grid=(m // tm,),
        in_specs=[row(zc), full(g_cq), full(g_ckv), full(wq), full(wkv), row(LANES), row(LANES)],
        out_specs=[row(nq), row(nq), row(MLA_HEADS * MLA_V)],
        out_shape=[jax.ShapeDtypeStruct((m, nq), BF16), jax.ShapeDtypeStruct((m, nq), BF16),
                   jax.ShapeDtypeStruct((m, MLA_HEADS * MLA_V), BF16)],
        compiler_params=_params(("parallel",), est),
    )(z1, g_cq, g_ckv, wq, wkv, cos_t, sin_t)


def _nt_dot(a, b):
    return lax.dot_general(a, b, (((1,), (1,)), ((), ())), preferred_element_type=F32)


def _softmax_tiles_keymajor(scores, load_vt, carry):
    p, stats = [], []
    for s, (m, l, _) in zip(scores, carry):
        m_new = jnp.maximum(m, jnp.max(s, axis=0, keepdims=True))
        alpha = jnp.exp(m - m_new)
        ph = jnp.exp(s - m_new)
        stats.append((m_new, alpha, alpha * l + jnp.sum(ph, axis=0, keepdims=True)))
        p.append(ph.astype(BF16))
    out = []
    for h, ((m_new, alpha, l_new), (_, _, acc)) in enumerate(zip(stats, carry)):
        acc = alpha * acc
        for n, vt in enumerate(load_vt(h)):
            acc = acc + jnp.dot(vt, p[h][n * vt.shape[1]:(n + 1) * vt.shape[1], :], preferred_element_type=F32)
        out.append((m_new, l_new, acc))
    return out


def _softmax_init_keymajor(queries, dv):
    return (jnp.full((1, queries), MASK_VALUE, F32), jnp.zeros((1, queries), F32), jnp.zeros((dv, queries), F32))


def _transposed_bf16(x):
    return x.astype(F32).T.astype(BF16)


def _softmax_tiles(scores, load_v, carry):
    p, stats = [], []
    for s, (m, l, _) in zip(scores, carry):
        m_new = jnp.maximum(m, jnp.max(s, axis=1, keepdims=True))
        alpha = jnp.exp(m - m_new)
        ph = jnp.exp(s - m_new)
        stats.append((m_new, alpha, alpha * l + jnp.sum(ph, axis=1, keepdims=True)))
        p.append(ph.astype(BF16))
    return [(m_new, l_new, alpha * acc + jnp.dot(p[h], load_v(h), preferred_element_type=F32))
            for h, ((m_new, alpha, l_new), (_, _, acc)) in enumerate(zip(stats, carry))]


def _softmax_init(rows, dv):
    return (jnp.full((rows, 1), MASK_VALUE, F32), jnp.zeros((rows, 1), F32), jnp.zeros((rows, dv), F32))


def _causal_attn_kernel(q_ref, k_ref, v_ref, o_ref, *, tq, tk, dk, dv, hpg):
    i = pl.program_id(2)
    n_full = (i * tq) // tk
    heads = range(hpg)
    q = [q_ref[:, h * dk:(h + 1) * dk] for h in heads]

    def tile(j, carry, mask):
        off = pl.multiple_of(j * tk, tk)
        s = [_nt_dot(q[h], k_ref[pl.ds(off, tk), h * dk:(h + 1) * dk]) for h in heads]
        if mask is not None:
            s = [jnp.where(mask, sh, MASK_VALUE) for sh in s]
        return _softmax_tiles(s, lambda h: v_ref[pl.ds(off, tk), h * dv:(h + 1) * dv], carry)

    carry = lax.fori_loop(0, n_full, lambda j, c: tile(j, c, None), [_softmax_init(tq, dv) for _ in heads])
    qpos = i * tq + lax.broadcasted_iota(jnp.int32, (tq, tk), 0)
    kpos = n_full * tk + lax.broadcasted_iota(jnp.int32, (tq, tk), 1)
    carry = tile(n_full, carry, kpos <= qpos)
    for h in heads:
        _, l, acc = carry[h]
        o_ref[:, h * dv:(h + 1) * dv] = (acc / l).astype(o_ref.dtype)


def causal_attention(q, k, v, batch, heads, dk, dv, tq=256, tk=1024, hpg=3):
    t = q.shape[0]
    s = t // batch
    nq = s // tq
    assert tk % tq == 0 and s % tk == 0 and heads % hpg == 0
    est = 4 * hpg * (_nbytes((s, dk), BF16) + _nbytes((s, dv), BF16)) + 8 * hpg * _nbytes((tq, tk), F32)
    return pl.pallas_call(
        functools.partial(_causal_attn_kernel, tq=tq, tk=tk, dk=dk, dv=dv, hpg=hpg),
        grid=(batch, heads // hpg, nq),
        in_specs=[pl.BlockSpec((tq, hpg * dk), lambda b, g, i: (b * nq + i, g)),
                  pl.BlockSpec((s, hpg * dk), lambda b, g, i: (b, g)),
                  pl.BlockSpec((s, hpg * dv), lambda b, g, i: (b, g))],
        out_specs=pl.BlockSpec((tq, hpg * dv), lambda b, g, i: (b * nq + i, g)),
        out_shape=jax.ShapeDtypeStruct((t, heads * dv), BF16),
        compiler_params=_params(("parallel", "parallel", "arbitrary"), est),
    )(q, k, v)


def _moba_select(gate_t, i, nb):
    nq = gate_t.shape[1]
    row = lax.broadcasted_iota(jnp.int32, (nb, nq), 0)
    rank = jnp.zeros((nb, nq), jnp.int32)
    for jj in range(nb):
        gj = gate_t[jj:jj + 1, :]
        ahead = jnp.logical_or(gj > gate_t, jnp.logical_and(gj == gate_t, jj < row))
        rank = rank + jnp.where(jnp.logical_and(ahead, jj < i), 1, 0)
    return jnp.where(jnp.logical_and(row < i, rank < MOBA_TOPK), 1.0, 0.0)


def _moba_kernel(q_ref, k_ref, v_ref, cq_ref, sq_ref, ck_ref, sk_ref, o_ref, kr_scr, vt_scr, km_scr, sel_scr,
                 *, nb, nh, scale):
    i = pl.program_id(1)
    blk = MOBA_BLOCK
    d = HEAD_DIM
    heads = range(nh)

    @pl.when(i == 0)
    def _():
        def prep(j, _):
            rows = pl.ds(pl.multiple_of(j * blk, blk), blk)
            c, s = ck_ref[rows, :], sk_ref[rows, :]
            for h in heads:
                kj = _half_swap(k_ref[rows, h * d:(h + 1) * d].astype(F32), c, s)
                km_scr[pl.ds(h * nb + j, 1), :] = jnp.mean(kj, axis=0, keepdims=True)
                kr_scr[rows, h * d:(h + 1) * d] = kj.astype(BF16)
                vt_scr[j, h * d:(h + 1) * d, :] = _transposed_bf16(v_ref[rows, h * d:(h + 1) * d])
            return 0
        lax.fori_loop(0, nb, prep, 0)

    cq, sq = cq_ref[...], sq_ref[...]
    qb = []
    for h in heads:
        q = _half_swap(q_ref[:, h * d:(h + 1) * d].astype(F32), cq, sq)
        gate_t = lax.dot_general(km_scr[h * nb:(h + 1) * nb, :], q, (((1,), (1,)), ((), ())),
                                 precision=lax.Precision.HIGHEST, preferred_element_type=F32)
        sel_scr[h * nb:(h + 1) * nb, :] = _moba_select(gate_t, i, nb)
        qb.append((q * scale).astype(BF16))

    def tile(j, nblk, carry, bias_fn):
        rows = pl.ds(pl.multiple_of(j * blk, blk), nblk * blk)
        s = [_nt_dot(kr_scr[rows, h * d:(h + 1) * d], qb[h]) + bias_fn(h) for h in heads]
        return _softmax_tiles_keymajor(
            s, lambda h: [vt_scr[j + n, h * d:(h + 1) * d, :] for n in range(nblk)], carry)

    key_id = lax.broadcasted_iota(jnp.int32, (blk, blk), 0)
    qry_id = lax.broadcasted_iota(jnp.int32, (blk, blk), 1)
    causal = jnp.where(key_id <= qry_id, 0.0, MASK_VALUE)
    carry = tile(i, 1, [_softmax_init_keymajor(blk, d) for _ in heads], lambda h: causal)

    def body(p, carry):
        def bias(h):
            rows = [jnp.broadcast_to((1.0 - sel_scr[pl.ds(h * nb + 2 * p + n, 1), :]) * MASK_VALUE, (blk, blk))
                    for n in range(2)]
            return jnp.concatenate(rows, axis=0)
        return tile(2 * p, 2, carry, bias)

    carry = lax.fori_loop(0, (i + 1) // 2, body, carry)
    for h in heads:
        _, l, acc = carry[h]
        o_ref[:, h * d:(h + 1) * d] = (acc / l).T.astype(o_ref.dtype)


def moba_attention(zbc, cos_t, sin_t, batch, group0):
    t = zbc.shape[0]
    s = t // batch
    blk = MOBA_BLOCK
    nb = s // blk
    nh = MOBA_HEADS
    w = nh * HEAD_DIM
    est = 6 * _nbytes((s, w), BF16) + 4 * _nbytes((s, HEAD_DIM), F32) + 12 * nh * _nbytes((blk, blk), F32)
    return pl.pallas_call(
        functools.partial(_moba_kernel, nb=nb, nh=nh, scale=float(HEAD_DIM ** -0.5)),
        grid=(batch, nb),
        in_specs=[pl.BlockSpec((blk, w), lambda b, i: (b * nb + i, group0)),
                  pl.BlockSpec((s, w), lambda b, i: (b, group0 + 1)),
                  pl.BlockSpec((s, w), lambda b, i: (b, group0 + 2)),
                  pl.BlockSpec((blk, HEAD_DIM), lambda b, i: (b * nb + i, 0)),
                  pl.BlockSpec((blk, HEAD_DIM), lambda b, i: (b * nb + i, 0)),
                  pl.BlockSpec((s, HEAD_DIM), lambda b, i: (b, 0)),
                  pl.BlockSpec((s, HEAD_DIM), lambda b, i: (b, 0))],
        out_specs=pl.BlockSpec((blk, w), lambda b, i: (b * nb + i, 0)),
        out_shape=jax.ShapeDtypeStruct((t, w), BF16),
        scratch_shapes=[pltpu.VMEM((s, w), BF16), pltpu.VMEM((nb, w, blk), BF16),
                        pltpu.VMEM((nh * nb, HEAD_DIM), F32), pltpu.VMEM((nh * nb, blk), F32)],
        compiler_params=_params(("parallel", "arbitrary"), est),
    )(zbc, zbc, zbc, cos_t, sin_t, cos_t, sin_t)


def _sb_kernel(q_ref, k_ref, v_ref, u_ref, o_ref, *, tq, nh, scale):
    i = pl.program_id(1)
    d = HEAD_DIM
    heads = range(nh)
    u = u_ref[...]
    qb = [(q_ref[:, h * d:(h + 1) * d].astype(F32) * scale).astype(BF16) for h in heads]

    def tile(j, carry, strict):
        off = pl.multiple_of(j * tq, tq)
        z = [_nt_dot(qb[h], k_ref[pl.ds(off, tq), h * d:(h + 1) * d]) for h in heads]
        hi, lo = [], []
        for h in heads:
            lsm = -(jnp.maximum(z[h], 0.0) + jnp.log(1.0 + jnp.exp(-jnp.abs(z[h]))))
            if strict is not None:
                lsm = jnp.where(strict, lsm, 0.0)
            lsm_hi, lsm_lo = _split_bf16(lsm)
            hi.append(lsm_hi)
            lo.append(lsm_lo)
        incl = [jnp.dot(hi[h], u, preferred_element_type=F32) + jnp.dot(lo[h], u, preferred_element_type=F32)
                for h in heads]
        a = []
        for h in heads:
            ah = jnp.exp(jnp.minimum(z[h] + incl[h], 0.0) + carry[h][0])
            if strict is not None:
                ah = jnp.where(strict, ah, 0.0)
            a.append(ah.astype(BF16))
        return [(carry[h][0] + incl[h][:, 0:1],
                 carry[h][1] + jnp.dot(a[h], v_ref[pl.ds(off, tq), h * d:(h + 1) * d], preferred_element_type=F32))
                for h in heads]

    r_id = lax.broadcasted_iota(jnp.int32, (tq, tq), 0)
    c_id = lax.broadcasted_iota(jnp.int32, (tq, tq), 1)
    init = [(jnp.zeros((tq, 1), F32), jnp.zeros((tq, d), F32)) for _ in heads]
    carry = tile(i, init, c_id < r_id)

    def live(carry):
        worst = carry[0][0]
        for h in heads[1:]:
            worst = jnp.maximum(worst, carry[h][0])
        return (jnp.max(worst) > EXP_UNDERFLOW).astype(jnp.int32)

    def body(state):
        n, _, carry = state
        carry = tile(i - 1 - n, carry, None)
        return n + 1, live(carry), carry

    _, _, carry = lax.while_loop(lambda st: jnp.logical_and(st[0] < i, st[1] > 0), body,
                                 (jnp.int32(0), live(carry), carry))
    for h in heads:
        o_ref[:, h * d:(h + 1) * d] = carry[h][1].astype(o_ref.dtype)


def stick_breaking_attention(zbc, batch, group0, tq=256):
    t = zbc.shape[0]
    s = t // batch
    nq = s // tq
    nh = SB_HEADS
    w = nh * HEAD_DIM
    u = (jnp.arange(tq)[:, None] >= jnp.arange(tq)[None, :]).astype(BF16)
    est = 8 * _nbytes((s, w), BF16) + 16 * nh * _nbytes((tq, tq), F32)
    return pl.pallas_call(
        functools.partial(_sb_kernel, tq=tq, nh=nh, scale=float(HEAD_DIM ** -0.5)),
        grid=(batch, nq),
        in_specs=[pl.BlockSpec((tq, w), lambda b, i: (b * nq + i, group0)),
                  pl.BlockSpec((s, w), lambda b, i: (b, group0 + 1)),
                  pl.BlockSpec((s, w), lambda b, i: (b, group0 + 2)),
                  pl.BlockSpec((tq, tq), lambda b, i: (0, 0))],
        out_specs=pl.BlockSpec((tq, w), lambda b, i: (b * nq + i, 0)),
        out_shape=jax.ShapeDtypeStruct((t, w), BF16),
        compiler_params=_params(("parallel", "arbitrary"), est),
    )(zbc, zbc, zbc, u)


def _merge_kernel(oa_ref, ob_ref, oc_ref, ga_ref, gb_ref, gc_ref, wa_ref, wb_ref, wc_ref, o_ref):
    def branch(o, g, w):
        return jax.nn.sigmoid(g[...].astype(F32)) * jnp.dot(o[...], w[...], preferred_element_type=F32)
    o_ref[...] = (branch(oa_ref, ga_ref, wa_ref) + branch(ob_ref, gb_ref, wb_ref)
                  + branch(oc_ref, gc_ref, wc_ref)).astype(o_ref.dtype)


def gated_merge(o_a, o_b, o_c, zg, wa, wb, wc, tm=512):
    m = o_a.shape[0]
    d = wa.shape[1]
    row = lambda a: pl.BlockSpec((tm, a.shape[1]), lambda i: (i, 0))
    full = lambda a: pl.BlockSpec(a.shape, lambda i: (0, 0))
    gate = lambda n: pl.BlockSpec((tm, d), lambda i: (i, n))
    est = 4 * _nbytes((d, d), BF16) + 12 * _nbytes((tm, d), F32)
    return pl.pallas_call(
        _merge_kernel,
        grid=(m // tm,),
        in_specs=[row(o_a), row(o_b), row(o_c), gate(0), gate(1), gate(2), full(wa), full(wb), full(wc)],
        out_specs=pl.BlockSpec((tm, d), lambda i: (i, 0)),
        out_shape=jax.ShapeDtypeStruct((m, d), BF16),
        compiler_params=_params(("parallel",), est),
    )(o_a, o_b, o_c, zg, zg, zg, wa, wb, wc)


def _xattn_kernel(q_ref, k_ref, v_ref, o_ref, *, heads, scale):
    hd = q_ref.shape[1] // heads
    for h in range(heads):
        cols = slice(h * hd, (h + 1) * hd)
        s = _nt_dot(q_ref[:, cols], k_ref[:, cols]) * scale
        p = jnp.exp(s - jnp.max(s, axis=1, keepdims=True))
        o = jnp.dot(p.astype(BF16), v_ref[:, cols], preferred_element_type=F32)
        o_ref[:, cols] = (o / jnp.sum(p, axis=1, keepdims=True)).astype(o_ref.dtype)


def cross_attention(q, k, v, batch, heads, tq=512):
    t, d = q.shape
    s = t // batch
    mlen = k.shape[0] // batch
    nq = s // tq
    est = 8 * _nbytes((tq, d), BF16) + 8 * _nbytes((mlen, d), BF16) + 8 * _nbytes((tq, mlen), F32)
    return pl.pallas_call(
        functools.partial(_xattn_kernel, heads=heads, scale=float((d // heads) ** -0.5)),
        grid=(batch, nq),
        in_specs=[pl.BlockSpec((tq, d), lambda b, i: (b * nq + i, 0)),
                  pl.BlockSpec((mlen, d), lambda b, i: (b, 0)),
                  pl.BlockSpec((mlen, d), lambda b, i: (b, 0))],
        out_specs=pl.BlockSpec((tq, d), lambda b, i: (b * nq + i, 0)),
        out_shape=jax.ShapeDtypeStruct((t, d), BF16),
        compiler_params=_params(("parallel", "parallel"), est),
    )(q, k, v)


def _swiglu_up(x_ref, w1_ref, w3_ref, o_ref):
    x = x_ref[...]
    a = jnp.dot(x, w1_ref[...], preferred_element_type=F32)
    b = jnp.dot(x, w3_ref[...], preferred_element_type=F32)
    o_ref[...] = (a * jax.nn.sigmoid(a) * b).astype(BF16)


def _ffn_up_kernel(x_ref, w1_ref, w3_ref, o_ref):
    _swiglu_up(x_ref, w1_ref, w3_ref, o_ref)


def dense_ffn(x, w1, w3, w2, res, g_next, tm_up=1024, tf=512, tm_down=256):
    m, d = x.shape
    f = w1.shape[1]
    assert f % tf == 0
    est = 2 * (_nbytes((tm_up, d), BF16) + 2 * _nbytes((d, tf), BF16) + _nbytes((tm_up, tf), BF16)) \
        + 4 * _nbytes((tm_up, tf), F32)
    act = pl.pallas_call(
        _ffn_up_kernel,
        grid=(m // tm_up, f // tf),
        in_specs=[pl.BlockSpec((tm_up, d), lambda i, j: (i, 0)),
                  pl.BlockSpec((d, tf), lambda i, j: (0, j)),
                  pl.BlockSpec((d, tf), lambda i, j: (0, j))],
        out_specs=pl.BlockSpec((tm_up, tf), lambda i, j: (i, j)),
        out_shape=jax.ShapeDtypeStruct((m, f), BF16),
        compiler_params=_params(("parallel", "parallel"), est),
    )(x, w1, w3)
    return matmul_res_norm(act, w2, res, g_next, tm=tm_down)


def _split_bf16(x):
    hi = x.astype(BF16)
    return hi, (x - hi.astype(F32)).astype(BF16)


def _router_kernel(h_ref, g_ref, wh_ref, wl_ref, u_ref, idx_ref, gate_ref, rank_ref, count_ref, seen_scr):
    @pl.when(pl.program_id(0) == 0)
    def _():
        seen_scr[...] = jnp.zeros_like(seen_scr)

    hn = _rms(h_ref[...], g_ref[...])
    n_e = count_ref.shape[0]
    hn_hi, hn_lo = _split_bf16(hn)
    logits = (_nt_dot(hn_hi, wh_ref[...]) + _nt_dot(hn_lo, wh_ref[...]) + _nt_dot(hn_hi, wl_ref[...])).T[:n_e, :]
    e_id = lax.broadcasted_iota(jnp.int32, logits.shape, 0)
    v1 = jnp.max(logits, axis=0, keepdims=True)
    i1 = jnp.min(jnp.where(logits == v1, e_id, n_e), axis=0, keepdims=True)
    rest = jnp.where(e_id == i1, -jnp.inf, logits)
    v2 = jnp.max(rest, axis=0, keepdims=True)
    i2 = jnp.min(jnp.where(rest == v2, e_id, n_e), axis=0, keepdims=True)
    e2 = jnp.exp(v2 - v1)
    idx_ref[0:1, :] = i1
    idx_ref[1:2, :] = i2
    gate_ref[0:1, :] = 1.0 / (1.0 + e2)
    gate_ref[1:2, :] = e2 / (1.0 + e2)
    pick1 = jnp.where(e_id == i1, 1.0, 0.0)
    pick2 = jnp.where(e_id == i2, 1.0, 0.0)
    both = pick1 + pick2
    earlier = jnp.dot(both.astype(BF16), u_ref[...], preferred_element_type=F32) + seen_scr[...]
    rank_ref[0:1, :] = jnp.sum(pick1 * earlier, axis=0, keepdims=True).astype(jnp.int32)
    rank_ref[1:2, :] = jnp.sum(pick2 * earlier, axis=0, keepdims=True).astype(jnp.int32)
    seen_scr[...] += jnp.sum(both, axis=1, keepdims=True)
    count_ref[...] = seen_scr[...].astype(jnp.int32)


def moe_router(h, g, w_router, tm=512):
    m, d = h.shape
    n_e = w_router.shape[1]
    wr_t = jnp.zeros((LANES, d), F32).at[:n_e].set(w_router.T.astype(F32))
    wr_hi, wr_lo = _split_bf16(wr_t)
    u = (jnp.arange(tm)[:, None] < jnp.arange(tm)[None, :]).astype(BF16)
    est = 6 * _nbytes((tm, d), F32)
    return pl.pallas_call(
        _router_kernel,
        grid=(m // tm,),
        in_specs=[pl.BlockSpec((tm, d), lambda i: (i, 0)),
                  pl.BlockSpec((1, d), lambda i: (0, 0)),
                  pl.BlockSpec((LANES, d), lambda i: (0, 0)),
                  pl.BlockSpec((LANES, d), lambda i: (0, 0)),
                  pl.BlockSpec((tm, tm), lambda i: (0, 0))],
        out_specs=[pl.BlockSpec((MOE_TOPK, tm), lambda i: (0, i)),
                   pl.BlockSpec((MOE_TOPK, tm), lambda i: (0, i)),
                   pl.BlockSpec((MOE_TOPK, tm), lambda i: (0, i)),
                   pl.BlockSpec((n_e, 1), lambda i: (0, 0))],
        out_shape=[jax.ShapeDtypeStruct((MOE_TOPK, m), jnp.int32), jax.ShapeDtypeStruct((MOE_TOPK, m), F32),
                   jax.ShapeDtypeStruct((MOE_TOPK, m), jnp.int32), jax.ShapeDtypeStruct((n_e, 1), jnp.int32)],
        scratch_shapes=[pltpu.VMEM((n_e, 1), F32)],
        compiler_params=_params(("arbitrary",), est),
    )(h, g.reshape(1, d).astype(F32), wr_hi, wr_lo, u)


def _row_copy(src_hbm, src_row, dst_ref, dst_row, sem):
    return pltpu.make_async_copy(src_hbm.at[pl.ds(src_row, 1)], dst_ref.at[pl.ds(dst_row, 1)], sem)


PREFETCH_ROWS = 128


def _expert_up_kernel(te_ref, tv_ref, rows_ref, h_hbm, g_ref, w1_ref, w3_ref, o_ref, xbuf, xn_ref, sem):
    t = pl.program_id(0)
    j = pl.program_id(1)
    tm = MOE_TILE
    slot = lax.rem(t, 2)

    def request(tile, first, count):
        dst, dsem = xbuf.at[lax.rem(tile, 2)], sem.at[lax.rem(tile, 2)]

        def start(r, _):
            _row_copy(h_hbm, rows_ref[tile * tm + first + r], dst, first + r, dsem).start()
            return 0
        lax.fori_loop(0, count, start, 0, unroll=8)

    @pl.when(jnp.logical_and(t == 0, j == 0))
    def _():
        request(0, 0, tm)

    nxt = jnp.minimum(t + 1, pl.num_programs(0) - 1)
    @pl.when(jnp.logical_and(jnp.logical_and(t + 1 < pl.num_programs(0), tv_ref[nxt] > 0),
                             j < tm // PREFETCH_ROWS))
    def _():
        request(t + 1, j * PREFETCH_ROWS, PREFETCH_ROWS)

    @pl.when(jnp.logical_and(j == 0, tv_ref[t] > 0))
    def _():
        def wait(r, _):
            _row_copy(h_hbm, 0, xbuf.at[slot], r, sem.at[slot]).wait()
            return 0
        lax.fori_loop(0, tm, wait, 0, unroll=8)
        xn_ref[...] = _rms(xbuf[slot], g_ref[...]).astype(BF16)

    @pl.when(tv_ref[t] > 0)
    def _():
        _swiglu_up(xn_ref, w1_ref, w3_ref, o_ref)

    @pl.when(tv_ref[t] == 0)
    def _():
        o_ref[...] = jnp.zeros_like(o_ref)


def _expert_down_kernel(te_ref, tv_ref, a_ref, w_ref, o_ref):
    t = pl.program_id(0)

    @pl.when(pl.program_id(1) == 0)
    def _():
        o_ref[...] = jnp.zeros_like(o_ref)

    @pl.when(tv_ref[t] > 0)
    def _():
        o_ref[...] += jnp.dot(a_ref[...], w_ref[...], preferred_element_type=F32)


def expert_ffn(h, slot_tok, g, w1, w3, w2, tile_e, tile_valid, tf=512, nk=4):
    d = h.shape[1]
    n = slot_tok.shape[0]
    f = w1.shape[2]
    nf = f // tf
    tk = f // nk
    tm = MOE_TILE
    assert tm % PREFETCH_ROWS == 0 and nf >= tm // PREFETCH_ROWS
    assert f % tf == 0 and f % nk == 0 and tk % LANES == 0

    def hold(last):
        return lambda t, j, tv: j * tv[t] + last * (1 - tv[t])

    ju = hold(nf - 1)
    est = 2 * (_nbytes((tm, d), F32) + 2 * _nbytes((d, tf), BF16) + _nbytes((tm, tf), BF16)) \
        + _nbytes((tm, d), BF16) + 4 * _nbytes((tm, tf), F32) + _nbytes((tm, d), F32)
    act = pl.pallas_call(
        _expert_up_kernel,
        grid_spec=pltpu.PrefetchScalarGridSpec(
            num_scalar_prefetch=3,
            grid=(n // tm, nf),
            in_specs=[pl.BlockSpec(memory_space=pl.ANY),
                      pl.BlockSpec((1, d), lambda t, j, te, tv, rows: (0, 0)),
                      pl.BlockSpec((None, d, tf), lambda t, j, te, tv, rows: (te[t], 0, ju(t, j, tv))),
                      pl.BlockSpec((None, d, tf), lambda t, j, te, tv, rows: (te[t], 0, ju(t, j, tv)))],
            out_specs=pl.BlockSpec((tm, tf), lambda t, j, te, tv, rows: (t, j)),
            scratch_shapes=[pltpu.VMEM((2, tm, d), F32), pltpu.VMEM((tm, d), BF16),
                            pltpu.SemaphoreType.DMA((2,))],
        ),
        out_shape=jax.ShapeDtypeStruct((n, f), BF16),
        compiler_params=_params(("arbitrary", "arbitrary"), est),
    )(tile_e, tile_valid, slot_tok, h, g.reshape(1, d).astype(F32), w1, w3)
    jd = hold(nk - 1)
    est = 2 * (_nbytes((tm, tk), BF16) + _nbytes((tk, d), BF16) + _nbytes((tm, d), F32)) + 2 * _nbytes((tm, d), F32)
    return pl.pallas_call(
        _expert_down_kernel,
        grid_spec=pltpu.PrefetchScalarGridSpec(
            num_scalar_prefetch=2,
            grid=(n // tm, nk),
            in_specs=[pl.BlockSpec((tm, tk), lambda t, k, te, tv: (t, jd(t, k, tv))),
                      pl.BlockSpec((None, tk, d), lambda t, k, te, tv: (te[t], jd(t, k, tv), 0))],
            out_specs=pl.BlockSpec((tm, d), lambda t, k, te, tv: (t, 0)),
        ),
        out_shape=jax.ShapeDtypeStruct((n, d), F32),
        compiler_params=_params(("arbitrary", "arbitrary"), est),
    )(tile_e, tile_valid, act, w2)


def _combine_kernel(slots_ref, ys_hbm, h_ref, gate_ref, g_ref, o_ref, buf0, buf1, sem):
    base = pl.program_id(0) * GATHER_ROWS

    def start(r, _):
        flat = (base + r) * MOE_TOPK
        _row_copy(ys_hbm, slots_ref[flat], buf0, r, sem).start()
        _row_copy(ys_hbm, slots_ref[flat + 1], buf1, r, sem).start()
        return 0

    def wait(r, _):
        _row_copy(ys_hbm, 0, buf0, r, sem).wait()
        _row_copy(ys_hbm, 0, buf1, r, sem).wait()
        return 0

    lax.fori_loop(0, GATHER_ROWS, start, 0, unroll=8)
    lax.fori_loop(0, GATHER_ROWS, wait, 0, unroll=8)
    gate = gate_ref[...]
    h = h_ref[...] + gate[:, 0:1] * buf0[...] + gate[:, 1:2] * buf1[...]
    o_ref[...] = _rms(h, g_ref[...])


def moe_combine_norm(ys, slot_of_flat, h, gates, g_out):
    m, d = h.shape
    tm = GATHER_ROWS
    return pl.pallas_call(
        _combine_kernel,
        grid_spec=pltpu.PrefetchScalarGridSpec(
            num_scalar_prefetch=1,
            grid=(m // tm,),
            in_specs=[pl.BlockSpec(memory_space=pl.ANY),
                      pl.BlockSpec((tm, d), lambda i, s: (i, 0)),
                      pl.BlockSpec((tm, MOE_TOPK), lambda i, s: (i, 0)),
                      pl.BlockSpec((1, d), lambda i, s: (0, 0))],
            out_specs=pl.BlockSpec((tm, d), lambda i, s: (i, 0)),
            scratch_shapes=[pltpu.VMEM((tm, d), F32), pltpu.VMEM((tm, d), F32), pltpu.SemaphoreType.DMA(())],
        ),
        out_shape=jax.ShapeDtypeStruct((m, d), F32),
        compiler_params=_params(("arbitrary",), 8 * _nbytes((tm, d), F32)),
    )(slot_of_flat, ys, h, gates, g_out.reshape(1, d).astype(F32))


def moe_routing_metadata(top_idx, top_rank, counts, n_tokens):
    tk = n_tokens * MOE_TOPK
    flat_e = top_idx.T.reshape(-1)
    within = top_rank.T.reshape(-1)
    padded = (counts + MOE_TILE - 1) // MOE_TILE * MOE_TILE
    pad_end = jnp.cumsum(padded)
    pad_start = pad_end - padded
    slot_of_flat = (jnp.sum(jnp.where(flat_e[:, None] == jnp.arange(N_EXPERTS)[None, :], pad_start[None, :], 0),
                            axis=1) + within).astype(jnp.int32)
    n_tiles = tk // MOE_TILE + N_EXPERTS
    slot_tok = jnp.zeros((n_tiles * MOE_TILE,), jnp.int32).at[slot_of_flat].set(
        jnp.arange(tk, dtype=jnp.int32) // MOE_TOPK)
    tile_start = jnp.arange(n_tiles, dtype=jnp.int32) * MOE_TILE
    tile_valid = (tile_start < pad_end[-1]).astype(jnp.int32)
    tile_e = jnp.minimum(jnp.sum(tile_start[:, None] >= pad_end[None, :], axis=1), N_EXPERTS - 1)
    last_e = jnp.max(jnp.where(tile_valid > 0, tile_e, 0))
    tile_e = jnp.where(tile_valid > 0, tile_e, last_e).astype(jnp.int32)
    return slot_tok, slot_of_flat, tile_e, tile_valid


def _rope_partner(w):
    half = w.shape[-1] // 2
    return jnp.concatenate([-w[..., half:], w[..., :half]], axis=-1)


def _rope_tables(positions):
    pos = positions.astype(F32).reshape(-1, 1)

    def cs(half):
        inv_freq = ROPE_THETA ** (-jnp.arange(half, dtype=F32) / half)
        ang = pos * inv_freq
        return jnp.cos(ang), jnp.sin(ang)

    c32, s32 = cs(MLA_ROPE // 2)
    zeros = jnp.zeros((pos.shape[0], LANES // 2), F32)
    mla_c = jnp.concatenate([c32, c32, zeros], axis=1)
    mla_s = jnp.concatenate([s32, s32, zeros], axis=1)
    c64, s64 = cs(HEAD_DIM // 2)
    moba_c = jnp.concatenate([c64, c64], axis=1)
    moba_s = jnp.concatenate([-s64, s64], axis=1)
    return mla_c, mla_s, moba_c, moba_s


def _layer_weights(w_in, w_uq, w_ukv):
    o = 0
    w_cq = w_in[:, o:o + MLA_LORA]; o += MLA_LORA
    w_ckv = w_in[:, o:o + MLA_LORA]; o += MLA_LORA
    w_kr = w_in[:, o:o + MLA_ROPE]; o += MLA_ROPE
    nbc = 3 * (MOBA_HEADS + SB_HEADS) * HEAD_DIM
    w_bc = w_in[:, o:o + nbc]; o += nbc
    w_g = w_in[:, o:]
    w1 = jnp.concatenate([w_cq, w_ckv, w_kr, _rope_partner(w_kr)], axis=1)
    uq = w_uq.reshape(MLA_LORA, MLA_HEADS, MLA_NOPE + MLA_ROPE)
    uq_rope = uq[..., MLA_NOPE:]
    wq = jnp.concatenate([uq[..., :MLA_NOPE], uq_rope, _rope_partner(uq_rope)], axis=-1)
    wq = wq.reshape(MLA_LORA, MLA_HEADS * 2 * LANES)
    ukv = w_ukv.reshape(MLA_LORA, MLA_HEADS, MLA_NOPE + MLA_V)
    wkv = jnp.concatenate([ukv[..., :MLA_NOPE].reshape(MLA_LORA, -1), ukv[..., MLA_NOPE:].reshape(MLA_LORA, -1)],
                          axis=1)
    return w1, w_bc, w_g, wq, wkv


def kernel(x, mem, positions, g_mix, w_in, g_cq, g_ckv, w_uq, w_ukv, w_up_a, w_up_b, w_up_c, w_o, g_x, g_mem, w_xq, w_xk, w_xv, w_xo, g_ffn, w_ff1, w_ff3, w_ff2, w_router, w_e1, w_e3, w_e2, g_final):
    batch, seq, d = x.shape
    t = batch * seq
    depth = g_mix.shape[0]
    assert depth == 2, "the final norm is fused into the expert layer, which must come last"
    mem2 = mem.reshape(-1, d)
    mla_c, mla_s, moba_c, moba_s = _rope_tables(positions)
    assert MOBA_HEADS == SB_HEADS

    w_in, w_uq, w_ukv = w_in.astype(BF16), w_uq.astype(BF16), w_ukv.astype(BF16)

    h = x.reshape(t, d)
    hn = rmsnorm(h, g_mix[0], BF16)
    out = None
    for l in range(depth):
        w1, w_bc, w_g, wq, wkv = _layer_weights(w_in[l], w_uq[l], w_ukv[l])
        z1 = matmul(hn, w1, BF16, tn=w1.shape[1])
        zbc = matmul(hn, w_bc, BF16, tn=1280)
        zg = matmul(hn, w_g, BF16, tm=2048, tn=1024)
        q_a, k_a, v_a = mla_prep(z1, g_cq[l], g_ckv[l], wq, wkv, mla_c, mla_s)
        o_a = causal_attention(q_a, k_a, v_a, batch, MLA_HEADS, 2 * LANES, MLA_V)
        o_b = moba_attention(zbc, moba_c, moba_s, batch, 0)
        o_c = stick_breaking_attention(zbc, batch, 3)
        merged = gated_merge(o_a, o_b, o_c, zg, w_up_a[l].astype(BF16), w_up_b[l].astype(BF16),
                             w_up_c[l].astype(BF16))
        h, hn = matmul_res_norm(merged, w_o[l].astype(BF16), h, g_x[l])
        mn = rmsnorm(mem2, g_mem[l], BF16)
        q_x = matmul(hn, w_xq[l].astype(BF16), BF16, tm=2048)
        k_x = matmul(mn, w_xk[l].astype(BF16), BF16)
        v_x = matmul(mn, w_xv[l].astype(BF16), BF16)
        o_x = cross_attention(q_x, k_x, v_x, batch, XATTN_HEADS)
        h, hn = matmul_res_norm(o_x, w_xo[l].astype(BF16), h, g_ffn[l])
        g_next = g_mix[l + 1] if l + 1 < depth else g_final
        if l % 2 == 0:
            e = l // 2
            h, hn = dense_ffn(hn, w_ff1[e].astype(BF16), w_ff3[e].astype(BF16), w_ff2[e].astype(BF16), h, g_next)
            out = hn
        else:
            e = l // 2
            top_idx, top_gate, top_rank, counts = moe_router(h, g_ffn[l], w_router[e])
            slot_tok, slot_of_flat, tile_e, tile_valid = moe_routing_metadata(top_idx, top_rank, counts[:, 0], t)
            ys = expert_ffn(h, slot_tok, g_ffn[l],w_e1[e].astype(BF16), w_e3[e].astype(BF16), w_e2[e].astype(BF16),
                            tile_e, tile_valid)
            out = moe_combine_norm(ys, slot_of_flat, h, top_gate.T, g_next)
    return out.reshape(batch, seq, d).astype(x.dtype)
```

```python
import functools

import numpy as np
import jax
import jax.numpy as jnp
from jax import lax
from jax.experimental import pallas as pl
from jax.experimental.pallas import tpu as pltpu

F32 = jnp.float32
BF16 = jnp.bfloat16

HEAD_DIM = 128
MLA_HEADS = 6
MLA_LORA = 512
MLA_NOPE = 128
MLA_ROPE = 64
MLA_V = 128
MOBA_HEADS = 5
MOBA_BLOCK = 256
MOBA_TOPK = 3
SB_HEADS = 5
N_BRANCH = 3
ROPE_THETA = 10000.0
XATTN_HEADS = 4
N_EXPERTS = 8
MOE_TOPK = 2
RMS_EPS = 1e-6

LANES = 128
V7X_VMEM_BYTES = 64 * 1024 * 1024
VMEM_CEILING = V7X_VMEM_BYTES - 8 * 1024 * 1024

MASK_VALUE = -1e30
EXP_UNDERFLOW = -104.0

MOE_TILE = 512
GATHER_ROWS = 256


def _params(sem, est_bytes):
    limit = int(min(VMEM_CEILING, max(32 * 1024 * 1024, est_bytes * 5 // 4)))
    return pltpu.CompilerParams(dimension_semantics=sem, vmem_limit_bytes=limit)


def _nbytes(shape, dtype):
    return int(np.prod(shape)) * jnp.dtype(dtype).itemsize


def _rms(x, g):
    return x * lax.rsqrt(jnp.mean(x * x, axis=-1, keepdims=True) + RMS_EPS) * g


def _rmsnorm_kernel(x_ref, g_ref, o_ref):
    o_ref[...] = _rms(x_ref[...].astype(F32), g_ref[...]).astype(o_ref.dtype)


def rmsnorm(x, g, out_dtype, tm=512):
    m, d = x.shape
    tm = min(tm, m)
    return pl.pallas_call(
        _rmsnorm_kernel,
        grid=(m // tm,),
        in_specs=[pl.BlockSpec((tm, d), lambda i: (i, 0)),
                  pl.BlockSpec((1, d), lambda i: (0, 0))],
        out_specs=pl.BlockSpec((tm, d), lambda i: (i, 0)),
        out_shape=jax.ShapeDtypeStruct((m, d), out_dtype),
        compiler_params=_params(("parallel",), 4 * _nbytes((tm, d), F32)),
    )(x, g.reshape(1, d).astype(F32))


def _mm_kernel(a_ref, w_ref, o_ref):
    o_ref[...] = jnp.dot(a_ref[...], w_ref[...], preferred_element_type=F32).astype(o_ref.dtype)


def matmul(a, w, out_dtype, tm=1024, tn=1024):
    m, k = a.shape
    n = w.shape[1]
    tm, tn = min(tm, m), min(tn, n)
    assert m % tm == 0 and n % tn == 0, (m, n, tm, tn)
    est = 2 * (_nbytes((tm, k), a.dtype) + _nbytes((k, tn), w.dtype) + _nbytes((tm, tn), out_dtype)) \
        + _nbytes((tm, tn), F32)
    return pl.pallas_call(
        _mm_kernel,
        grid=(m // tm, n // tn),
        in_specs=[pl.BlockSpec((tm, k), lambda i, j: (i, 0)),
                  pl.BlockSpec((k, tn), lambda i, j: (0, j))],
        out_specs=pl.BlockSpec((tm, tn), lambda i, j: (i, j)),
        out_shape=jax.ShapeDtypeStruct((m, n), out_dtype),
        compiler_params=_params(("parallel", "parallel"), est),
    )(a, w)


def _mm_res_norm_kernel(a_ref, w_ref, r_ref, g_ref, h_ref, hn_ref):
    h = r_ref[...] + jnp.dot(a_ref[...], w_ref[...], preferred_element_type=F32)
    h_ref[...] = h
    hn_ref[...] = _rms(h, g_ref[...]).astype(hn_ref.dtype)


def matmul_res_norm(a, w, res, g, tm=512):
    m, k = a.shape
    n = w.shape[1]
    est = 2 * (_nbytes((tm, k), a.dtype) + 2 * _nbytes((tm, n), F32) + _nbytes((tm, n), BF16)) \
        + _nbytes((k, n), w.dtype) + 2 * _nbytes((tm, n), F32)
    return pl.pallas_call(
        _mm_res_norm_kernel,
        grid=(m // tm,),
        in_specs=[pl.BlockSpec((tm, k), lambda i: (i, 0)),
                  pl.BlockSpec((k, n), lambda i: (0, 0), pipeline_mode=pl.Buffered(1)),
                  pl.BlockSpec((tm, n), lambda i: (i, 0)),
                  pl.BlockSpec((1, n), lambda i: (0, 0))],
        out_specs=[pl.BlockSpec((tm, n), lambda i: (i, 0)),
                   pl.BlockSpec((tm, n), lambda i: (i, 0))],
        out_shape=[jax.ShapeDtypeStruct((m, n), F32), jax.ShapeDtypeStruct((m, n), BF16)],
        compiler_params=_params(("parallel",), est),
    )(a, w, res, g.reshape(1, n).astype(F32))


def _half_swap(y, c, s):
    return y * c + pltpu.roll(y, LANES // 2, 1) * s


def _mla_prep_kernel(z_ref, gq_ref, gkv_ref, wq_ref, wkv_ref, c_ref, s_ref, q_ref, k_ref, v_ref, *, scale):
    z = z_ref[...].astype(F32)
    c = c_ref[...]
    s = s_ref[...]
    nq = _rms(z[:, :MLA_LORA], gq_ref[...]).astype(BF16)
    nkv = _rms(z[:, MLA_LORA:2 * MLA_LORA], gkv_ref[...]).astype(BF16)
    q = jnp.dot(nq, wq_ref[...], preferred_element_type=F32)
    kv = jnp.dot(nkv, wkv_ref[...], preferred_element_type=F32)
    k_pe = _half_swap(z[:, 2 * MLA_LORA:], c, s).astype(BF16)
    for h in range(MLA_HEADS):
        lo = 2 * LANES * h
        q_ref[:, lo:lo + LANES] = (q[:, lo:lo + LANES] * scale).astype(BF16)
        q_ref[:, lo + LANES:lo + 2 * LANES] = (_half_swap(q[:, lo + LANES:lo + 2 * LANES], c, s) * scale).astype(BF16)
        k_ref[:, lo:lo + LANES] = kv[:, LANES * h:LANES * (h + 1)].astype(BF16)
        k_ref[:, lo + LANES:lo + 2 * LANES] = k_pe
    v_ref[...] = kv[:, MLA_HEADS * MLA_NOPE:].astype(BF16)


def mla_prep(z1, g_cq, g_ckv, wq, wkv, cos_t, sin_t, tm=512):
    m, zc = z1.shape
    nq, nkv = wq.shape[1], wkv.shape[1]
    scale = float((MLA_NOPE + MLA_ROPE) ** -0.5)
    row = lambda c: pl.BlockSpec((tm, c), lambda i: (i, 0))
    full = lambda a: pl.BlockSpec(a.shape, lambda i: (0, 0))
    g_cq = g_cq.reshape(1, -1).astype(F32)
    g_ckv = g_ckv.reshape(1, -1).astype(F32)
    est = 4 * _nbytes((tm, nq), F32) + 4 * (_nbytes(wq.shape, BF16) + _nbytes(wkv.shape, BF16))
    return pl.pallas_call(
        functools.partial(_mla_prep_kernel, scale=scale),
        grid=(m // tm,),
        in_specs=[row(zc), full(g_cq), full(g_ckv), full(wq), full(wkv), row(LANES), row(LANES)],
        out_specs=[row(nq), row(nq), row(MLA_HEADS * MLA_V)],
        out_shape=[jax.ShapeDtypeStruct((m, nq), BF16), jax.ShapeDtypeStruct((m, nq), BF16),
                   jax.ShapeDtypeStruct((m, MLA_HEADS * MLA_V), BF16)],
        compiler_params=_params(("parallel",), est),
    )(z1, g_cq, g_ckv, wq, wkv, cos_t, sin_t)


def _nt_dot(a, b):
    return lax.dot_general(a, b, (((1,), (1,)), ((), ())), preferred_element_type=F32)


def _softmax_tiles_keymajor(scores, load_vt, carry):
    p, stats = [], []
    for s, (m, l, _) in zip(scores, carry):
        m_new = jnp.maximum(m, jnp.max(s, axis=0, keepdims=True))
        alpha = jnp.exp(m - m_new)
        ph = jnp.exp(s - m_new)
        stats.append((m_new, alpha, alpha * l + jnp.sum(ph, axis=0, keepdims=True)))
        p.append(ph.astype(BF16))
    out = []
    for h, ((m_new, alpha, l_new), (_, _, acc)) in enumerate(zip(stats, carry)):
        acc = alpha * acc
        for n, vt in enumerate(load_vt(h)):
            acc = acc + jnp.dot(vt, p[h][n * vt.shape[1]:(n + 1) * vt.shape[1], :], preferred_element_type=F32)
        out.append((m_new, l_new, acc))
    return out


def _softmax_init_keymajor(queries, dv):
    return (jnp.full((1, queries), MASK_VALUE, F32), jnp.zeros((1, queries), F32), jnp.zeros((dv, queries), F32))


def _transposed_bf16(x):
    return x.astype(F32).T.astype(BF16)


def _softmax_tiles(scores, load_v, carry):
    p, stats = [], []
    for s, (m, l, _) in zip(scores, carry):
        m_new = jnp.maximum(m, jnp.max(s, axis=1, keepdims=True))
        alpha = jnp.exp(m - m_new)
        ph = jnp.exp(s - m_new)
        stats.append((m_new, alpha, alpha * l + jnp.sum(ph, axis=1, keepdims=True)))
        p.append(ph.astype(BF16))
    return [(m_new, l_new, alpha * acc + jnp.dot(p[h], load_v(h), preferred_element_type=F32))
            for h, ((m_new, alpha, l_new), (_, _, acc)) in enumerate(zip(stats, carry))]


def _softmax_init(rows, dv):
    return (jnp.full((rows, 1), MASK_VALUE, F32), jnp.zeros((rows, 1), F32), jnp.zeros((rows, dv), F32))


def _causal_attn_kernel(q_ref, k_ref, v_ref, o_ref, *, tq, tk, dk, dv, hpg):
    i = pl.program_id(2)
    n_full = (i * tq) // tk
    heads = range(hpg)
    q = [q_ref[:, h * dk:(h + 1) * dk] for h in heads]

    def tile(j, carry, mask):
        off = pl.multiple_of(j * tk, tk)
        s = [_nt_dot(q[h], k_ref[pl.ds(off, tk), h * dk:(h + 1) * dk]) for h in heads]
        if mask is not None:
            s = [jnp.where(mask, sh, MASK_VALUE) for sh in s]
        return _softmax_tiles(s, lambda h: v_ref[pl.ds(off, tk), h * dv:(h + 1) * dv], carry)

    carry = lax.fori_loop(0, n_full, lambda j, c: tile(j, c, None), [_softmax_init(tq, dv) for _ in heads])
    qpos = i * tq + lax.broadcasted_iota(jnp.int32, (tq, tk), 0)
    kpos = n_full * tk + lax.broadcasted_iota(jnp.int32, (tq, tk), 1)
    carry = tile(n_full, carry, kpos <= qpos)
    for h in heads:
        _, l, acc = carry[h]
        o_ref[:, h * dv:(h + 1) * dv] = (acc / l).astype(o_ref.dtype)


def causal_attention(q, k, v, batch, heads, dk, dv, tq=256, tk=1024, hpg=3):
    t = q.shape[0]
    s = t // batch
    nq = s // tq
    assert tk % tq == 0 and s % tk == 0 and heads % hpg == 0
    est = 4 * hpg * (_nbytes((s, dk), BF16) + _nbytes((s, dv), BF16)) + 8 * hpg * _nbytes((tq, tk), F32)
    return pl.pallas_call(
        functools.partial(_causal_attn_kernel, tq=tq, tk=tk, dk=dk, dv=dv, hpg=hpg),
        grid=(batch, heads // hpg, nq),
        in_specs=[pl.BlockSpec((tq, hpg * dk), lambda b, g, i: (b * nq + i, g)),
                  pl.BlockSpec((s, hpg * dk), lambda b, g, i: (b, g)),
                  pl.BlockSpec((s, hpg * dv), lambda b, g, i: (b, g))],
        out_specs=pl.BlockSpec((tq, hpg * dv), lambda b, g, i: (b * nq + i, g)),
        out_shape=jax.ShapeDtypeStruct((t, heads * dv), BF16),
        compiler_params=_params(("parallel", "parallel", "arbitrary"), est),
    )(q, k, v)


def _moba_select(gate_t, i, nb):
    nq = gate_t.shape[1]
    row = lax.broadcasted_iota(jnp.int32, (nb, nq), 0)
    rank = jnp.zeros((nb, nq), jnp.int32)
    for jj in range(nb):
        gj = gate_t[jj:jj + 1, :]
        ahead = jnp.logical_or(gj > gate_t, jnp.logical_and(gj == gate_t, jj < row))
        rank = rank + jnp.where(jnp.logical_and(ahead, jj < i), 1, 0)
    return jnp.where(jnp.logical_and(row < i, rank < MOBA_TOPK), 1.0, 0.0)


def _moba_kernel(q_ref, k_ref, v_ref, cq_ref, sq_ref, ck_ref, sk_ref, o_ref, kr_scr, vt_scr, km_scr, sel_scr,
                 *, nb, nh, scale):
    i = pl.program_id(1)
    blk = MOBA_BLOCK
    d = HEAD_DIM
    heads = range(nh)

    @pl.when(i == 0)
    def _():
        def prep(j, _):
            rows = pl.ds(pl.multiple_of(j * blk, blk), blk)
            c, s = ck_ref[rows, :], sk_ref[rows, :]
            for h in heads:
                kj = _half_swap(k_ref[rows, h * d:(h + 1) * d].astype(F32), c, s)
                km_scr[pl.ds(h * nb + j, 1), :] = jnp.mean(kj, axis=0, keepdims=True)
                kr_scr[rows, h * d:(h + 1) * d] = kj.astype(BF16)
                vt_scr[j, h * d:(h + 1) * d, :] = _transposed_bf16(v_ref[rows, h * d:(h + 1) * d])
            return 0
        lax.fori_loop(0, nb, prep, 0)

    cq, sq = cq_ref[...], sq_ref[...]
    qb = []
    for h in heads:
        q = _half_swap(q_ref[:, h * d:(h + 1) * d].astype(F32), cq, sq)
        gate_t = lax.dot_general(km_scr[h * nb:(h + 1) * nb, :], q, (((1,), (1,)), ((), ())),
                                 precision=lax.Precision.HIGHEST, preferred_element_type=F32)
        sel_scr[h * nb:(h + 1) * nb, :] = _moba_select(gate_t, i, nb)
        qb.append((q * scale).astype(BF16))

    def tile(j, nblk, carry, bias_fn):
        rows = pl.ds(pl.multiple_of(j * blk, blk), nblk * blk)
        s = [_nt_dot(kr_scr[rows, h * d:(h + 1) * d], qb[h]) + bias_fn(h) for h in heads]
        return _softmax_tiles_keymajor(
            s, lambda h: [vt_scr[j + n, h * d:(h + 1) * d, :] for n in range(nblk)], carry)

    key_id = lax.broadcasted_iota(jnp.int32, (blk, blk), 0)
    qry_id = lax.broadcasted_iota(jnp.int32, (blk, blk), 1)
    causal = jnp.where(key_id <= qry_id, 0.0, MASK_VALUE)
    carry = tile(i, 1, [_softmax_init_keymajor(blk, d) for _ in heads], lambda h: causal)

    def body(p, carry):
        def bias(h):
            rows = [jnp.broadcast_to((1.0 - sel_scr[pl.ds(h * nb + 4 * p + n, 1), :]) * MASK_VALUE, (blk, blk))
                    for n in range(4)]
            return jnp.concatenate(rows, axis=0)
        return tile(4 * p, 4, carry, bias)

    carry = lax.fori_loop(0, (i + 3) // 4, body, carry)
    for h in heads:
        _, l, acc = carry[h]
        o_ref[:, h * d:(h + 1) * d] = (acc / l).T.astype(o_ref.dtype)


def moba_attention(zbc, cos_t, sin_t, batch, group0):
    t = zbc.shape[0]
    s = t // batch
    blk = MOBA_BLOCK
    nb = s // blk
    nh = MOBA_HEADS
    w = nh * HEAD_DIM
    est = 6 * _nbytes((s, w), BF16) + 4 * _nbytes((s, HEAD_DIM), F32) + 12 * nh * _nbytes((blk, blk), F32)
    return pl.pallas_call(
        functools.partial(_moba_kernel, nb=nb, nh=nh, scale=float(HEAD_DIM ** -0.5)),
        grid=(batch, nb),
        in_specs=[pl.BlockSpec((blk, w), lambda b, i: (b * nb + i, group0)),
                  pl.BlockSpec((s, w), lambda b, i: (b, group0 + 1)),
                  pl.BlockSpec((s, w), lambda b, i: (b, group0 + 2)),
                  pl.BlockSpec((blk, HEAD_DIM), lambda b, i: (b * nb + i, 0)),
                  pl.BlockSpec((blk, HEAD_DIM), lambda b, i: (b * nb + i, 0)),
                  pl.BlockSpec((s, HEAD_DIM), lambda b, i: (b, 0)),
                  pl.BlockSpec((s, HEAD_DIM), lambda b, i: (b, 0))],
        out_specs=pl.BlockSpec((blk, w), lambda b, i: (b * nb + i, 0)),
        out_shape=jax.ShapeDtypeStruct((t, w), BF16),
        scratch_shapes=[pltpu.VMEM((s, w), BF16), pltpu.VMEM((nb, w, blk), BF16),
                        pltpu.VMEM((nh * nb, HEAD_DIM), F32), pltpu.VMEM((nh * nb, blk), F32)],
        compiler_params=_params(("parallel", "arbitrary"), est),
    )(zbc, zbc, zbc, cos_t, sin_t, cos_t, sin_t)


def _sb_kernel(q_ref, k_ref, v_ref, u_ref, o_ref, *, tq, nh, scale):
    i = pl.program_id(1)
    d = HEAD_DIM
    heads = range(nh)
    u = u_ref[...]
    qb = [(q_ref[:, h * d:(h + 1) * d].astype(F32) * scale).astype(BF16) for h in heads]

    def tile(j, carry, strict):
        off = pl.multiple_of(j * tq, tq)
        z = [_nt_dot(qb[h], k_ref[pl.ds(off, tq), h * d:(h + 1) * d]) for h in heads]
        hi, lo = [], []
        for h in heads:
            lsm = -(jnp.maximum(z[h], 0.0) + jnp.log(1.0 + jnp.exp(-jnp.abs(z[h]))))
            if strict is not None:
                lsm = jnp.where(strict, lsm, 0.0)
            lsm_hi, lsm_lo = _split_bf16(lsm)
            hi.append(lsm_hi)
            lo.append(lsm_lo)
        incl = [jnp.dot(hi[h], u, preferred_element_type=F32) + jnp.dot(lo[h], u, preferred_element_type=F32)
                for h in heads]
        a = []
        for h in heads:
            ah = jnp.exp(jnp.minimum(z[h] + incl[h], 0.0) + carry[h][0])
            if strict is not None:
                ah = jnp.where(strict, ah, 0.0)
            a.append(ah.astype(BF16))
        return [(carry[h][0] + incl[h][:, 0:1],
                 carry[h][1] + jnp.dot(a[h], v_ref[pl.ds(off, tq), h * d:(h + 1) * d], preferred_element_type=F32))
                for h in heads]

    r_id = lax.broadcasted_iota(jnp.int32, (tq, tq), 0)
    c_id = lax.broadcasted_iota(jnp.int32, (tq, tq), 1)
    init = [(jnp.zeros((tq, 1), F32), jnp.zeros((tq, d), F32)) for _ in heads]
    carry = tile(i, init, c_id < r_id)

    def live(carry):
        worst = carry[0][0]
        for h in heads[1:]:
            worst = jnp.maximum(worst, carry[h][0])
        return (jnp.max(worst) > EXP_UNDERFLOW).astype(jnp.int32)

    def body(state):
        n, _, carry = state
        carry = tile(i - 1 - n, carry, None)
        return n + 1, live(carry), carry

    _, _, carry = lax.while_loop(lambda st: jnp.logical_and(st[0] < i, st[1] > 0), body,
                                 (jnp.int32(0), live(carry), carry))
    for h in heads:
        o_ref[:, h * d:(h + 1) * d] = carry[h][1].astype(o_ref.dtype)


def stick_breaking_attention(zbc, batch, group0, tq=256):
    t = zbc.shape[0]
    s = t // batch
    nq = s // tq
    nh = SB_HEADS
    w = nh * HEAD_DIM
    u = (jnp.arange(tq)[:, None] >= jnp.arange(tq)[None, :]).astype(BF16)
    est = 8 * _nbytes((s, w), BF16) + 16 * nh * _nbytes((tq, tq), F32)
    return pl.pallas_call(
        functools.partial(_sb_kernel, tq=tq, nh=nh, scale=float(HEAD_DIM ** -0.5)),
        grid=(batch, nq),
        in_specs=[pl.BlockSpec((tq, w), lambda b, i: (b * nq + i, group0)),
                  pl.BlockSpec((s, w), lambda b, i: (b, group0 + 1)),
                  pl.BlockSpec((s, w), lambda b, i: (b, group0 + 2)),
                  pl.BlockSpec((tq, tq), lambda b, i: (0, 0))],
        out_specs=pl.BlockSpec((tq, w), lambda b, i: (b * nq + i, 0)),
        out_shape=jax.ShapeDtypeStruct((t, w), BF16),
        compiler_params=_params(("parallel", "arbitrary"), est),
    )(zbc, zbc, zbc, u)


def _merge_kernel(oa_ref, ob_ref, oc_ref, ga_ref, gb_ref, gc_ref, wa_ref, wb_ref, wc_ref, o_ref):
    def branch(o, g, w):
        return jax.nn.sigmoid(g[...].astype(F32)) * jnp.dot(o[...], w[...], preferred_element_type=F32)
    o_ref[...] = (branch(oa_ref, ga_ref, wa_ref) + branch(ob_ref, gb_ref, wb_ref)
                  + branch(oc_ref, gc_ref, wc_ref)).astype(o_ref.dtype)


def gated_merge(o_a, o_b, o_c, zg, wa, wb, wc, tm=512):
    m = o_a.shape[0]
    d = wa.shape[1]
    row = lambda a: pl.BlockSpec((tm, a.shape[1]), lambda i: (i, 0))
    full = lambda a: pl.BlockSpec(a.shape, lambda i: (0, 0))
    gate = lambda n: pl.BlockSpec((tm, d), lambda i: (i, n))
    est = 4 * _nbytes((d, d), BF16) + 12 * _nbytes((tm, d), F32)
    return pl.pallas_call(
        _merge_kernel,
        grid=(m // tm,),
        in_specs=[row(o_a), row(o_b), row(o_c), gate(0), gate(1), gate(2), full(wa), full(wb), full(wc)],
        out_specs=pl.BlockSpec((tm, d), lambda i: (i, 0)),
        out_shape=jax.ShapeDtypeStruct((m, d), BF16),
        compiler_params=_params(("parallel",), est),
    )(o_a, o_b, o_c, zg, zg, zg, wa, wb, wc)


def _xattn_kernel(q_ref, k_ref, v_ref, o_ref, *, heads, scale):
    hd = q_ref.shape[1] // heads
    for h in range(heads):
        cols = slice(h * hd, (h + 1) * hd)
        s = _nt_dot(q_ref[:, cols], k_ref[:, cols]) * scale
        p = jnp.exp(s - jnp.max(s, axis=1, keepdims=True))
        o = jnp.dot(p.astype(BF16), v_ref[:, cols], preferred_element_type=F32)
        o_ref[:, cols] = (o / jnp.sum(p, axis=1, keepdims=True)).astype(o_ref.dtype)


def cross_attention(q, k, v, batch, heads, tq=512):
    t, d = q.shape
    s = t // batch
    mlen = k.shape[0] // batch
    nq = s // tq
    est = 8 * _nbytes((tq, d), BF16) + 8 * _nbytes((mlen, d), BF16) + 8 * _nbytes((tq, mlen), F32)
    return pl.pallas_call(
        functools.partial(_xattn_kernel, heads=heads, scale=float((d // heads) ** -0.5)),
        grid=(batch, nq),
        in_specs=[pl.BlockSpec((tq, d), lambda b, i: (b * nq + i, 0)),
                  pl.BlockSpec((mlen, d), lambda b, i: (b, 0)),
                  pl.BlockSpec((mlen, d), lambda b, i: (b, 0))],
        out_specs=pl.BlockSpec((tq, d), lambda b, i: (b * nq + i, 0)),
        out_shape=jax.ShapeDtypeStruct((t, d), BF16),
        compiler_params=_params(("parallel", "parallel"), est),
    )(q, k, v)


def _swiglu_up(x_ref, w1_ref, w3_ref, o_ref):
    x = x_ref[...]
    a = jnp.dot(x, w1_ref[...], preferred_element_type=F32)
    b = jnp.dot(x, w3_ref[...], preferred_element_type=F32)
    o_ref[...] = (a * jax.nn.sigmoid(a) * b).astype(BF16)


def _ffn_up_kernel(x_ref, w1_ref, w3_ref, o_ref):
    _swiglu_up(x_ref, w1_ref, w3_ref, o_ref)


def dense_ffn(x, w1, w3, w2, res, g_next, tm_up=1024, tf=512, tm_down=256):
    m, d = x.shape
    f = w1.shape[1]
    assert f % tf == 0
    est = 2 * (_nbytes((tm_up, d), BF16) + 2 * _nbytes((d, tf), BF16) + _nbytes((tm_up, tf), BF16)) \
        + 4 * _nbytes((tm_up, tf), F32)
    act = pl.pallas_call(
        _ffn_up_kernel,
        grid=(m // tm_up, f // tf),
        in_specs=[pl.BlockSpec((tm_up, d), lambda i, j: (i, 0)),
                  pl.BlockSpec((d, tf), lambda i, j: (0, j)),
                  pl.BlockSpec((d, tf), lambda i, j: (0, j))],
        out_specs=pl.BlockSpec((tm_up, tf), lambda i, j: (i, j)),
        out_shape=jax.ShapeDtypeStruct((m, f), BF16),
        compiler_params=_params(("parallel", "parallel"), est),
    )(x, w1, w3)
    return matmul_res_norm(act, w2, res, g_next, tm=tm_down)


def _split_bf16(x):
    hi = x.astype(BF16)
    return hi, (x - hi.astype(F32)).astype(BF16)


def _router_kernel(h_ref, g_ref, wh_ref, wl_ref, u_ref, idx_ref, gate_ref, rank_ref, count_ref, seen_scr):
    @pl.when(pl.program_id(0) == 0)
    def _():
        seen_scr[...] = jnp.zeros_like(seen_scr)

    hn = _rms(h_ref[...], g_ref[...])
    n_e = count_ref.shape[0]
    hn_hi, hn_lo = _split_bf16(hn)
    logits = (_nt_dot(hn_hi, wh_ref[...]) + _nt_dot(hn_lo, wh_ref[...]) + _nt_dot(hn_hi, wl_ref[...])).T[:n_e, :]
    e_id = lax.broadcasted_iota(jnp.int32, logits.shape, 0)
    v1 = jnp.max(logits, axis=0, keepdims=True)
    i1 = jnp.min(jnp.where(logits == v1, e_id, n_e), axis=0, keepdims=True)
    rest = jnp.where(e_id == i1, -jnp.inf, logits)
    v2 = jnp.max(rest, axis=0, keepdims=True)
    i2 = jnp.min(jnp.where(rest == v2, e_id, n_e), axis=0, keepdims=True)
    e2 = jnp.exp(v2 - v1)
    idx_ref[0:1, :] = i1
    idx_ref[1:2, :] = i2
    gate_ref[0:1, :] = 1.0 / (1.0 + e2)
    gate_ref[1:2, :] = e2 / (1.0 + e2)
    pick1 = jnp.where(e_id == i1, 1.0, 0.0)
    pick2 = jnp.where(e_id == i2, 1.0, 0.0)
    both = pick1 + pick2
    earlier = jnp.dot(both.astype(BF16), u_ref[...], preferred_element_type=F32) + seen_scr[...]
    rank_ref[0:1, :] = jnp.sum(pick1 * earlier, axis=0, keepdims=True).astype(jnp.int32)
    rank_ref[1:2, :] = jnp.sum(pick2 * earlier, axis=0, keepdims=True).astype(jnp.int32)
    seen_scr[...] += jnp.sum(both, axis=1, keepdims=True)
    count_ref[...] = seen_scr[...].astype(jnp.int32)


def moe_router(h, g, w_router, tm=512):
    m, d = h.shape
    n_e = w_router.shape[1]
    wr_t = jnp.zeros((LANES, d), F32).at[:n_e].set(w_router.T.astype(F32))
    wr_hi, wr_lo = _split_bf16(wr_t)
    u = (jnp.arange(tm)[:, None] < jnp.arange(tm)[None, :]).astype(BF16)
    est = 6 * _nbytes((tm, d), F32)
    return pl.pallas_call(
        _router_kernel,
        grid=(m // tm,),
        in_specs=[pl.BlockSpec((tm, d), lambda i: (i, 0)),
                  pl.BlockSpec((1, d), lambda i: (0, 0)),
                  pl.BlockSpec((LANES, d), lambda i: (0, 0)),
                  pl.BlockSpec((LANES, d), lambda i: (0, 0)),
                  pl.BlockSpec((tm, tm), lambda i: (0, 0))],
        out_specs=[pl.BlockSpec((MOE_TOPK, tm), lambda i: (0, i)),
                   pl.BlockSpec((MOE_TOPK, tm), lambda i: (0, i)),
                   pl.BlockSpec((MOE_TOPK, tm), lambda i: (0, i)),
                   pl.BlockSpec((n_e, 1), lambda i: (0, 0))],
        out_shape=[jax.ShapeDtypeStruct((MOE_TOPK, m), jnp.int32), jax.ShapeDtypeStruct((MOE_TOPK, m), F32),
                   jax.ShapeDtypeStruct((MOE_TOPK, m), jnp.int32), jax.ShapeDtypeStruct((n_e, 1), jnp.int32)],
        scratch_shapes=[pltpu.VMEM((n_e, 1), F32)],
        compiler_params=_params(("arbitrary",), est),
    )(h, g.reshape(1, d).astype(F32), wr_hi, wr_lo, u)


def _row_copy(src_hbm, src_row, dst_ref, dst_row, sem):
    return pltpu.make_async_copy(src_hbm.at[pl.ds(src_row, 1)], dst_ref.at[pl.ds(dst_row, 1)], sem)


PREFETCH_ROWS = 128


def _expert_up_kernel(te_ref, tv_ref, rows_ref, h_hbm, g_ref, w1_ref, w3_ref, o_ref, xbuf, xn_ref, sem):
    t = pl.program_id(0)
    j = pl.program_id(1)
    tm = MOE_TILE
    slot = lax.rem(t, 2)

    def request(tile, first, count):
        dst, dsem = xbuf.at[lax.rem(tile, 2)], sem.at[lax.rem(tile, 2)]

        def start(r, _):
            _row_copy(h_hbm, rows_ref[tile * tm + first + r], dst, first + r, dsem).start()
            return 0
        lax.fori_loop(0, count, start, 0, unroll=8)

    @pl.when(jnp.logical_and(t == 0, j == 0))
    def _():
        request(0, 0, tm)

    nxt = jnp.minimum(t + 1, pl.num_programs(0) - 1)
    @pl.when(jnp.logical_and(jnp.logical_and(t + 1 < pl.num_programs(0), tv_ref[nxt] > 0),
                             j < tm // PREFETCH_ROWS))
    def _():
        request(t + 1, j * PREFETCH_ROWS, PREFETCH_ROWS)

    @pl.when(jnp.logical_and(j == 0, tv_ref[t] > 0))
    def _():
        def wait(r, _):
            _row_copy(h_hbm, 0, xbuf.at[slot], r, sem.at[slot]).wait()
            return 0
        lax.fori_loop(0, tm, wait, 0, unroll=8)
        xn_ref[...] = _rms(xbuf[slot], g_ref[...]).astype(BF16)

    @pl.when(tv_ref[t] > 0)
    def _():
        _swiglu_up(xn_ref, w1_ref, w3_ref, o_ref)

    @pl.when(tv_ref[t] == 0)
    def _():
        o_ref[...] = jnp.zeros_like(o_ref)


def _expert_down_kernel(te_ref, tv_ref, a_ref, w_ref, o_ref):
    t = pl.program_id(0)

    @pl.when(pl.program_id(1) == 0)
    def _():
        o_ref[...] = jnp.zeros_like(o_ref)

    @pl.when(tv_ref[t] > 0)
    def _():
        o_ref[...] += jnp.dot(a_ref[...], w_ref[...], preferred_element_type=F32)


def expert_ffn(h, slot_tok, g, w1, w3, w2, tile_e, tile_valid, tf=1024, nk=4):
    d = h.shape[1]
    n = slot_tok.shape[0]
    f = w1.shape[2]
    nf = f // tf
    tk = f // nk
    tm = MOE_TILE
    assert tm % PREFETCH_ROWS == 0 and nf >= tm // PREFETCH_ROWS
    assert f % tf == 0 and f % nk == 0 and tk % LANES == 0

    def hold(last):
        return lambda t, j, tv: j * tv[t] + last * (1 - tv[t])

    ju = hold(nf - 1)
    est = 2 * (_nbytes((tm, d), F32) + 2 * _nbytes((d, tf), BF16) + _nbytes((tm, tf), BF16)) \
        + _nbytes((tm, d), BF16) + 4 * _nbytes((tm, tf), F32) + _nbytes((tm, d), F32)
    act = pl.pallas_call(
        _expert_up_kernel,
        grid_spec=pltpu.PrefetchScalarGridSpec(
            num_scalar_prefetch=3,
            grid=(n // tm, nf),
            in_specs=[pl.BlockSpec(memory_space=pl.ANY),
                      pl.BlockSpec((1, d), lambda t, j, te, tv, rows: (0, 0)),
                      pl.BlockSpec((None, d, tf), lambda t, j, te, tv, rows: (te[t], 0, ju(t, j, tv))),
                      pl.BlockSpec((None, d, tf), lambda t, j, te, tv, rows: (te[t], 0, ju(t, j, tv)))],
            out_specs=pl.BlockSpec((tm, tf), lambda t, j, te, tv, rows: (t, j)),
            scratch_shapes=[pltpu.VMEM((2, tm, d), F32), pltpu.VMEM((tm, d), BF16),
                            pltpu.SemaphoreType.DMA((2,))],
        ),
        out_shape=jax.ShapeDtypeStruct((n, f), BF16),
        compiler_params=_params(("arbitrary", "arbitrary"), est),
    )(tile_e, tile_valid, slot_tok, h, g.reshape(1, d).astype(F32), w1, w3)
    jd = hold(nk - 1)
    est = 2 * (_nbytes((tm, tk), BF16) + _nbytes((tk, d), BF16) + _nbytes((tm, d), F32)) + 2 * _nbytes((tm, d), F32)
    return pl.pallas_call(
        _expert_down_kernel,
        grid_spec=pltpu.PrefetchScalarGridSpec(
            num_scalar_prefetch=2,
            grid=(n // tm, nk),
            in_specs=[pl.BlockSpec((tm, tk), lambda t, k, te, tv: (t, jd(t, k, tv))),
                      pl.BlockSpec((None, tk, d), lambda t, k, te, tv: (te[t], jd(t, k, tv), 0))],
            out_specs=pl.BlockSpec((tm, d), lambda t, k, te, tv: (t, 0)),
        ),
        out_shape=jax.ShapeDtypeStruct((n, d), F32),
        compiler_params=_params(("arbitrary", "arbitrary"), est),
    )(tile_e, tile_valid, act, w2)


def _combine_kernel(slots_ref, ys_hbm, h_ref, gate_ref, g_ref, o_ref, buf0, buf1, sem):
    base = pl.program_id(0) * GATHER_ROWS

    def start(r, _):
        flat = (base + r) * MOE_TOPK
        _row_copy(ys_hbm, slots_ref[flat], buf0, r, sem).start()
        _row_copy(ys_hbm, slots_ref[flat + 1], buf1, r, sem).start()
        return 0

    def wait(r, _):
        _row_copy(ys_hbm, 0, buf0, r, sem).wait()
        _row_copy(ys_hbm, 0, buf1, r, sem).wait()
        return 0

    lax.fori_loop(0, GATHER_ROWS, start, 0, unroll=8)
    lax.fori_loop(0, GATHER_ROWS, wait, 0, unroll=8)
    gate = gate_ref[...]
    h = h_ref[...] + gate[:, 0:1] * buf0[...] + gate[:, 1:2] * buf1[...]
    o_ref[...] = _rms(h, g_ref[...])


def moe_combine_norm(ys, slot_of_flat, h, gates, g_out):
    m, d = h.shape
    tm = GATHER_ROWS
    return pl.pallas_call(
        _combine_kernel,
        grid_spec=pltpu.PrefetchScalarGridSpec(
            num_scalar_prefetch=1,
            grid=(m // tm,),
            in_specs=[pl.BlockSpec(memory_space=pl.ANY),
                      pl.BlockSpec((tm, d), lambda i, s: (i, 0)),
                      pl.BlockSpec((tm, MOE_TOPK), lambda i, s: (i, 0)),
                      pl.BlockSpec((1, d), lambda i, s: (0, 0))],
            out_specs=pl.BlockSpec((tm, d), lambda i, s: (i, 0)),
            scratch_shapes=[pltpu.VMEM((tm, d), F32), pltpu.VMEM((tm, d), F32), pltpu.SemaphoreType.DMA(())],
        ),
        out_shape=jax.ShapeDtypeStruct((m, d), F32),
        compiler_params=_params(("arbitrary",), 8 * _nbytes((tm, d), F32)),
    )(slot_of_flat, ys, h, gates, g_out.reshape(1, d).astype(F32))


def moe_routing_metadata(top_idx, top_rank, counts, n_tokens):
    tk = n_tokens * MOE_TOPK
    flat_e = top_idx.T.reshape(-1)
    within = top_rank.T.reshape(-1)
    padded = (counts + MOE_TILE - 1) // MOE_TILE * MOE_TILE
    pad_end = jnp.cumsum(padded)
    pad_start = pad_end - padded
    slot_of_flat = (jnp.sum(jnp.where(flat_e[:, None] == jnp.arange(N_EXPERTS)[None, :], pad_start[None, :], 0),
                            axis=1) + within).astype(jnp.int32)
    n_tiles = tk // MOE_TILE + N_EXPERTS
    slot_tok = jnp.zeros((n_tiles * MOE_TILE,), jnp.int32).at[slot_of_flat].set(
        jnp.arange(tk, dtype=jnp.int32) // MOE_TOPK)
    tile_start = jnp.arange(n_tiles, dtype=jnp.int32) * MOE_TILE
    tile_valid = (tile_start < pad_end[-1]).astype(jnp.int32)
    tile_e = jnp.minimum(jnp.sum(tile_start[:, None] >= pad_end[None, :], axis=1), N_EXPERTS - 1)
    last_e = jnp.max(jnp.where(tile_valid > 0, tile_e, 0))
    tile_e = jnp.where(tile_valid > 0, tile_e, last_e).astype(jnp.int32)
    return slot_tok, slot_of_flat, tile_e, tile_valid


def _rope_partner(w):
    half = w.shape[-1] // 2
    return jnp.concatenate([-w[..., half:], w[..., :half]], axis=-1)


def _rope_tables(positions):
    pos = positions.astype(F32).reshape(-1, 1)

    def cs(half):
        inv_freq = ROPE_THETA ** (-jnp.arange(half, dtype=F32) / half)
        ang = pos * inv_freq
        return jnp.cos(ang), jnp.sin(ang)

    c32, s32 = cs(MLA_ROPE // 2)
    zeros = jnp.zeros((pos.shape[0], LANES // 2), F32)
    mla_c = jnp.concatenate([c32, c32, zeros], axis=1)
    mla_s = jnp.concatenate([s32, s32, zeros], axis=1)
    c64, s64 = cs(HEAD_DIM // 2)
    moba_c = jnp.concatenate([c64, c64], axis=1)
    moba_s = jnp.concatenate([-s64, s64], axis=1)
    return mla_c, mla_s, moba_c, moba_s


def _layer_weights(w_in, w_uq, w_ukv):
    o = 0
    w_cq = w_in[:, o:o + MLA_LORA]; o += MLA_LORA
    w_ckv = w_in[:, o:o + MLA_LORA]; o += MLA_LORA
    w_kr = w_in[:, o:o + MLA_ROPE]; o += MLA_ROPE
    nbc = 3 * (MOBA_HEADS + SB_HEADS) * HEAD_DIM
    w_bc = w_in[:, o:o + nbc]; o += nbc
    w_g = w_in[:, o:]
    w1 = jnp.concatenate([w_cq, w_ckv, w_kr, _rope_partner(w_kr)], axis=1)
    uq = w_uq.reshape(MLA_LORA, MLA_HEADS, MLA_NOPE + MLA_ROPE)
    uq_rope = uq[..., MLA_NOPE:]
    wq = jnp.concatenate([uq[..., :MLA_NOPE], uq_rope, _rope_partner(uq_rope)], axis=-1)
    wq = wq.reshape(MLA_LORA, MLA_HEADS * 2 * LANES)
    ukv = w_ukv.reshape(MLA_LORA, MLA_HEADS, MLA_NOPE + MLA_V)
    wkv = jnp.concatenate([ukv[..., :MLA_NOPE].reshape(MLA_LORA, -1), ukv[..., MLA_NOPE:].reshape(MLA_LORA, -1)],
                          axis=1)
    return w1, w_bc, w_g, wq, wkv


def kernel(x, mem, positions, g_mix, w_in, g_cq, g_ckv, w_uq, w_ukv, w_up_a, w_up_b, w_up_c, w_o, g_x, g_mem, w_xq, w_xk, w_xv, w_xo, g_ffn, w_ff1, w_ff3, w_ff2, w_router, w_e1, w_e3, w_e2, g_final):
    batch, seq, d = x.shape
    t = batch * seq
    depth = g_mix.shape[0]
    assert depth == 2, "the final norm is fused into the expert layer, which must come last"
    mem2 = mem.reshape(-1, d)
    mla_c, mla_s, moba_c, moba_s = _rope_tables(positions)
    assert MOBA_HEADS == SB_HEADS

    w_in, w_uq, w_ukv = w_in.astype(BF16), w_uq.astype(BF16), w_ukv.astype(BF16)

    h = x.reshape(t, d)
    hn = rmsnorm(h, g_mix[0], BF16)
    out = None
    for l in range(depth):
        w1, w_bc, w_g, wq, wkv = _layer_weights(w_in[l], w_uq[l], w_ukv[l])
        z1 = matmul(hn, w1, BF16, tm=2048, tn=w1.shape[1])
        zbc = matmul(hn, w_bc, BF16, tm=2048, tn=768)
        zg = matmul(hn, w_g, BF16, tm=2048, tn=1024)
        q_a, k_a, v_a = mla_prep(z1, g_cq[l], g_ckv[l], wq, wkv, mla_c, mla_s)
        o_a = causal_attention(q_a, k_a, v_a, batch, MLA_HEADS, 2 * LANES, MLA_V)
        o_b = moba_attention(zbc, moba_c, moba_s, batch, 0)
        o_c = stick_breaking_attention(zbc, batch, 3)
        merged = gated_merge(o_a, o_b, o_c, zg, w_up_a[l].astype(BF16), w_up_b[l].astype(BF16),
                             w_up_c[l].astype(BF16))
        h, hn = matmul_res_norm(merged, w_o[l].astype(BF16), h, g_x[l])
        mn = rmsnorm(mem2, g_mem[l], BF16)
        q_x = matmul(hn, w_xq[l].astype(BF16), BF16, tm=2048)
        k_x = matmul(mn, w_xk[l].astype(BF16), BF16)
        v_x = matmul(mn, w_xv[l].astype(BF16), BF16)
        o_x = cross_attention(q_x, k_x, v_x, batch, XATTN_HEADS)
        h, hn = matmul_res_norm(o_x, w_xo[l].astype(BF16), h, g_ffn[l])
        g_next = g_mix[l + 1] if l + 1 < depth else g_final
        if l % 2 == 0:
            e = l // 2
            h, hn = dense_ffn(hn, w_ff1[e].astype(BF16), w_ff3[e].astype(BF16), w_ff2[e].astype(BF16), h, g_next)
            out = hn
        else:
            e = l // 2
            top_idx, top_gate, top_rank, counts = moe_router(h, g_ffn[l], w_router[e])
            slot_tok, slot_of_flat, tile_e, tile_valid = moe_routing_metadata(top_idx, top_rank, counts[:, 0], t)
            ys = expert_ffn(h, slot_tok, g_ffn[l],w_e1[e].astype(BF16), w_e3[e].astype(BF16), w_e2[e].astype(BF16),
                            tile_e, tile_valid)
            out = moe_combine_norm(ys, slot_of_flat, h, top_gate.T, g_next)
    return out.reshape(batch, seq, d).astype(x.dtype)
```
